```python
import math
import jax
import jax.numpy as jnp
from jax import lax
import numpy as np

D_MODEL = 1024
BATCH = 8
SEQ = 2048
DEPTH = 4

GRID_W = 64
CTX_LEN = 256
N_MIXERS = 3
D_FF = 4 * D_MODEL
NORM_EPS = 1e-6
ROPE_BASE = 10000.0

A_HEADS = 8
A_DQK = D_MODEL // (2 * A_HEADS)
A_DV = D_MODEL // A_HEADS
A_CHUNK = 128
A_IN = 2 * A_HEADS * A_DQK + A_HEADS * A_DV + D_MODEL + 4 * A_HEADS

SWA_HEADS = 16
SWA_KV_HEADS = 4
SWA_DH = D_MODEL // SWA_HEADS
SWA_GROUP = SWA_HEADS // SWA_KV_HEADS
WINDOW = 128
SWA_BLOCK = WINDOW
SWA_IN = (SWA_HEADS + 2 * SWA_KV_HEADS) * SWA_DH

DIFF_HEADS = 8
DIFF_DH = D_MODEL // (2 * DIFF_HEADS)
DIFF_DV = 2 * DIFF_DH
DIFF_BLOCK = 128
DIFF_IN = 4 * DIFF_HEADS * DIFF_DH + DIFF_HEADS * DIFF_DV

N_MLSTM = (DEPTH + N_MIXERS - 1) // N_MIXERS
N_SWA = (DEPTH + N_MIXERS - 2) // N_MIXERS
N_DIFF = (DEPTH + N_MIXERS - 3) // N_MIXERS

kernel_name = "hybrid_mlstm_swa_diffattn_prefix_trunk"

F32 = jnp.float32


def _rms_norm(x, g):
    xf = x.astype(F32)
    y = xf * lax.rsqrt(jnp.mean(xf * xf, axis=-1, keepdims=True) + NORM_EPS)
    return (y * g.astype(F32)).astype(x.dtype)


def _modulate(x, shift, scale):
    return x * (1.0 + scale) + shift


def _sq_relu_mlp(u, w1, w2):
    return jnp.square(jax.nn.relu(u @ w1)) @ w2


def _axial_rope_tables(n_tok, head_dim):
    rows = n_tok // GRID_W
    row = jnp.repeat(jnp.arange(rows, dtype=jnp.int32), GRID_W).astype(F32)
    col = jnp.tile(jnp.arange(GRID_W, dtype=jnp.int32), rows).astype(F32)
    quarter = head_dim // 4
    inv = ROPE_BASE ** (-jnp.arange(quarter, dtype=F32) / quarter)
    ang = jnp.concatenate([row[:, None] * inv, col[:, None] * inv], axis=-1)
    return jnp.cos(ang), jnp.sin(ang)


def _apply_rope(x, cos, sin):
    half = x.shape[-1] // 2
    x1, x2 = x[..., :half], x[..., half:]
    cs = cos[:, None, :].astype(x.dtype)
    sn = sin[:, None, :].astype(x.dtype)
    return jnp.concatenate([x1 * cs - x2 * sn, x1 * sn + x2 * cs], axis=-1)


def _to_chunks(t):
    bsz, n, h = t.shape[:3]
    t = t.reshape((bsz, n // A_CHUNK, A_CHUNK, h) + t.shape[3:])
    return jnp.moveaxis(t, 3, 1)


def _from_chunks(t):
    t = jnp.moveaxis(t, 1, 3)
    return t.reshape((t.shape[0], t.shape[1] * t.shape[2]) + t.shape[3:])


def _mlstm_direction(q, k, v, log_i, log_f, state, need_out):
    k_c, v_c = _to_chunks(k), _to_chunks(v)
    li, lf = _to_chunks(log_i), _to_chunks(log_f)
    b = jnp.cumsum(lf, axis=-1)
    b_last = b[..., -1]
    a = b_last[..., None] - b + li
    g = jnp.max(a, axis=-1)
    w = jnp.exp(a - g[..., None])
    kv = jnp.einsum("bhcs,bhcsd,bhcse->bhcde", w, k_c, v_c)
    kn = jnp.einsum("bhcs,bhcsd->bhcd", w, k_c)

    def step(carry, xs):
        c_st, n_st, m_st = carry
        bl, gj, kvj, knj = xs
        m_new = jnp.maximum(bl + m_st, gj)
        decay = jnp.exp(bl + m_st - m_new)
        inject = jnp.exp(gj - m_new)
        c_new = decay[..., None, None] * c_st + inject[..., None, None] * kvj
        n_new = decay[..., None] * n_st + inject[..., None] * knj
        return (c_new, n_new, m_new), (c_st, n_st, m_st)

    xs = (jnp.moveaxis(b_last, 2, 0), jnp.moveaxis(g, 2, 0), jnp.moveaxis(kv, 2, 0), jnp.moveaxis(kn, 2, 0))
    final, starts = lax.scan(step, state, xs)
    if not need_out:
        return None, final
    c0 = jnp.moveaxis(starts[0], 0, 2)
    n0 = jnp.moveaxis(starts[1], 0, 2)
    m0 = jnp.moveaxis(starts[2], 0, 2)
    q_c = _to_chunks(q)
    L = q_c.shape[-2]
    d = b[..., :, None] - b[..., None, :] + li[..., None, :]
    d = jnp.where(jnp.tril(jnp.ones((L, L), dtype=bool)), d, -jnp.inf)
    inter = b + m0[..., None]
    m_t = jnp.maximum(jnp.max(d, axis=-1), inter)
    wq = jnp.exp(d - m_t[..., None]) * jnp.einsum("bhctd,bhcsd->bhcts", q_c, k_c)
    carry_w = jnp.exp(inter - m_t)
    num = jnp.einsum("bhcts,bhcse->bhcte", wq, v_c) + carry_w[..., None] * jnp.einsum("bhctd,bhcde->bhcte", q_c, c0)
    den = jnp.sum(wq, axis=-1) + carry_w * jnp.einsum("bhctd,bhcd->bhct", q_c, n0)
    h = num / jnp.maximum(jnp.abs(den), jnp.exp(-m_t))[..., None]
    return _from_chunks(h), final


def _mlstm_mixer(u_ctx, u_lat, w_in, gate_b, head_norm, w_out, need_ctx):
    qk = A_HEADS * A_DQK
    vd = A_HEADS * A_DV
    splits = [qk, 2 * qk, 2 * qk + vd, 2 * qk + vd + D_MODEL]

    def project(u):
        bsz, n = u.shape[:2]
        q, k, v, o, gates = jnp.split(u @ w_in, splits, axis=-1)
        q = q.astype(F32).reshape(bsz, n, A_HEADS, A_DQK)
        k = k.astype(F32).reshape(bsz, n, A_HEADS, A_DQK) * (A_DQK ** -0.5)
        v = v.astype(F32).reshape(bsz, n, A_HEADS, A_DV)
        gates = gates.astype(F32).reshape(bsz, n, 4, A_HEADS) + gate_b.astype(F32)
        fwd = (gates[:, :, 0], jax.nn.log_sigmoid(gates[:, :, 1]))
        bwd = (gates[:, :, 2], jax.nn.log_sigmoid(gates[:, :, 3]))
        return q, k, v, o, fwd, bwd

    def flip(t):
        return jnp.flip(t, axis=1)

    cq, ck, cv, co, cf, cb = project(u_ctx)
    lq, lk, lv, lo, lf, lb = project(u_lat)
    bsz = u_lat.shape[0]
    zero = (jnp.zeros((bsz, A_HEADS, A_DQK, A_DV), F32), jnp.zeros((bsz, A_HEADS, A_DQK), F32),
            jnp.zeros((bsz, A_HEADS), F32))
    hcf, st_f = _mlstm_direction(cq, ck, cv, cf[0], cf[1], zero, need_ctx)
    hcb, st_b = _mlstm_direction(flip(cq), flip(ck), flip(cv), flip(cb[0]), flip(cb[1]), zero, need_ctx)
    hlf, _ = _mlstm_direction(lq, lk, lv, lf[0], lf[1], st_f, True)
    hlb, _ = _mlstm_direction(flip(lq), flip(lk), flip(lv), flip(lb[0]), flip(lb[1]), st_b, True)

    def finish(hf, hb, o):
        hs = hf + flip(hb)
        hn = hs * lax.rsqrt(jnp.mean(hs * hs, axis=-1, keepdims=True) + NORM_EPS)
        hn = hn * head_norm.astype(F32).reshape(A_HEADS, A_DV)
        bsz_, n = hs.shape[:2]
        return (hn.reshape(bsz_, n, vd).astype(o.dtype) * jax.nn.sigmoid(o)) @ w_out

    y_lat = finish(hlf, hlb, lo)
    y_ctx = finish(hcf, hcb, co) if need_ctx else None
    return y_lat, y_ctx


def _swa_mixer(u_ctx, u_lat, w_in, sink, w_out, cos, sin, need_ctx):
    scale = SWA_DH ** -0.5
    sink_g = sink.astype(F32).reshape(SWA_KV_HEADS, SWA_GROUP)[:, :, None]

    def project(u):
        bsz, n = u.shape[:2]
        q, k, v = jnp.split(u @ w_in, [SWA_HEADS * SWA_DH, (SWA_HEADS + SWA_KV_HEADS) * SWA_DH], axis=-1)
        return (q.reshape(bsz, n, SWA_HEADS, SWA_DH), k.reshape(bsz, n, SWA_KV_HEADS, SWA_DH),
                v.reshape(bsz, n, SWA_KV_HEADS, SWA_DH))

    cq, ck, cv = project(u_ctx)
    lq, lk, lv = project(u_lat)
    lq = _apply_rope(lq, cos, sin)
    lk = _apply_rope(lk, cos, sin)
    bsz, n_tok = u_lat.shape[:2]
    nb = n_tok // SWA_BLOCK
    kp = jnp.pad(lk, ((0, 0), (SWA_BLOCK, SWA_BLOCK), (0, 0), (0, 0)))
    vp = jnp.pad(lv, ((0, 0), (SWA_BLOCK, SWA_BLOCK), (0, 0), (0, 0)))
    q_blocks = jnp.moveaxis(lq.reshape(bsz, nb, SWA_BLOCK, SWA_KV_HEADS, SWA_GROUP, SWA_DH), 1, 0)
    q_idx = jnp.arange(SWA_BLOCK)[:, None]
    k_idx = jnp.arange(3 * SWA_BLOCK)[None, :]
    rel = k_idx - SWA_BLOCK - q_idx

    def block(args):
        j, qb = args
        kb = lax.dynamic_slice_in_dim(kp, j * SWA_BLOCK, 3 * SWA_BLOCK, axis=1)
        vb = lax.dynamic_slice_in_dim(vp, j * SWA_BLOCK, 3 * SWA_BLOCK, axis=1)
        kpos = j * SWA_BLOCK - SWA_BLOCK + k_idx
        valid = (jnp.abs(rel) <= WINDOW) & (kpos >= 0) & (kpos < n_tok)
        s_lat = jnp.einsum("bqhgd,bkhd->bhgqk", qb, kb).astype(F32) * scale
        s_lat = jnp.where(valid, s_lat, -jnp.inf)
        s_ctx = jnp.einsum("bqhgd,bchd->bhgqc", qb, ck).astype(F32) * scale
        m = jnp.maximum(jnp.maximum(jnp.max(s_lat, -1), jnp.max(s_ctx, -1)), sink_g)
        p_lat = jnp.exp(s_lat - m[..., None])
        p_ctx = jnp.exp(s_ctx - m[..., None])
        den = jnp.sum(p_lat, -1) + jnp.sum(p_ctx, -1) + jnp.exp(sink_g - m)
        o = jnp.einsum("bhgqk,bkhd->bqhgd", p_lat, vb) + jnp.einsum("bhgqc,bchd->bqhgd", p_ctx, cv)
        return o / jnp.moveaxis(den, 3, 1)[..., None]

    o_lat = lax.map(block, (jnp.arange(nb), q_blocks))
    o_lat = jnp.moveaxis(o_lat, 0, 1).reshape(bsz, n_tok, SWA_HEADS * SWA_DH)
    y_lat = o_lat.astype(u_lat.dtype) @ w_out
    if not need_ctx:
        return y_lat, None
    n_ctx = u_ctx.shape[1]
    cqg = cq.reshape(bsz, n_ctx, SWA_KV_HEADS, SWA_GROUP, SWA_DH)
    s = jnp.einsum("bqhgd,bchd->bhgqc", cqg, ck).astype(F32) * scale
    m = jnp.maximum(jnp.max(s, -1), sink_g)
    p = jnp.exp(s - m[..., None])
    den = jnp.sum(p, -1) + jnp.exp(sink_g - m)
    o_ctx = jnp.einsum("bhgqc,bchd->bqhgd", p, cv) / jnp.moveaxis(den, 3, 1)[..., None]
    y_ctx = o_ctx.reshape(bsz, n_ctx, SWA_HEADS * SWA_DH).astype(u_ctx.dtype) @ w_out
    return y_lat, y_ctx


def _diff_mixer(u_ctx, u_lat, w_in, lam_q1, lam_k1, lam_q2, lam_k2, head_norm, w_out, cos, sin,
                layer_idx, need_ctx):
    scale = DIFF_DH ** -0.5
    lam_init = 0.8 - 0.6 * math.exp(-0.3 * layer_idx)
    lam = (jnp.exp(jnp.sum(lam_q1.astype(F32) * lam_k1.astype(F32)))
           - jnp.exp(jnp.sum(lam_q2.astype(F32) * lam_k2.astype(F32))) + lam_init)
    qk = 2 * DIFF_HEADS * DIFF_DH

    def project(u, rope):
        bsz, n = u.shape[:2]
        q, k, v = jnp.split(u @ w_in, [qk, 2 * qk], axis=-1)
        q = q.reshape(bsz, n, 2 * DIFF_HEADS, DIFF_DH)
        k = k.reshape(bsz, n, 2 * DIFF_HEADS, DIFF_DH)
        if rope:
            q = _apply_rope(q, cos, sin)
            k = _apply_rope(k, cos, sin)
        return (q.reshape(bsz, n, DIFF_HEADS, 2, DIFF_DH), k.reshape(bsz, n, DIFF_HEADS, 2, DIFF_DH),
                v.reshape(bsz, n, DIFF_HEADS, DIFF_DV))

    def finish(o, dtype):
        od = o[:, :, :, 0] - lam * o[:, :, :, 1]
        od = od * lax.rsqrt(jnp.mean(od * od, axis=-1, keepdims=True) + NORM_EPS)
        od = od * head_norm.astype(F32).reshape(DIFF_HEADS, DIFF_DV) * (1.0 - lam_init)
        bsz, n = od.shape[:2]
        return od.reshape(bsz, n, DIFF_HEADS * DIFF_DV).astype(dtype) @ w_out

    cq, ck, cv = project(u_ctx, False)
    lq, lk, lv = project(u_lat, True)
    bsz, n_tok = u_lat.shape[:2]
    nb = n_tok // DIFF_BLOCK
    q_blocks = jnp.moveaxis(lq.reshape(bsz, nb, DIFF_BLOCK, DIFF_HEADS, 2, DIFF_DH), 1, 0)

    def block(qb):
        s_lat = jnp.einsum("bqhmd,bkhmd->bhmqk", qb, lk).astype(F32) * scale
        s_ctx = jnp.einsum("bqhmd,bchmd->bhmqc", qb, ck).astype(F32) * scale
        mx = jnp.maximum(jnp.max(s_lat, -1), jnp.max(s_ctx, -1))[..., None]
        p_lat = jnp.exp(s_lat - mx)
        p_ctx = jnp.exp(s_ctx - mx)
        den = jnp.sum(p_lat, -1) + jnp.sum(p_ctx, -1)
        o = jnp.einsum("bhmqk,bkhe->bqhme", p_lat, lv) + jnp.einsum("bhmqc,bche->bqhme", p_ctx, cv)
        return o / jnp.moveaxis(den, 3, 1)[..., None]

    o_lat = lax.map(block, q_blocks)
    o_lat = jnp.moveaxis(o_lat, 0, 1).reshape(bsz, n_tok, DIFF_HEADS, 2, DIFF_DV)
    y_lat = finish(o_lat, u_lat.dtype)
    if not need_ctx:
        return y_lat, None
    s = jnp.einsum("bqhmd,bkhmd->bhmqk", cq, ck).astype(F32) * scale
    a = jax.nn.softmax(s, axis=-1)
    o_ctx = jnp.einsum("bhmqk,bkhe->bqhme", a, cv)
    return y_lat, finish(o_ctx, u_ctx.dtype)


def setup_inputs(seed: int = 0) -> dict:
    key = jax.random.key(seed)
    ks = jax.random.split(key, 25)
    D = D_MODEL

    def nrm(k, shape, scale):
        return jax.random.normal(k, shape, F32) * scale

    forget_bias = jnp.linspace(3.0, 6.0, A_HEADS, dtype=F32)
    zeros_h = jnp.zeros((A_HEADS,), F32)
    gate_base = jnp.stack([zeros_h, forget_bias, zeros_h, forget_bias])
    return {
        "x": nrm(ks[0], (BATCH, SEQ, D), 1.0),
        "c": nrm(ks[1], (BATCH, D), 1.0),
        "ctx": nrm(ks[2], (BATCH, CTX_LEN, D), 1.0),
        "c_ctx": nrm(ks[3], (D,), 1.0),
        "ada_w": nrm(ks[4], (DEPTH, D, 6 * D), 0.5 * D ** -0.5),
        "ada_b": nrm(ks[5], (DEPTH, 6 * D), 0.02),
        "norm_mix": 1.0 + nrm(ks[6], (DEPTH, D), 0.02),
        "norm_ffn": 1.0 + nrm(ks[7], (DEPTH, D), 0.02),
        "ffn_w1": nrm(ks[8], (DEPTH, D, D_FF), D ** -0.5),
        "ffn_w2": nrm(ks[9], (DEPTH, D_FF, D), D_FF ** -0.5),
        "mlstm_w_in": nrm(ks[10], (N_MLSTM, D, A_IN), D ** -0.5),
        "mlstm_gate_b": gate_base + nrm(ks[11], (N_MLSTM, 4, A_HEADS), 0.1),
        "mlstm_head_norm": 1.0 + nrm(ks[12], (N_MLSTM, A_HEADS * A_DV), 0.02),
        "mlstm_w_out": nrm(ks[13], (N_MLSTM, A_HEADS * A_DV, D), (A_HEADS * A_DV) ** -0.5),
        "swa_w_in": nrm(ks[14], (N_SWA, D, SWA_IN), D ** -0.5),
        "swa_sink": nrm(ks[15], (N_SWA, SWA_HEADS), 0.5),
        "swa_w_out": nrm(ks[16], (N_SWA, SWA_HEADS * SWA_DH, D), (SWA_HEADS * SWA_DH) ** -0.5),
        "diff_w_in": nrm(ks[17], (N_DIFF, D, DIFF_IN), D ** -0.5),
        "diff_lambda_q1": nrm(ks[18], (N_DIFF, DIFF_DH), 0.1),
        "diff_lambda_k1": nrm(ks[19], (N_DIFF, DIFF_DH), 0.1),
        "diff_lambda_q2": nrm(ks[20], (N_DIFF, DIFF_DH), 0.1),
        "diff_lambda_k2": nrm(ks[21], (N_DIFF, DIFF_DH), 0.1),
        "diff_head_norm": 1.0 + nrm(ks[22], (N_DIFF, DIFF_HEADS * DIFF_DV), 0.02),
        "diff_w_out": nrm(ks[23], (N_DIFF, DIFF_HEADS * DIFF_DV, D), (DIFF_HEADS * DIFF_DV) ** -0.5),
        "final_norm": 1.0 + nrm(ks[24], (D,), 0.02),
    }


def reference(x, c, ctx, c_ctx, ada_w, ada_b, norm_mix, norm_ffn, ffn_w1, ffn_w2,
              mlstm_w_in, mlstm_gate_b, mlstm_head_norm, mlstm_w_out,
              swa_w_in, swa_sink, swa_w_out,
              diff_w_in, diff_lambda_q1, diff_lambda_k1, diff_lambda_q2, diff_lambda_k2,
              diff_head_norm, diff_w_out, final_norm):
    n_tok = x.shape[1]
    cos_s, sin_s = _axial_rope_tables(n_tok, SWA_DH)
    cos_d, sin_d = _axial_rope_tables(n_tok, DIFF_DH)
    cond_lat = jax.nn.silu(c)[:, None, :]
    cond_ctx = jax.nn.silu(c_ctx)[None, None, :]
    h, hc = x, ctx
    for i in range(DEPTH):
        kind, slot = i % N_MIXERS, i // N_MIXERS
        need_ctx = i < DEPTH - 1
        mod_l = jnp.split(cond_lat @ ada_w[i] + ada_b[i], 6, axis=-1)
        mod_c = jnp.split(cond_ctx @ ada_w[i] + ada_b[i], 6, axis=-1)
        u = _modulate(_rms_norm(h, norm_mix[i]), mod_l[0], mod_l[1])
        uc = _modulate(_rms_norm(hc, norm_mix[i]), mod_c[0], mod_c[1])
        if kind == 0:
            y, yc = _mlstm_mixer(uc, u, mlstm_w_in[slot], mlstm_gate_b[slot], mlstm_head_norm[slot],
                                 mlstm_w_out[slot], need_ctx)
        elif kind == 1:
            y, yc = _swa_mixer(uc, u, swa_w_in[slot], swa_sink[slot], swa_w_out[slot], cos_s, sin_s, need_ctx)
        else:
            y, yc = _diff_mixer(uc, u, diff_w_in[slot], diff_lambda_q1[slot], diff_lambda_k1[slot],
                                diff_lambda_q2[slot], diff_lambda_k2[slot], diff_head_norm[slot],
                                diff_w_out[slot], cos_d, sin_d, i, need_ctx)
        h = h + mod_l[2] * y
        h = h + mod_l[5] * _sq_relu_mlp(_modulate(_rms_norm(h, norm_ffn[i]), mod_l[3], mod_l[4]),
                                        ffn_w1[i], ffn_w2[i])
        if need_ctx:
            hc = hc + mod_c[2] * yc
            hc = hc + mod_c[5] * _sq_relu_mlp(_modulate(_rms_norm(hc, norm_ffn[i]), mod_c[3], mod_c[4]),
                                              ffn_w1[i], ffn_w2[i])
    return _rms_norm(h, final_norm)
```

```python
import functools
import math

import jax
import jax.numpy as jnp
from jax import lax
from jax.experimental import pallas as pl
from jax.experimental.pallas import tpu as pltpu

F32 = jnp.float32
BF16 = jnp.bfloat16

LANES = 128
SUBLANES = 8
VMEM_LIMIT_BYTES = 56 * 1024 * 1024

NORM_EPS = 1e-6
ROPE_BASE = 10000.0
GRID_W = 64
N_MIXERS = 3

A_HEADS = 8
A_DQK = 64
A_DV = 128
A_CHUNK = 128
A_PAIRS = A_HEADS // 2

SWA_HEADS = 16
SWA_KV_HEADS = 4
SWA_DH = 64
SWA_GROUP = SWA_HEADS // SWA_KV_HEADS
SWA_BLOCK = 128

DIFF_HEADS = 8
DIFF_DH = 64
DIFF_DV = 128


def _params(*sem):
    return pltpu.CompilerParams(dimension_semantics=sem, vmem_limit_bytes=VMEM_LIMIT_BYTES)


def _dot(a, b):
    return jnp.dot(a, b, preferred_element_type=F32)


def _dot_nt(a, b):
    return lax.dot_general(a, b, (((1,), (1,)), ((), ())), preferred_element_type=F32)


def _dot_tn(a, b):
    return lax.dot_general(a, b, (((0,), (0,)), ((), ())), preferred_element_type=F32)


def _norm_mod(x, gain, shift, scale):
    y = x * lax.rsqrt(jnp.mean(x * x, axis=-1, keepdims=True) + NORM_EPS) * gain
    return y * (1.0 + scale) + shift


def _mod_row(ref, row):
    return ref[0, pl.ds(row, 1), :]


def _ada_kernel(c_ref, w_ref, b_ref, o_ref):
    c = c_ref[...]
    s = (c * jax.nn.sigmoid(c)).astype(BF16)
    o_ref[0] = _dot(s, w_ref[0].astype(BF16)) + b_ref[0]


def _ada_table(cond, ada_w, ada_b):
    depth, d, n = ada_w.shape
    r = cond.shape[0]
    tn = n // 4
    return pl.pallas_call(
        _ada_kernel,
        grid=(depth, n // tn),
        in_specs=[pl.BlockSpec((r, d), lambda i, j: (0, 0)),
                  pl.BlockSpec((1, d, tn), lambda i, j: (i, 0, j)),
                  pl.BlockSpec((1, 1, tn), lambda i, j: (i, 0, j))],
        out_specs=pl.BlockSpec((1, r, tn), lambda i, j: (i, 0, j)),
        out_shape=jax.ShapeDtypeStruct((depth, r, n), F32),
        compiler_params=_params("parallel", "parallel"),
        name="ada_table",
    )(cond, ada_w, ada_b.reshape(depth, 1, n))


def _rope_block(blk, cos, sin_signed):
    lane = lax.broadcasted_iota(jnp.int32, blk.shape, 1)
    first_half = (lane & 32) == 0
    partner = jnp.where(first_half, pltpu.roll(blk, LANES - 32, 1), pltpu.roll(blk, 32, 1))
    return blk * cos + partner * sin_signed


def _proj_kernel(*refs, ctx_row, n_out, rope_cols, kscale, chunk):
    if rope_cols:
        x_ref, sh_ref, sc_ref, g_ref, w_ref, cos_ref, sin_ref, o_ref = refs
    else:
        x_ref, sh_ref, sc_ref, g_ref, w_ref, o_ref = refs
    row = pl.program_id(0) if ctx_row is None else ctx_row
    u = _norm_mod(x_ref[0], g_ref[...], _mod_row(sh_ref, row), _mod_row(sc_ref, row)).astype(BF16)
    for c0 in range(0, n_out, chunk):
        acc = _dot(u, w_ref[:, c0:c0 + chunk])
        for l0 in range(0, chunk, LANES):
            col = c0 + l0
            blk = acc[:, l0:l0 + LANES]
            if col < rope_cols:
                blk = _rope_block(blk, cos_ref[...], sin_ref[...])
            if kscale is not None and kscale[0] <= col < kscale[1]:
                blk = blk * kscale[2]
            o_ref[0, :, col:col + LANES] = blk


def _project(h, mods, layer, gain, w, *, ctx_row=None, tm, rope=None, rope_cols=0, kscale=None):
    b, t, d = h.shape
    n = w.shape[1]
    r = mods.shape[1]
    chunk = 512
    assert t % tm == 0 and n % chunk == 0
    in_specs = [pl.BlockSpec((1, tm, d), lambda bi, i: (bi, i, 0)),
                pl.BlockSpec((1, r, d), lambda bi, i: (layer, 0, 0)),
                pl.BlockSpec((1, r, d), lambda bi, i: (layer, 0, 1)),
                pl.BlockSpec((1, d), lambda bi, i: (0, 0)),
                pl.BlockSpec((d, n), lambda bi, i: (0, 0))]
    args = [h, mods, mods, gain.reshape(1, d), w]
    if rope_cols:
        in_specs += [pl.BlockSpec((tm, LANES), lambda bi, i: (i, 0))] * 2
        args += list(rope)
    return pl.pallas_call(
        functools.partial(_proj_kernel, ctx_row=ctx_row, n_out=n, rope_cols=rope_cols, kscale=kscale, chunk=chunk),
        grid=(b, t // tm),
        in_specs=in_specs,
        out_specs=pl.BlockSpec((1, tm, n), lambda bi, i: (bi, i, 0)),
        out_shape=jax.ShapeDtypeStruct((b, t, n), F32),
        compiler_params=_params("parallel", "parallel"),
        name="project",
    )(*args)


def _log_sigmoid(x):
    return jnp.minimum(x, 0.0) - jnp.log1p(jnp.exp(-jnp.abs(x)))


def _tri_cumsum(tri, x):
    hi = x.astype(BF16)
    r1 = x - hi.astype(F32)
    mid = r1.astype(BF16)
    lo = (r1 - mid.astype(F32)).astype(BF16)
    return _dot(tri, hi) + _dot(tri, mid) + _dot(tri, lo)


def _mlstm_chunk(q, k, v, m_col, m_row, hh, rev, state, emit):
    jl = (4 if rev else 0) + hh
    jb = jl + 2
    li_c, b_c = m_col[:, jl:jl + 1], m_col[:, jb:jb + 1]
    li_r, b_r = m_row[jl:jl + 1, :], m_row[jb:jb + 1, :]
    length = q.shape[0]
    b_last = b_c[0:1] if rev else b_c[length - 1:length]
    c0, n0, m0 = state
    a = (b_last - b_c) + li_c
    g = jnp.max(a, axis=0, keepdims=True)
    kw = k * jnp.exp(a - g)
    kv = _dot_tn(kw.astype(BF16), v.astype(BF16))
    kn = jnp.sum(kw, axis=0, keepdims=True)
    m_new = jnp.maximum(b_last + m0, g)
    decay = jnp.exp(b_last + m0 - m_new)
    inject = jnp.exp(g - m_new)
    new_state = (decay * c0 + inject * kv, decay * n0 + inject * kn, m_new)
    if not emit:
        return None, new_state
    t_i = lax.broadcasted_iota(jnp.int32, (length, length), 0)
    s_i = lax.broadcasted_iota(jnp.int32, (length, length), 1)
    mask = (s_i >= t_i) if rev else (s_i <= t_i)
    d = jnp.where(mask, (b_c - b_r) + li_r, -jnp.inf)
    inter = b_c + m0
    m_t = jnp.maximum(jnp.max(d, axis=-1, keepdims=True), inter)
    qb = q.astype(BF16)
    wq = jnp.exp(d - m_t) * _dot_nt(qb, k.astype(BF16))
    carry_w = jnp.exp(inter - m_t)
    num = _dot(wq.astype(BF16), v.astype(BF16)) + carry_w * _dot(qb, c0.astype(BF16))
    den = jnp.sum(wq, axis=-1, keepdims=True) + carry_w * jnp.sum(q * n0, axis=-1, keepdims=True)
    return num / jnp.maximum(jnp.abs(den), jnp.exp(-m_t)), new_state


def _mlstm_kernel(*refs, n_ctx, n_lat, need_ctx):
    if need_ctx:
        (ql, kl, vl, ol, gl, qc, kc, vc, oc, gc, gb_ref, hn_ref, yl_ref, yc_ref,
         mcol_scr, mrow_scr, hf_scr) = refs
    else:
        (ql, kl, vl, ol, gl, qc, kc, vc, gc, gb_ref, hn_ref, yl_ref,
         mcol_scr, mrow_scr, hf_scr) = refs
        oc = yc_ref = None
    L = A_CHUNK
    row_i = lax.broadcasted_iota(jnp.int32, (L, L), 0)
    col_i = lax.broadcasted_iota(jnp.int32, (L, L), 1)
    tri_lo = jnp.where(col_i <= row_i, 1.0, 0.0).astype(BF16)
    tri_up = jnp.where(col_i >= row_i, 1.0, 0.0).astype(BF16)
    fwd_cum = (col_i == 2) | (col_i == 3)
    bwd_cum = (col_i == 6) | (col_i == 7)
    gate_b = gb_ref[...]

    def prep(g_raw, dst):
        g = g_raw + gate_b
        lf = _log_sigmoid(g)
        m = jnp.where(fwd_cum, _tri_cumsum(tri_lo, lf), jnp.where(bwd_cum, _tri_cumsum(tri_up, lf), g))
        mcol_scr[pl.ds(dst, L), :] = m
        mrow_scr[pl.ds(dst, L), :] = m.T

    for c in range(n_ctx):
        prep(gc[0, c * L:(c + 1) * L, :], c * L)

    def prep_lat(c, carry):
        src = pl.multiple_of(c * L, L)
        prep(gl[0, pl.ds(src, L), :], pl.multiple_of((c + n_ctx) * L, L))
        return carry

    lax.fori_loop(0, n_lat, prep_lat, 0)

    def heads(qr, kr, vr, src):
        q2, k2, v2 = qr[0, pl.ds(src, L), :], kr[0, pl.ds(src, L), :], vr[0, pl.ds(src, L), :]
        return [(q2[:, hh * A_DQK:(hh + 1) * A_DQK], k2[:, hh * A_DQK:(hh + 1) * A_DQK],
                 v2[:, hh * A_DV:(hh + 1) * A_DV]) for hh in range(2)]

    def step(qr, kr, vr, src, dst, rev, states, emit, finish):
        m_col = mcol_scr[pl.ds(dst, L), :]
        m_row = mrow_scr[pl.ds(dst, L), :]
        new_states = []
        for hh, (q, k, v) in enumerate(heads(qr, kr, vr, src)):
            h, st = _mlstm_chunk(q, k, v, m_col, m_row, hh, rev, states[hh], emit)
            new_states.append(st)
            if emit:
                finish(hh, h)
        return tuple(new_states)

    zero = tuple((jnp.zeros((A_DQK, A_DV), F32), jnp.zeros((1, A_DQK), F32), jnp.zeros((1, 1), F32))
                 for _ in range(2))

    def park(dst):
        def fn(hh, h):
            hf_scr[pl.ds(dst, L), hh * A_DV:(hh + 1) * A_DV] = h
        return fn

    st = zero
    for c in range(n_ctx):
        st = step(qc, kc, vc, c * L, c * L, False, st, need_ctx, park(c * L))

    def fwd_lat(c, st):
        src = pl.multiple_of(c * L, L)
        dst = pl.multiple_of((c + n_ctx) * L, L)
        return step(ql, kl, vl, src, dst, False, st, True, park(dst))

    lax.fori_loop(0, n_lat, fwd_lat, st)

    def finish_into(o_ref, y_ref, src, dst):
        def fn(hh, h):
            hs = hf_scr[pl.ds(dst, L), hh * A_DV:(hh + 1) * A_DV] + h
            hn = hs * lax.rsqrt(jnp.mean(hs * hs, axis=-1, keepdims=True) + NORM_EPS)
            hn = hn * hn_ref[:, hh * A_DV:(hh + 1) * A_DV]
            gate = jax.nn.sigmoid(o_ref[0, pl.ds(src, L), hh * A_DV:(hh + 1) * A_DV])
            y_ref[0, pl.ds(src, L), hh * A_DV:(hh + 1) * A_DV] = hn * gate
        return fn

    st = zero
    for c in reversed(range(n_ctx)):
        fin = finish_into(oc, yc_ref, c * L, c * L) if need_ctx else None
        st = step(qc, kc, vc, c * L, c * L, True, st, need_ctx, fin)

    def bwd_lat(i, st):
        c = n_lat - 1 - i
        src = pl.multiple_of(c * L, L)
        dst = pl.multiple_of((c + n_ctx) * L, L)
        return step(ql, kl, vl, src, dst, True, st, True, finish_into(ol, yl_ref, src, dst))

    lax.fori_loop(0, n_lat, bwd_lat, st)


def _mlstm_mix(p, pc, gate_b, head_norm, need_ctx):
    b, t, _ = p.shape
    nc = pc.shape[1]
    L = A_CHUNK
    gate_blk = (2 * A_HEADS * A_DQK + 2 * A_HEADS * A_DV) // LANES

    def specs(rows, with_o):
        s = [pl.BlockSpec((1, rows, 2 * A_DQK), lambda bi, hp: (bi, 0, hp)),
             pl.BlockSpec((1, rows, 2 * A_DQK), lambda bi, hp: (bi, 0, A_PAIRS + hp)),
             pl.BlockSpec((1, rows, 2 * A_DV), lambda bi, hp: (bi, 0, A_PAIRS + hp))]
        if with_o:
            s.append(pl.BlockSpec((1, rows, 2 * A_DV), lambda bi, hp: (bi, 0, 2 * A_PAIRS + hp)))
        s.append(pl.BlockSpec((1, rows, LANES), lambda bi, hp: (bi, 0, gate_blk + hp)))
        return s

    in_specs = specs(t, True) + specs(nc, need_ctx)
    args = [p] * 5 + [pc] * (5 if need_ctx else 4)
    in_specs += [pl.BlockSpec((1, LANES), lambda bi, hp: (0, hp)),
                 pl.BlockSpec((1, 2 * A_DV), lambda bi, hp: (0, hp))]
    args += [gate_b, head_norm]
    out_specs = [pl.BlockSpec((1, t, 2 * A_DV), lambda bi, hp: (bi, 0, hp))]
    out_shape = [jax.ShapeDtypeStruct((b, t, A_HEADS * A_DV), F32)]
    if need_ctx:
        out_specs.append(pl.BlockSpec((1, nc, 2 * A_DV), lambda bi, hp: (bi, 0, hp)))
        out_shape.append(jax.ShapeDtypeStruct((b, nc, A_HEADS * A_DV), F32))
    outs = pl.pallas_call(
        functools.partial(_mlstm_kernel, n_ctx=nc // L, n_lat=t // L, need_ctx=need_ctx),
        grid=(b, A_PAIRS),
        in_specs=in_specs,
        out_specs=out_specs,
        out_shape=out_shape,
        scratch_shapes=[pltpu.VMEM((nc + t, LANES), F32), pltpu.VMEM((nc + t, LANES), F32),
                        pltpu.VMEM((nc + t, 2 * A_DV), F32)],
        compiler_params=_params("parallel", "parallel"),
        name="mlstm_mix",
    )(*args)
    return (outs[0], outs[1]) if need_ctx else (outs[0], None)


def _swa_attend(q4, k, v, valid, sink_col):
    s = _dot_nt(q4.astype(BF16), k.astype(BF16)) * (SWA_DH ** -0.5)
    if valid is not None:
        s = jnp.where(valid, s, -jnp.inf)
    m = jnp.maximum(jnp.max(s, axis=-1, keepdims=True), sink_col)
    p = jnp.exp(s - m)
    den = jnp.sum(p, axis=-1, keepdims=True) + jnp.exp(sink_col - m)
    return _dot(p.astype(BF16), v.astype(BF16)) / den


def _swa_heads(q, keys, vals, valid, sink_ref, o_ref, rows):
    row_i = lax.broadcasted_iota(jnp.int32, (SWA_GROUP * rows, 1), 0)
    for hk in range(SWA_KV_HEADS):
        ks = slice(hk * SWA_DH, (hk + 1) * SWA_DH)
        k = jnp.concatenate([x[:, ks] for x in keys], axis=0)
        v = jnp.concatenate([x[:, ks] for x in vals], axis=0)
        q4 = jnp.concatenate([q[:, (hk * SWA_GROUP + g) * SWA_DH:(hk * SWA_GROUP + g + 1) * SWA_DH]
                              for g in range(SWA_GROUP)], axis=0)
        sink_col = jnp.zeros((SWA_GROUP * rows, 1), F32)
        for g in range(SWA_GROUP):
            head = hk * SWA_GROUP + g
            sink_col = jnp.where((row_i >= g * rows) & (row_i < (g + 1) * rows), sink_ref[:, head:head + 1], sink_col)
        o4 = _swa_attend(q4, k, v, valid, sink_col)
        for g in range(SWA_GROUP):
            head = hk * SWA_GROUP + g
            o_ref[0, :, head * SWA_DH:(head + 1) * SWA_DH] = o4[g * rows:(g + 1) * rows]


def _swa_lat_kernel(q_ref, kp_ref, kc_ref, kn_ref, vp_ref, vc_ref, vn_ref, kx_ref, vx_ref, sink_ref, o_ref, *, nblk):
    j = pl.program_id(1)
    L = SWA_BLOCK
    nc = kx_ref.shape[1]
    qi = lax.broadcasted_iota(jnp.int32, (L, L), 0)
    ki = lax.broadcasted_iota(jnp.int32, (L, L), 1)
    valid = jnp.concatenate([(ki >= qi) & (j > 0), jnp.ones((L, L), jnp.bool_), (ki <= qi) & (j < nblk - 1),
                             jnp.ones((L, nc), jnp.bool_)], axis=1)
    valid = jnp.concatenate([valid] * SWA_GROUP, axis=0)
    _swa_heads(q_ref[0], [kp_ref[0], kc_ref[0], kn_ref[0], kx_ref[0]],
               [vp_ref[0], vc_ref[0], vn_ref[0], vx_ref[0]], valid, sink_ref, o_ref, L)


def _swa_ctx_kernel(q_ref, kx_ref, vx_ref, sink_ref, o_ref):
    _swa_heads(q_ref[0], [kx_ref[0]], [vx_ref[0]], None, sink_ref, o_ref, q_ref.shape[1])


def _swa_mix(p, pc, sink, need_ctx):
    b, t, _ = p.shape
    nc = pc.shape[1]
    L = SWA_BLOCK
    nblk = t // L
    qw = SWA_HEADS * SWA_DH
    kvw = SWA_KV_HEADS * SWA_DH
    kblk, vblk = qw // kvw, qw // kvw + 1
    sink2 = sink.reshape(1, SWA_HEADS)

    def kv_spec(col, shift):
        return pl.BlockSpec((1, L, kvw), lambda bi, j: (bi, jnp.clip(j + shift, 0, nblk - 1), col))

    y = pl.pallas_call(
        functools.partial(_swa_lat_kernel, nblk=nblk),
        grid=(b, nblk),
        in_specs=[pl.BlockSpec((1, L, qw), lambda bi, j: (bi, j, 0)),
                  kv_spec(kblk, -1), kv_spec(kblk, 0), kv_spec(kblk, 1),
                  kv_spec(vblk, -1), kv_spec(vblk, 0), kv_spec(vblk, 1),
                  pl.BlockSpec((1, nc, kvw), lambda bi, j: (bi, 0, kblk)),
                  pl.BlockSpec((1, nc, kvw), lambda bi, j: (bi, 0, vblk)),
                  pl.BlockSpec((1, SWA_HEADS), lambda bi, j: (0, 0))],
        out_specs=pl.BlockSpec((1, L, qw), lambda bi, j: (bi, j, 0)),
        out_shape=jax.ShapeDtypeStruct((b, t, qw), F32),
        compiler_params=_params("parallel", "parallel"),
        name="swa_mix",
    )(p, p, p, p, p, p, p, pc, pc, sink2)
    if not need_ctx:
        return y, None
    yc = pl.pallas_call(
        _swa_ctx_kernel,
        grid=(b,),
        in_specs=[pl.BlockSpec((1, nc, qw), lambda bi: (bi, 0, 0)),
                  pl.BlockSpec((1, nc, kvw), lambda bi: (bi, 0, kblk)),
                  pl.BlockSpec((1, nc, kvw), lambda bi: (bi, 0, vblk)),
                  pl.BlockSpec((1, SWA_HEADS), lambda bi: (0, 0))],
        out_specs=pl.BlockSpec((1, nc, qw), lambda bi: (bi, 0, 0)),
        out_shape=jax.ShapeDtypeStruct((b, nc, qw), F32),
        compiler_params=_params("parallel"),
        name="swa_ctx_mix",
    )(pc, pc, pc, sink2)
    return y, yc


def _diff_kernel(*refs, lam_init, with_lat):
    if with_lat:
        q_ref, kl_ref, vl_ref, kx_ref, vx_ref, lam_ref, hn_ref, o_ref = refs
    else:
        q_ref, kx_ref, vx_ref, lam_ref, hn_ref, o_ref = refs
    q = q_ref[0]
    lane = lax.broadcasted_iota(jnp.int32, q.shape, 1)
    scale = DIFF_DH ** -0.5
    kx, vx = kx_ref[0].astype(BF16), vx_ref[0].astype(BF16)
    if with_lat:
        kl, vl = kl_ref[0].astype(BF16), vl_ref[0].astype(BF16)
    outs = []
    for m in range(2):
        qm = jnp.where((lane >= DIFF_DH) if m else (lane < DIFF_DH), q, 0.0).astype(BF16)
        s_ctx = _dot_nt(qm, kx) * scale
        mx = jnp.max(s_ctx, axis=-1, keepdims=True)
        if with_lat:
            s_lat = _dot_nt(qm, kl) * scale
            mx = jnp.maximum(jnp.max(s_lat, axis=-1, keepdims=True), mx)
        p_ctx = jnp.exp(s_ctx - mx)
        den = jnp.sum(p_ctx, axis=-1, keepdims=True)
        o = _dot(p_ctx.astype(BF16), vx)
        if with_lat:
            p_lat = jnp.exp(s_lat - mx)
            den = jnp.sum(p_lat, axis=-1, keepdims=True) + den
            o = _dot(p_lat.astype(BF16), vl) + o
        outs.append(o / den)
    lam = (jnp.exp(jnp.sum(lam_ref[0:1, :] * lam_ref[1:2, :], axis=-1, keepdims=True))
           - jnp.exp(jnp.sum(lam_ref[2:3, :] * lam_ref[3:4, :], axis=-1, keepdims=True)) + lam_init)
    od = outs[0] - lam * outs[1]
    od = od * lax.rsqrt(jnp.mean(od * od, axis=-1, keepdims=True) + NORM_EPS)
    o_ref[0] = od * hn_ref[...] * (1.0 - lam_init)


def _diff_mix(p, pc, lam, head_norm, lam_init, need_ctx, tq):
    b, t, _ = p.shape
    nc = pc.shape[1]
    w = DIFF_DV
    kblk, vblk = DIFF_HEADS, 2 * DIFF_HEADS
    out_w = DIFF_HEADS * DIFF_DV
    y = pl.pallas_call(
        functools.partial(_diff_kernel, lam_init=lam_init, with_lat=True),
        grid=(b, DIFF_HEADS, t // tq),
        in_specs=[pl.BlockSpec((1, tq, w), lambda bi, h, i: (bi, i, h)),
                  pl.BlockSpec((1, t, w), lambda bi, h, i: (bi, 0, kblk + h)),
                  pl.BlockSpec((1, t, w), lambda bi, h, i: (bi, 0, vblk + h)),
                  pl.BlockSpec((1, nc, w), lambda bi, h, i: (bi, 0, kblk + h)),
                  pl.BlockSpec((1, nc, w), lambda bi, h, i: (bi, 0, vblk + h)),
                  pl.BlockSpec((4, DIFF_DH), lambda bi, h, i: (0, 0)),
                  pl.BlockSpec((1, w), lambda bi, h, i: (0, h))],
        out_specs=pl.BlockSpec((1, tq, w), lambda bi, h, i: (bi, i, h)),
        out_shape=jax.ShapeDtypeStruct((b, t, out_w), F32),
        compiler_params=_params("parallel", "parallel", "parallel"),
        name="diff_mix",
    )(p, p, p, pc, pc, lam, head_norm)
    if not need_ctx:
        return y, None
    yc = pl.pallas_call(
        functools.partial(_diff_kernel, lam_init=lam_init, with_lat=False),
        grid=(b, DIFF_HEADS),
        in_specs=[pl.BlockSpec((1, nc, w), lambda bi, h: (bi, 0, h)),
                  pl.BlockSpec((1, nc, w), lambda bi, h: (bi, 0, kblk + h)),
                  pl.BlockSpec((1, nc, w), lambda bi, h: (bi, 0, vblk + h)),
                  pl.BlockSpec((4, DIFF_DH), lambda bi, h: (0, 0)),
                  pl.BlockSpec((1, w), lambda bi, h: (0, h))],
        out_specs=pl.BlockSpec((1, nc, w), lambda bi, h: (bi, 0, h)),
        out_shape=jax.ShapeDtypeStruct((b, nc, out_w), F32),
        compiler_params=_params("parallel", "parallel"),
        name="diff_ctx_mix",
    )(pc, pc, pc, lam, head_norm)
    return y, yc


def _post_kernel(*refs, ctx_row, n_ff, final):
    if final:
        (h_ref, y_ref, g2_ref, sh_ref, sc_ref, g5_ref, gain_ref, wo_ref, w1_ref, w2_ref, fn_ref,
         o_ref, h1_scr, u_scr, acc_scr) = refs
    else:
        (h_ref, y_ref, g2_ref, sh_ref, sc_ref, g5_ref, gain_ref, wo_ref, w1_ref, w2_ref,
         o_ref, h1_scr, u_scr, acc_scr) = refs
    j = pl.program_id(2)
    row = pl.program_id(0) if ctx_row is None else ctx_row

    @pl.when(j == 0)
    def _():
        h1 = h_ref[0] + _mod_row(g2_ref, row) * _dot(y_ref[0].astype(BF16), wo_ref[...])
        h1_scr[...] = h1
        u_scr[...] = _norm_mod(h1, gain_ref[...], _mod_row(sh_ref, row), _mod_row(sc_ref, row)).astype(BF16)
        acc_scr[...] = jnp.zeros_like(acc_scr)

    hidden = jnp.square(jnp.maximum(_dot(u_scr[...], w1_ref[...]), 0.0))
    acc_scr[...] += _dot(hidden.astype(BF16), w2_ref[...])

    @pl.when(j == n_ff - 1)
    def _():
        out = h1_scr[...] + _mod_row(g5_ref, row) * acc_scr[...]
        if final:
            out = out * lax.rsqrt(jnp.mean(out * out, axis=-1, keepdims=True) + NORM_EPS) * fn_ref[...]
        o_ref[0] = out


def _post(h, y, mods, layer, gain, wo, w1, w2, *, ctx_row=None, tm, tf, final_gain=None):
    b, t, d = h.shape
    dy = y.shape[2]
    ff = w1.shape[1]
    r = mods.shape[1]
    assert t % tm == 0 and ff % tf == 0
    n_ff = ff // tf
    final = final_gain is not None

    def mod_spec(k):
        return pl.BlockSpec((1, r, d), lambda bi, i, j: (layer, 0, k))

    in_specs = [pl.BlockSpec((1, tm, d), lambda bi, i, j: (bi, i, 0)),
                pl.BlockSpec((1, tm, dy), lambda bi, i, j: (bi, i, 0)),
                mod_spec(2), mod_spec(3), mod_spec(4), mod_spec(5),
                pl.BlockSpec((1, d), lambda bi, i, j: (0, 0)),
                pl.BlockSpec((dy, d), lambda bi, i, j: (0, 0)),
                pl.BlockSpec((d, tf), lambda bi, i, j: (0, j)),
                pl.BlockSpec((tf, d), lambda bi, i, j: (j, 0))]
    args = [h, y, mods, mods, mods, mods, gain.reshape(1, d), wo, w1, w2]
    if final:
        in_specs.append(pl.BlockSpec((1, d), lambda bi, i, j: (0, 0)))
        args.append(final_gain.reshape(1, d))
    return pl.pallas_call(
        functools.partial(_post_kernel, ctx_row=ctx_row, n_ff=n_ff, final=final),
        grid=(b, t // tm, n_ff),
        in_specs=in_specs,
        out_specs=pl.BlockSpec((1, tm, d), lambda bi, i, j: (bi, i, 0)),
        out_shape=jax.ShapeDtypeStruct((b, t, d), F32),
        scratch_shapes=[pltpu.VMEM((tm, d), F32), pltpu.VMEM((tm, d), BF16), pltpu.VMEM((tm, d), F32)],
        compiler_params=_params("parallel", "parallel", "arbitrary"),
        name="post",
    )(*args)


def _rope_tables(n_tok, head_dim):
    rows = n_tok // GRID_W
    row = jnp.repeat(jnp.arange(rows, dtype=jnp.int32), GRID_W).astype(F32)
    col = jnp.tile(jnp.arange(GRID_W, dtype=jnp.int32), rows).astype(F32)
    quarter = head_dim // 4
    inv = ROPE_BASE ** (-jnp.arange(quarter, dtype=F32) / quarter)
    ang = jnp.concatenate([row[:, None] * inv, col[:, None] * inv], axis=-1)
    cos, sin = jnp.cos(ang), jnp.sin(ang)
    return jnp.tile(cos, (1, 4)), jnp.tile(jnp.concatenate([-sin, sin], axis=-1), (1, 2))


def _mlstm_weights(w_in, gate_b):
    d = w_in.shape[0]
    main = 2 * A_HEADS * A_DQK + 2 * A_HEADS * A_DV
    wg = w_in[:, main:].reshape(d, 4, A_PAIRS, 2)
    wg = jnp.transpose(wg, (0, 2, 1, 3)).reshape(d, A_PAIRS, 8)
    wg = jnp.pad(wg, ((0, 0), (0, 0), (0, LANES - 8))).reshape(d, A_PAIRS * LANES)
    gb = jnp.transpose(gate_b.astype(F32).reshape(4, A_PAIRS, 2), (1, 0, 2)).reshape(A_PAIRS, 8)
    gb = jnp.pad(gb, ((0, 0), (0, LANES - 8))).reshape(1, A_PAIRS * LANES)
    return jnp.concatenate([w_in[:, :main], wg], axis=1).astype(BF16), gb


def kernel(x, c, ctx, c_ctx, ada_w, ada_b, norm_mix, norm_ffn, ffn_w1, ffn_w2, mlstm_w_in, mlstm_gate_b, mlstm_head_norm, mlstm_w_out, swa_w_in, swa_sink, swa_w_out, diff_w_in, diff_lambda_q1, diff_lambda_k1, diff_lambda_q2, diff_lambda_k2, diff_head_norm, diff_w_out, final_norm):
    bsz, n_tok, d = x.shape
    n_ctx = ctx.shape[1]
    depth = ada_w.shape[0]
    rows = -(-(bsz + 1) // SUBLANES) * SUBLANES
    cond = jnp.concatenate([c, c_ctx[None, :], jnp.zeros((rows - bsz - 1, d), F32)], axis=0)
    mods = _ada_table(cond, ada_w, ada_b)
    rope = _rope_tables(n_tok, SWA_DH)
    tm_lat, tm_post, tf = 512, 512, 1024

    h, hc = x, ctx
    for i in range(depth):
        kind, slot = i % N_MIXERS, i // N_MIXERS
        need_ctx = i < depth - 1
        proj = functools.partial(_project, mods=mods, layer=i, gain=norm_mix[i])
        if kind == 0:
            w, gb = _mlstm_weights(mlstm_w_in[slot], mlstm_gate_b[slot])
            ks = (A_HEADS * A_DQK, 2 * A_HEADS * A_DQK, A_DQK ** -0.5)
            p = proj(h, w=w, tm=tm_lat, kscale=ks)
            pc = proj(hc, w=w, tm=n_ctx, ctx_row=bsz, kscale=ks)
            y, yc = _mlstm_mix(p, pc, gb, mlstm_head_norm[slot].reshape(1, -1), need_ctx)
            wo = mlstm_w_out[slot]
        elif kind == 1:
            w = swa_w_in[slot].astype(BF16)
            rc = (SWA_HEADS + SWA_KV_HEADS) * SWA_DH
            p = proj(h, w=w, tm=tm_lat, rope=rope, rope_cols=rc)
            pc = proj(hc, w=w, tm=n_ctx, ctx_row=bsz)
            y, yc = _swa_mix(p, pc, swa_sink[slot], need_ctx)
            wo = swa_w_out[slot]
        else:
            w = diff_w_in[slot].astype(BF16)
            rc = 4 * DIFF_HEADS * DIFF_DH
            p = proj(h, w=w, tm=tm_lat, rope=rope, rope_cols=rc)
            pc = proj(hc, w=w, tm=n_ctx, ctx_row=bsz)
            lam = jnp.stack([diff_lambda_q1[slot], diff_lambda_k1[slot], diff_lambda_q2[slot], diff_lambda_k2[slot]])
            lam_init = 0.8 - 0.6 * math.exp(-0.3 * i)
            y, yc = _diff_mix(p, pc, lam.astype(F32), diff_head_norm[slot].reshape(1, -1), lam_init, need_ctx, tq=256)
            wo = diff_w_out[slot]
        post = functools.partial(_post, mods=mods, layer=i, gain=norm_ffn[i], wo=wo.astype(BF16),
                                 w1=ffn_w1[i].astype(BF16), w2=ffn_w2[i].astype(BF16), tf=tf)
        h = post(h, y, tm=tm_post, final_gain=None if need_ctx else final_norm)
        if need_ctx:
            hc = post(hc, yc, tm=n_ctx, ctx_row=bsz)
    return h
```

```python
import functools
import math

import jax
import jax.numpy as jnp
from jax import lax
from jax.experimental import pallas as pl
from jax.experimental.pallas import tpu as pltpu

F32 = jnp.float32
BF16 = jnp.bfloat16

LANES = 128
SUBLANES = 8
VMEM_LIMIT_BYTES = 56 * 1024 * 1024
LOG2E = math.log2(math.e)

NORM_EPS = 1e-6
ROPE_BASE = 10000.0
GRID_W = 64
N_MIXERS = 3

A_HEADS = 8
A_DQK = 64
A_DV = 128
A_CHUNK = 128
A_PAIRS = A_HEADS // 2
A_GATES = 4 * A_HEADS

SWA_HEADS = 16
SWA_KV_HEADS = 4
SWA_DH = 64
SWA_GROUP = SWA_HEADS // SWA_KV_HEADS
SWA_BLOCK = 128

DIFF_HEADS = 8
DIFF_DH = 64
DIFF_DV = 128


def _params(*sem):
    return pltpu.CompilerParams(dimension_semantics=sem, vmem_limit_bytes=VMEM_LIMIT_BYTES)


def _dot(a, b):
    return jnp.dot(a, b, preferred_element_type=F32)


def _dot_nt(a, b):
    return lax.dot_general(a, b, (((1,), (1,)), ((), ())), preferred_element_type=F32)


def _norm_mod(x, gain, shift, scale):
    y = x * lax.rsqrt(jnp.mean(x * x, axis=-1, keepdims=True) + NORM_EPS) * gain
    return y * (1.0 + scale) + shift


def _mod_row(ref, row):
    return ref[0, pl.ds(row, 1), :]


def _ada_kernel(c_ref, w_ref, b_ref, o_ref):
    c = c_ref[...]
    s = (c * jax.nn.sigmoid(c)).astype(BF16)
    o_ref[0] = _dot(s, w_ref[0].astype(BF16)) + b_ref[0]


def _ada_table(cond, ada_w, ada_b):
    depth, d, n = ada_w.shape
    r = cond.shape[0]
    tn = n // 4
    return pl.pallas_call(
        _ada_kernel,
        grid=(depth, n // tn),
        in_specs=[pl.BlockSpec((r, d), lambda i, j: (0, 0)),
                  pl.BlockSpec((1, d, tn), lambda i, j: (i, 0, j)),
                  pl.BlockSpec((1, 1, tn), lambda i, j: (i, 0, j))],
        out_specs=pl.BlockSpec((1, r, tn), lambda i, j: (i, 0, j)),
        out_shape=jax.ShapeDtypeStruct((depth, r, n), F32),
        compiler_params=_params("parallel", "parallel"),
        name="ada_table",
    )(cond, ada_w, ada_b.reshape(depth, 1, n))


def _rope_block(blk, cos, sin_signed):
    lane = lax.broadcasted_iota(jnp.int32, blk.shape, 1)
    first_half = (lane & 32) == 0
    partner = jnp.where(first_half, pltpu.roll(blk, LANES - 32, 1), pltpu.roll(blk, 32, 1))
    return blk * cos + partner * sin_signed


def _proj_kernel(*refs, ctx_row, n_out, rope_cols, qscale, chunk):
    if rope_cols:
        x_ref, sh_ref, sc_ref, g_ref, w_ref, cos_ref, sin_ref, o_ref = refs
    else:
        x_ref, sh_ref, sc_ref, g_ref, w_ref, o_ref = refs
    row = pl.program_id(0) if ctx_row is None else ctx_row
    u = _norm_mod(x_ref[0], g_ref[...], _mod_row(sh_ref, row), _mod_row(sc_ref, row)).astype(BF16)
    for c0 in range(0, n_out, chunk):
        acc = _dot(u, w_ref[:, c0:c0 + chunk])
        for l0 in range(0, chunk, LANES):
            col = c0 + l0
            blk = acc[:, l0:l0 + LANES]
            if col < rope_cols:
                blk = _rope_block(blk, cos_ref[...], sin_ref[...])
            if col < qscale[0]:
                blk = blk * qscale[1]
            o_ref[0, :, col:col + LANES] = blk.astype(o_ref.dtype)


def _project(h, mods, layer, gain, w, *, ctx_row=None, tm, rope=None, rope_cols=0, qscale):
    b, t, d = h.shape
    n = w.shape[1]
    r = mods.shape[1]
    chunk = 512
    assert t % tm == 0 and n % chunk == 0
    in_specs = [pl.BlockSpec((1, tm, d), lambda bi, i: (bi, i, 0)),
                pl.BlockSpec((1, r, d), lambda bi, i: (layer, 0, 0)),
                pl.BlockSpec((1, r, d), lambda bi, i: (layer, 0, 1)),
                pl.BlockSpec((1, d), lambda bi, i: (0, 0)),
                pl.BlockSpec((d, n), lambda bi, i: (0, 0))]
    args = [h, mods, mods, gain.reshape(1, d), w]
    if rope_cols:
        in_specs += [pl.BlockSpec((tm, LANES), lambda bi, i: (i, 0))] * 2
        args += list(rope)
    return pl.pallas_call(
        functools.partial(_proj_kernel, ctx_row=ctx_row, n_out=n, rope_cols=rope_cols, qscale=qscale, chunk=chunk),
        grid=(b, t // tm),
        in_specs=in_specs,
        out_specs=pl.BlockSpec((1, tm, n), lambda bi, i: (bi, i, 0)),
        out_shape=jax.ShapeDtypeStruct((b, t, n), BF16),
        compiler_params=_params("parallel", "parallel"),
        name="project",
    )(*args)


def _log_sigmoid(x):
    return jnp.minimum(x, 0.0) - jnp.log1p(jnp.exp(-jnp.abs(x)))


def _lane_scan(x, op, fill, reverse):
    lane = lax.broadcasted_iota(jnp.int32, x.shape, 1)
    k = 1
    while k < LANES:
        if reverse:
            shifted, ok = pltpu.roll(x, LANES - k, 1), lane < LANES - k
        else:
            shifted, ok = pltpu.roll(x, k, 1), lane >= k
        x = op(x, jnp.where(ok, shifted, fill))
        k *= 2
    return x


def _proj_mlstm_kernel(x_ref, sh_ref, sc_ref, g_ref, w_ref, wt_ref, gb_ref, o_ref, kt_ref, gt_ref, ct_ref, ws_ref,
                       *, ctx_row, n_out, chunk):
    row = pl.program_id(0) if ctx_row is None else ctx_row
    u = _norm_mod(x_ref[0], g_ref[...], _mod_row(sh_ref, row), _mod_row(sc_ref, row)).astype(BF16)
    for c0 in range(0, n_out, chunk):
        o_ref[0, :, c0:c0 + chunk] = _dot(u, w_ref[:, c0:c0 + chunk])
    ut = _dot_nt(wt_ref[...], u)
    nk = A_HEADS * A_DQK
    row8 = lax.broadcasted_iota(jnp.int32, (A_GATES, LANES), 0) & 7
    fwd = row8 < 4
    is_cum = (row8 & 2) != 0
    for ci in range(u.shape[0] // A_CHUNK):
        cols = slice(ci * A_CHUNK, (ci + 1) * A_CHUNK)
        kt_ref[0, ci] = ut[:nk, cols] * (A_DQK ** -0.5)
        x = ut[nk:, cols] + gb_ref[...]
        lf = _log_sigmoid(x)
        cum = jnp.where(fwd, _lane_scan(lf, jnp.add, 0.0, False), _lane_scan(lf, jnp.add, 0.0, True))
        gt_ref[0, ci] = jnp.where(is_cum, cum, x)
        cum_up = pltpu.roll(cum, A_GATES - 2, 0)
        r = x - cum_up
        cmax = jnp.where(fwd, _lane_scan(r, jnp.maximum, -jnp.inf, False), _lane_scan(r, jnp.maximum, -jnp.inf, True))
        b_last = jnp.where(fwd, jnp.broadcast_to(cum_up[:, A_CHUNK - 1:A_CHUNK], cum_up.shape),
                           jnp.broadcast_to(cum_up[:, 0:1], cum_up.shape))
        a = (b_last - cum_up) + x
        g = jnp.broadcast_to(jnp.max(a, axis=-1, keepdims=True), a.shape)
        ct_ref[0, ci] = jnp.where(is_cum, pltpu.roll(b_last, 2, 0), cmax)
        ws_ref[0, ci] = jnp.where(is_cum, pltpu.roll(g, 2, 0), jnp.exp(a - g))


def _project_mlstm(h, mods, layer, gain, w, wt, gb, *, ctx_row=None, tm):
    b, t, d = h.shape
    n = w.shape[1]
    r = mods.shape[1]
    nt = wt.shape[0]
    nk = A_HEADS * A_DQK
    chunk = 512
    cpt = tm // A_CHUNK
    assert t % tm == 0 and n % chunk == 0 and tm % A_CHUNK == 0
    return pl.pallas_call(
        functools.partial(_proj_mlstm_kernel, ctx_row=ctx_row, n_out=n, chunk=chunk),
        grid=(b, t // tm),
        in_specs=[pl.BlockSpec((1, tm, d), lambda bi, i: (bi, i, 0)),
                  pl.BlockSpec((1, r, d), lambda bi, i: (layer, 0, 0)),
                  pl.BlockSpec((1, r, d), lambda bi, i: (layer, 0, 1)),
                  pl.BlockSpec((1, d), lambda bi, i: (0, 0)),
                  pl.BlockSpec((d, n), lambda bi, i: (0, 0)),
                  pl.BlockSpec((nt, d), lambda bi, i: (0, 0)),
                  pl.BlockSpec((A_GATES, LANES), lambda bi, i: (0, 0))],
        out_specs=[pl.BlockSpec((1, tm, n), lambda bi, i: (bi, i, 0)),
                   pl.BlockSpec((1, cpt, nk, A_CHUNK), lambda bi, i: (bi, i, 0, 0)),
                   ] + [pl.BlockSpec((1, cpt, A_GATES, A_CHUNK), lambda bi, i: (bi, i, 0, 0))] * 3,
        out_shape=[jax.ShapeDtypeStruct((b, t, n), F32),
                   jax.ShapeDtypeStruct((b, t // A_CHUNK, nk, A_CHUNK), F32),
                   ] + [jax.ShapeDtypeStruct((b, t // A_CHUNK, A_GATES, A_CHUNK), F32)] * 3,
        compiler_params=_params("parallel", "parallel"),
        name="project_mlstm",
    )(h, mods, mods, gain.reshape(1, d), w, wt, gb)


def _mlstm_kernel(*refs, n_ctx, n_lat, need_ctx):
    if need_ctx:
        (ql, vl, ol, ktl, gtl, ctl, wsl, qc, vc, oc, ktc, gtc, ctc, wsc, hn_ref, yl_ref, yc_ref,
         kv_scr, cbd_scr, cst_scr, g_scr, bl_scr, m0_scr) = refs
    else:
        (ql, vl, ol, ktl, gtl, ctl, wsl, qc, vc, ktc, gtc, ctc, wsc, hn_ref, yl_ref,
         kv_scr, cbd_scr, cst_scr, g_scr, bl_scr, m0_scr) = refs
        oc = yc_ref = None
    L = A_CHUNK
    n_all = n_ctx + n_lat
    ones_v = jnp.ones((L, A_DV), BF16)

    def v_ext(v2, hh):
        return jnp.concatenate([v2[:, hh * A_DV:(hh + 1) * A_DV].astype(BF16), ones_v], axis=1)

    def rows(gt, dr, hh):
        return gt[dr * 4 + hh:dr * 4 + hh + 1, :], gt[dr * 4 + 2 + hh:dr * 4 + 3 + hh, :]

    def contrib(c, ct, ws, kt, v2):
        for hh in range(2):
            vx = v_ext(v2, hh)
            kth = kt[hh * A_DQK:(hh + 1) * A_DQK, :]
            for dr in range(2):
                w, g = rows(ws, dr, hh)
                idx = c * 4 + dr * 2 + hh
                kv_scr[idx] = _dot((kth * w).astype(BF16), vx)
                g_scr[pl.ds(idx, 1), :] = g
                bl_scr[pl.ds(idx, 1), :] = rows(ct, dr, hh)[1]

    for c in range(n_ctx):
        contrib(c, ctc[0, c], wsc[0, c], ktc[0, c], vc[0, c * L:(c + 1) * L, :])

    def contrib_lat(c, carry):
        contrib(c + n_ctx, ctl[0, c], wsl[0, c], ktl[0, c], vl[0, pl.ds(pl.multiple_of(c * L, L), L), :])
        return carry

    lax.fori_loop(0, n_lat, contrib_lat, 0)

    cst_scr[...] = jnp.zeros_like(cst_scr)
    zpad = jnp.zeros((A_DQK, 2 * A_DV), BF16)

    def scan_step(i, ms):
        c_bwd = jnp.where(i < n_ctx, n_ctx - 1 - i, n_all - 1 - (i - n_ctx))
        new_ms = []
        for dr, c in ((0, i), (1, c_bwd)):
            c0s = [cst_scr[dr * 2 + hh] for hh in range(2)]
            cbd_scr[c * 2 + dr] = jnp.concatenate(
                [jnp.concatenate([c0s[0].astype(BF16), zpad], axis=1),
                 jnp.concatenate([zpad, c0s[1].astype(BF16)], axis=1)], axis=0)
            for hh in range(2):
                idx = c * 4 + dr * 2 + hh
                m0 = ms[dr * 2 + hh]
                m0_scr[pl.ds(idx, 1), :] = m0
                g, b_last = g_scr[pl.ds(idx, 1), :], bl_scr[pl.ds(idx, 1), :]
                m_new = jnp.maximum(b_last + m0, g)
                decay = jnp.exp(b_last + m0 - m_new)
                inject = jnp.exp(g - m_new)
                decay, inject = (jnp.concatenate([z, z], axis=1) for z in (decay, inject))
                cst_scr[dr * 2 + hh] = decay * c0s[hh] + inject * kv_scr[idx]
                new_ms.append(m_new)
        return tuple(new_ms)

    lax.fori_loop(0, n_all, scan_step, tuple(jnp.zeros((1, L), F32) for _ in range(4)))

    t_i = lax.broadcasted_iota(jnp.int32, (L, L), 0)
    s_i = lax.broadcasted_iota(jnp.int32, (L, L), 1)
    masks = (s_i <= t_i, s_i >= t_i)
    zk = jnp.zeros((A_DQK, L), BF16)
    zrows = jnp.zeros((L - 2 * SUBLANES, L), F32)

    def outputs(c, gt, ct, kt, q2, v2, o2, store):
        colm = jnp.concatenate([gt, ct, zrows], axis=0).T
        qb = q2.astype(BF16)
        ktb = kt.astype(BF16)
        kt_bd = jnp.concatenate([jnp.concatenate([ktb[:A_DQK], zk], axis=1),
                                 jnp.concatenate([zk, ktb[A_DQK:]], axis=1)], axis=0)
        s2 = _dot(qb, kt_bd)
        vxs = [v_ext(v2, hh) for hh in range(2)]
        hsum = [None, None]
        for dr in range(2):
            qc2 = _dot(qb, cbd_scr[c * 2 + dr])
            for hh in range(2):
                idx = c * 4 + dr * 2 + hh
                li, cum = rows(gt, dr, hh)
                m0 = m0_scr[pl.ds(idx, 1), :]
                cmax = jnp.broadcast_to(colm[:, 8 + dr * 4 + hh:9 + dr * 4 + hh], (L, L))
                cum_t = jnp.broadcast_to(colm[:, dr * 4 + 2 + hh:dr * 4 + 3 + hh], (L, L))
                mm = jnp.maximum(cmax, m0)
                p = jnp.where(masks[dr], jnp.exp((li - cum) - mm), 0.0)
                wq = (p * s2[:, hh * L:(hh + 1) * L]).astype(BF16)
                carry = jnp.exp(m0 - mm)
                ne = _dot(wq, vxs[hh]) + jnp.concatenate([carry, carry], axis=1) * qc2[:, hh * 2 * A_DV:(hh + 1) * 2 * A_DV]
                h = ne[:, :A_DV] / jnp.maximum(jnp.abs(ne[:, A_DV:]), jnp.exp(-(cum_t + mm)))
                hsum[hh] = h if dr == 0 else hsum[hh] + h
        for hh in range(2):
            hs = hsum[hh]
            hn = hs * lax.rsqrt(jnp.mean(hs * hs, axis=-1, keepdims=True) + NORM_EPS)
            hn = hn * hn_ref[:, hh * A_DV:(hh + 1) * A_DV]
            store(hh, hn * jax.nn.sigmoid(o2[:, hh * A_DV:(hh + 1) * A_DV]))

    if need_ctx:
        for c in range(n_ctx):
            sl = slice(c * L, (c + 1) * L)

            def store_ctx(hh, y, sl=sl):
                yc_ref[0, sl, hh * A_DV:(hh + 1) * A_DV] = y

            outputs(c, gtc[0, c], ctc[0, c], ktc[0, c], qc[0, sl, :], vc[0, sl, :], oc[0, sl, :], store_ctx)

    def outputs_lat(c, carry):
        src = pl.ds(pl.multiple_of(c * L, L), L)

        def store_lat(hh, y):
            yl_ref[0, src, hh * A_DV:(hh + 1) * A_DV] = y

        outputs(c + n_ctx, gtl[0, c], ctl[0, c], ktl[0, c], ql[0, src, :], vl[0, src, :], ol[0, src, :], store_lat)
        return carry

    lax.fori_loop(0, n_lat, outputs_lat, 0)


def _mlstm_mix(p, kt, gt, ct, ws, pc, ktc, gtc, ctc, wsc, head_norm, need_ctx):
    b, t, _ = p.shape
    nc = pc.shape[1]
    L = A_CHUNK
    n_lat, n_ctx = t // L, nc // L
    n_all = n_lat + n_ctx
    qw, vw = 2 * A_DQK, 2 * A_DV
    v_blk = A_HEADS * A_DQK // vw
    o_blk = v_blk + A_PAIRS

    def specs(rows, nch, with_o):
        s = [pl.BlockSpec((1, rows, qw), lambda bi, hp: (bi, 0, hp)),
             pl.BlockSpec((1, rows, vw), lambda bi, hp: (bi, 0, v_blk + hp))]
        if with_o:
            s.append(pl.BlockSpec((1, rows, vw), lambda bi, hp: (bi, 0, o_blk + hp)))
        s.append(pl.BlockSpec((1, nch, qw, L), lambda bi, hp: (bi, 0, hp, 0)))
        s += [pl.BlockSpec((1, nch, SUBLANES, L), lambda bi, hp: (bi, 0, hp, 0))] * 3
        return s

    in_specs = specs(t, n_lat, True) + specs(nc, n_ctx, need_ctx)
    in_specs.append(pl.BlockSpec((1, vw), lambda bi, hp: (0, hp)))
    args = [p, p, p, kt, gt, ct, ws] + ([pc, pc, pc] if need_ctx else [pc, pc]) + [ktc, gtc, ctc, wsc, head_norm]
    out_specs = [pl.BlockSpec((1, t, vw), lambda bi, hp: (bi, 0, hp))]
    out_shape = [jax.ShapeDtypeStruct((b, t, A_HEADS * A_DV), F32)]
    if need_ctx:
        out_specs.append(pl.BlockSpec((1, nc, vw), lambda bi, hp: (bi, 0, hp)))
        out_shape.append(jax.ShapeDtypeStruct((b, nc, A_HEADS * A_DV), F32))
    n_rows = -(-n_all * 4 // SUBLANES) * SUBLANES
    outs = pl.pallas_call(
        functools.partial(_mlstm_kernel, n_ctx=n_ctx, n_lat=n_lat, need_ctx=need_ctx),
        grid=(b, A_PAIRS),
        in_specs=in_specs,
        out_specs=out_specs,
        out_shape=out_shape,
        scratch_shapes=[pltpu.VMEM((n_all * 4, A_DQK, vw), F32),
                        pltpu.VMEM((n_all * 2, qw, 2 * vw), BF16),
                        pltpu.VMEM((4, A_DQK, vw), F32),
                        pltpu.VMEM((n_rows, L), F32), pltpu.VMEM((n_rows, L), F32), pltpu.VMEM((n_rows, L), F32)],
        compiler_params=_params("parallel", "parallel"),
        name="mlstm_mix",
    )(*args)
    return (outs[0], outs[1]) if need_ctx else (outs[0], None)


def _swa_attend(q4, k, v, valid, sink_col):
    s = _dot_nt(q4, k)
    if valid is not None:
        s = jnp.where(valid, s, -jnp.inf)
    m = jnp.maximum(jnp.max(s, axis=-1, keepdims=True), sink_col)
    p = jnp.exp2(s - m)
    den = jnp.sum(p, axis=-1, keepdims=True) + jnp.exp2(sink_col - m)
    return _dot(p.astype(BF16), v) / den


def _swa_heads(q, keys, vals, valid, sink_ref, o_ref, rows):
    row_i = lax.broadcasted_iota(jnp.int32, (SWA_GROUP * rows, 1), 0)
    for hk in range(SWA_KV_HEADS):
        ks = slice(hk * SWA_DH, (hk + 1) * SWA_DH)
        k = jnp.concatenate([x[:, ks] for x in keys], axis=0)
        v = jnp.concatenate([x[:, ks] for x in vals], axis=0)
        q4 = jnp.concatenate([q[:, (hk * SWA_GROUP + g) * SWA_DH:(hk * SWA_GROUP + g + 1) * SWA_DH]
                              for g in range(SWA_GROUP)], axis=0)
        sink_col = jnp.zeros((SWA_GROUP * rows, 1), F32)
        for g in range(SWA_GROUP):
            head = hk * SWA_GROUP + g
            sink_col = jnp.where((row_i >= g * rows) & (row_i < (g + 1) * rows),
                                 sink_ref[:, head:head + 1] * LOG2E, sink_col)
        o4 = _swa_attend(q4, k, v, valid, sink_col)
        for g in range(SWA_GROUP):
            head = hk * SWA_GROUP + g
            o_ref[0, :, head * SWA_DH:(head + 1) * SWA_DH] = o4[g * rows:(g + 1) * rows]


def _swa_lat_kernel(q_ref, kp_ref, kc_ref, kn_ref, vp_ref, vc_ref, vn_ref, kx_ref, vx_ref, sink_ref, o_ref, *, nblk):
    j = pl.program_id(1)
    L = SWA_BLOCK
    nc = kx_ref.shape[1]
    qi = lax.broadcasted_iota(jnp.int32, (L, L), 0)
    ki = lax.broadcasted_iota(jnp.int32, (L, L), 1)
    valid = jnp.concatenate([(ki >= qi) & (j > 0), jnp.ones((L, L), jnp.bool_), (ki <= qi) & (j < nblk - 1),
                             jnp.ones((L, nc), jnp.bool_)], axis=1)
    valid = jnp.concatenate([valid] * SWA_GROUP, axis=0)
    _swa_heads(q_ref[0], [kp_ref[0], kc_ref[0], kn_ref[0], kx_ref[0]],
               [vp_ref[0], vc_ref[0], vn_ref[0], vx_ref[0]], valid, sink_ref, o_ref, L)


def _swa_ctx_kernel(q_ref, kx_ref, vx_ref, sink_ref, o_ref):
    _swa_heads(q_ref[0], [kx_ref[0]], [vx_ref[0]], None, sink_ref, o_ref, q_ref.shape[1])


def _swa_mix(p, pc, sink, need_ctx):
    b, t, _ = p.shape
    nc = pc.shape[1]
    L = SWA_BLOCK
    nblk = t // L
    qw = SWA_HEADS * SWA_DH
    kvw = SWA_KV_HEADS * SWA_DH
    kblk, vblk = qw // kvw, qw // kvw + 1
    sink2 = sink.reshape(1, SWA_HEADS)

    def kv_spec(col, shift):
        return pl.BlockSpec((1, L, kvw), lambda bi, j: (bi, jnp.clip(j + shift, 0, nblk - 1), col))

    y = pl.pallas_call(
        functools.partial(_swa_lat_kernel, nblk=nblk),
        grid=(b, nblk),
        in_specs=[pl.BlockSpec((1, L, qw), lambda bi, j: (bi, j, 0)),
                  kv_spec(kblk, -1), kv_spec(kblk, 0), kv_spec(kblk, 1),
                  kv_spec(vblk, -1), kv_spec(vblk, 0), kv_spec(vblk, 1),
                  pl.BlockSpec((1, nc, kvw), lambda bi, j: (bi, 0, kblk)),
                  pl.BlockSpec((1, nc, kvw), lambda bi, j: (bi, 0, vblk)),
                  pl.BlockSpec((1, SWA_HEADS), lambda bi, j: (0, 0))],
        out_specs=pl.BlockSpec((1, L, qw), lambda bi, j: (bi, j, 0)),
        out_shape=jax.ShapeDtypeStruct((b, t, qw), F32),
        compiler_params=_params("parallel", "parallel"),
        name="swa_mix",
    )(p, p, p, p, p, p, p, pc, pc, sink2)
    if not need_ctx:
        return y, None
    yc = pl.pallas_call(
        _swa_ctx_kernel,
        grid=(b,),
        in_specs=[pl.BlockSpec((1, nc, qw), lambda bi: (bi, 0, 0)),
                  pl.BlockSpec((1, nc, kvw), lambda bi: (bi, 0, kblk)),
                  pl.BlockSpec((1, nc, kvw), lambda bi: (bi, 0, vblk)),
                  pl.BlockSpec((1, SWA_HEADS), lambda bi: (0, 0))],
        out_specs=pl.BlockSpec((1, nc, qw), lambda bi: (bi, 0, 0)),
        out_shape=jax.ShapeDtypeStruct((b, nc, qw), F32),
        compiler_params=_params("parallel"),
        name="swa_ctx_mix",
    )(pc, pc, pc, sink2)
    return y, yc


def _diff_kernel(*refs, lam_init, with_lat):
    if with_lat:
        q_ref, kx_ref, vx_ref, kl_ref, vl_ref, lam_ref, hn_ref, o_ref, k_scr, v_scr = refs
    else:
        q_ref, kx_ref, vx_ref, lam_ref, hn_ref, o_ref, k_scr, v_scr = refs
    nc = kx_ref.shape[1]

    @pl.when(pl.program_id(2) == 0)
    def _():
        k_scr[0:nc, :] = kx_ref[0]
        v_scr[0:nc, 0:DIFF_DV] = vx_ref[0]
        if with_lat:
            k_scr[nc:, :] = kl_ref[0]
            v_scr[nc:, 0:DIFF_DV] = vl_ref[0]
        v_scr[:, DIFF_DV:] = jnp.ones((v_scr.shape[0], DIFF_DV), BF16)

    q = q_ref[0]
    lane = lax.broadcasted_iota(jnp.int32, q.shape, 1)
    outs = []
    for m in range(2):
        qm = jnp.where((lane >= DIFF_DH) if m else (lane < DIFF_DH), q, jnp.zeros_like(q))
        s = _dot_nt(qm, k_scr[...])
        p = jnp.exp2(s - jnp.max(s, axis=-1, keepdims=True)).astype(BF16)
        ne = _dot(p, v_scr[...])
        outs.append(ne[:, :DIFF_DV] / ne[:, DIFF_DV:])
    lam = (jnp.exp(jnp.sum(lam_ref[0:1, :] * lam_ref[1:2, :], axis=-1, keepdims=True))
           - jnp.exp(jnp.sum(lam_ref[2:3, :] * lam_ref[3:4, :], axis=-1, keepdims=True)) + lam_init)
    od = outs[0] - lam * outs[1]
    od = od * lax.rsqrt(jnp.mean(od * od, axis=-1, keepdims=True) + NORM_EPS)
    o_ref[0] = od * hn_ref[...] * (1.0 - lam_init)


def _diff_mix(p, pc, lam, head_norm, lam_init, need_ctx, tq):
    b, t, _ = p.shape
    nc = pc.shape[1]
    w = DIFF_DV
    kblk, vblk = DIFF_HEADS, 2 * DIFF_HEADS
    out_w = DIFF_HEADS * DIFF_DV

    def call(q_arr, n_q, tile, with_lat, name):
        n_keys = nc + (t if with_lat else 0)
        in_specs = [pl.BlockSpec((1, tile, w), lambda bi, h, i: (bi, i, h)),
                    pl.BlockSpec((1, nc, w), lambda bi, h, i: (bi, 0, kblk + h)),
                    pl.BlockSpec((1, nc, w), lambda bi, h, i: (bi, 0, vblk + h))]
        args = [q_arr, pc, pc]
        if with_lat:
            in_specs += [pl.BlockSpec((1, t, w), lambda bi, h, i: (bi, 0, kblk + h)),
                         pl.BlockSpec((1, t, w), lambda bi, h, i: (bi, 0, vblk + h))]
            args += [p, p]
        in_specs += [pl.BlockSpec((4, DIFF_DH), lambda bi, h, i: (0, 0)),
                     pl.BlockSpec((1, w), lambda bi, h, i: (0, h))]
        args += [lam, head_norm]
        return pl.pallas_call(
            functools.partial(_diff_kernel, lam_init=lam_init, with_lat=with_lat),
            grid=(b, DIFF_HEADS, n_q // tile),
            in_specs=in_specs,
            out_specs=pl.BlockSpec((1, tile, w), lambda bi, h, i: (bi, i, h)),
            out_shape=jax.ShapeDtypeStruct((b, n_q, out_w), F32),
            scratch_shapes=[pltpu.VMEM((n_keys, w), BF16), pltpu.VMEM((n_keys, 2 * w), BF16)],
            compiler_params=_params("parallel", "parallel", "arbitrary"),
            name=name,
        )(*args)

    y = call(p, t, tq, True, "diff_mix")
    yc = call(pc, nc, nc, False, "diff_ctx_mix") if need_ctx else None
    return y, yc


def _post_kernel(*refs, ctx_row, n_ff, final):
    if final:
        (h_ref, y_ref, g2_ref, sh_ref, sc_ref, g5_ref, gain_ref, wo_ref, w1_ref, w2_ref, fn_ref,
         o_ref, h1_scr, u_scr, acc_scr) = refs
    else:
        (h_ref, y_ref, g2_ref, sh_ref, sc_ref, g5_ref, gain_ref, wo_ref, w1_ref, w2_ref,
         o_ref, h1_scr, u_scr, acc_scr) = refs
    j = pl.program_id(2)
    row = pl.program_id(0) if ctx_row is None else ctx_row

    @pl.when(j == 0)
    def _():
        h1 = h_ref[0] + _mod_row(g2_ref, row) * _dot(y_ref[0].astype(BF16), wo_ref[...])
        h1_scr[...] = h1
        u_scr[...] = _norm_mod(h1, gain_ref[...], _mod_row(sh_ref, row), _mod_row(sc_ref, row)).astype(BF16)
        acc_scr[...] = jnp.zeros_like(acc_scr)

    hidden = jnp.square(jnp.maximum(_dot(u_scr[...], w1_ref[...]), 0.0))
    acc_scr[...] += _dot(hidden.astype(BF16), w2_ref[...])

    @pl.when(j == n_ff - 1)
    def _():
        out = h1_scr[...] + _mod_row(g5_ref, row) * acc_scr[...]
        if final:
            out = out * lax.rsqrt(jnp.mean(out * out, axis=-1, keepdims=True) + NORM_EPS) * fn_ref[...]
        o_ref[0] = out


def _post(h, y, mods, layer, gain, wo, w1, w2, *, ctx_row=None, tm, tf, final_gain=None):
    b, t, d = h.shape
    dy = y.shape[2]
    ff = w1.shape[1]
    r = mods.shape[1]
    assert t % tm == 0 and ff % tf == 0
    n_ff = ff // tf
    final = final_gain is not None

    def mod_spec(k):
        return pl.BlockSpec((1, r, d), lambda bi, i, j: (layer, 0, k))

    in_specs = [pl.BlockSpec((1, tm, d), lambda bi, i, j: (bi, i, 0)),
                pl.BlockSpec((1, tm, dy), lambda bi, i, j: (bi, i, 0)),
                mod_spec(2), mod_spec(3), mod_spec(4), mod_spec(5),
                pl.BlockSpec((1, d), lambda bi, i, j: (0, 0)),
                pl.BlockSpec((dy, d), lambda bi, i, j: (0, 0)),
                pl.BlockSpec((d, tf), lambda bi, i, j: (0, j)),
                pl.BlockSpec((tf, d), lambda bi, i, j: (j, 0))]
    args = [h, y, mods, mods, mods, mods, gain.reshape(1, d), wo, w1, w2]
    if final:
        in_specs.append(pl.BlockSpec((1, d), lambda bi, i, j: (0, 0)))
        args.append(final_gain.reshape(1, d))
    return pl.pallas_call(
        functools.partial(_post_kernel, ctx_row=ctx_row, n_ff=n_ff, final=final),
        grid=(b, t // tm, n_ff),
        in_specs=in_specs,
        out_specs=pl.BlockSpec((1, tm, d), lambda bi, i, j: (bi, i, 0)),
        out_shape=jax.ShapeDtypeStruct((b, t, d), F32),
        scratch_shapes=[pltpu.VMEM((tm, d), F32), pltpu.VMEM((tm, d), BF16), pltpu.VMEM((tm, d), F32)],
        compiler_params=_params("parallel", "parallel", "arbitrary"),
        name="post",
    )(*args)


def _rope_tables(n_tok, head_dim):
    rows = n_tok // GRID_W
    row = jnp.repeat(jnp.arange(rows, dtype=jnp.int32), GRID_W).astype(F32)
    col = jnp.tile(jnp.arange(GRID_W, dtype=jnp.int32), rows).astype(F32)
    quarter = head_dim // 4
    inv = ROPE_BASE ** (-jnp.arange(quarter, dtype=F32) / quarter)
    ang = jnp.concatenate([row[:, None] * inv, col[:, None] * inv], axis=-1)
    cos, sin = jnp.cos(ang), jnp.sin(ang)
    return jnp.tile(cos, (1, 4)), jnp.tile(jnp.concatenate([-sin, sin], axis=-1), (1, 2))


def _mlstm_weights(w_in, gate_b):
    d = w_in.shape[0]
    nk = A_HEADS * A_DQK
    main = 2 * nk + 2 * A_HEADS * A_DV
    w = jnp.concatenate([w_in[:, :nk], w_in[:, 2 * nk:main]], axis=1).astype(BF16)
    wg = jnp.transpose(w_in[:, main:].reshape(d, 4, A_PAIRS, 2), (0, 2, 1, 3)).reshape(d, A_GATES)
    wt = jnp.concatenate([w_in[:, nk:2 * nk], wg], axis=1).T.astype(BF16)
    gb = jnp.transpose(gate_b.astype(F32).reshape(4, A_PAIRS, 2), (1, 0, 2)).reshape(A_GATES, 1)
    return w, wt, jnp.broadcast_to(gb, (A_GATES, LANES))


def kernel(x, c, ctx, c_ctx, ada_w, ada_b, norm_mix, norm_ffn, ffn_w1, ffn_w2, mlstm_w_in, mlstm_gate_b, mlstm_head_norm, mlstm_w_out, swa_w_in, swa_sink, swa_w_out, diff_w_in, diff_lambda_q1, diff_lambda_k1, diff_lambda_q2, diff_lambda_k2, diff_head_norm, diff_w_out, final_norm):
    bsz, n_tok, d = x.shape
    n_ctx = ctx.shape[1]
    depth = ada_w.shape[0]
    rows = -(-(bsz + 1) // SUBLANES) * SUBLANES
    cond = jnp.concatenate([c, c_ctx[None, :], jnp.zeros((rows - bsz - 1, d), F32)], axis=0)
    mods = _ada_table(cond, ada_w, ada_b)
    rope = _rope_tables(n_tok, SWA_DH)
    tm_lat, tm_post, tf = 512, 512, 1024

    h, hc = x, ctx
    for i in range(depth):
        kind, slot = i % N_MIXERS, i // N_MIXERS
        need_ctx = i < depth - 1
        if kind == 0:
            w, wt, gb = _mlstm_weights(mlstm_w_in[slot], mlstm_gate_b[slot])
            proj = functools.partial(_project_mlstm, mods=mods, layer=i, gain=norm_mix[i], w=w, wt=wt, gb=gb)
            lat = proj(h, tm=tm_lat)
            cx = proj(hc, tm=n_ctx, ctx_row=bsz)
            y, yc = _mlstm_mix(*lat, *cx, mlstm_head_norm[slot].reshape(1, -1), need_ctx)
            wo = mlstm_w_out[slot]
        else:
            proj = functools.partial(_project, mods=mods, layer=i, gain=norm_mix[i])
            if kind == 1:
                w = swa_w_in[slot].astype(BF16)
                rc = (SWA_HEADS + SWA_KV_HEADS) * SWA_DH
                qs = (SWA_HEADS * SWA_DH, SWA_DH ** -0.5 * LOG2E)
            else:
                w = diff_w_in[slot].astype(BF16)
                rc = 4 * DIFF_HEADS * DIFF_DH
                qs = (2 * DIFF_HEADS * DIFF_DH, DIFF_DH ** -0.5 * LOG2E)
            p = proj(h, w=w, tm=tm_lat, rope=rope, rope_cols=rc, qscale=qs)
            pc = proj(hc, w=w, tm=n_ctx, ctx_row=bsz, qscale=qs)
            if kind == 1:
                y, yc = _swa_mix(p, pc, swa_sink[slot], need_ctx)
                wo = swa_w_out[slot]
            else:
                lam = jnp.stack([diff_lambda_q1[slot], diff_lambda_k1[slot], diff_lambda_q2[slot], diff_lambda_k2[slot]])
                lam_init = 0.8 - 0.6 * math.exp(-0.3 * i)
                y, yc = _diff_mix(p, pc, lam.astype(F32), diff_head_norm[slot].reshape(1, -1), lam_init, need_ctx, tq=512)
                wo = diff_w_out[slot]
        post = functools.partial(_post, mods=mods, layer=i, gain=norm_ffn[i], wo=wo.astype(BF16),
                                 w1=ffn_w1[i].astype(BF16), w2=ffn_w2[i].astype(BF16), tf=tf)
        h = post(h, y, tm=tm_post, final_gain=None if need_ctx else final_norm)
        if need_ctx:
            hc = post(hc, yc, tm=n_ctx, ctx_row=bsz)
    return h
```

```python
import functools
import math

import jax
import jax.numpy as jnp
from jax import lax
from jax.experimental import pallas as pl
from jax.experimental.pallas import tpu as pltpu

F32 = jnp.float32
BF16 = jnp.bfloat16

LANES = 128
SUBLANES = 8
VMEM_LIMIT_BYTES = 56 * 1024 * 1024
LOG2E = math.log2(math.e)

NORM_EPS = 1e-6
ROPE_BASE = 10000.0
GRID_W = 64
N_MIXERS = 3

A_HEADS = 8
A_DQK = 64
A_DV = 128
A_CHUNK = 128
A_PAIRS = A_HEADS // 2
A_GATES = 4 * A_HEADS

SWA_HEADS = 16
SWA_KV_HEADS = 4
SWA_DH = 64
SWA_GROUP = SWA_HEADS // SWA_KV_HEADS
SWA_BLOCK = 128

DIFF_HEADS = 8
DIFF_DH = 64
DIFF_DV = 128


def _params(*sem):
    return pltpu.CompilerParams(dimension_semantics=sem, vmem_limit_bytes=VMEM_LIMIT_BYTES)


def _dot(a, b):
    return jnp.dot(a, b, preferred_element_type=F32)


def _dot_nt(a, b):
    return lax.dot_general(a, b, (((1,), (1,)), ((), ())), preferred_element_type=F32)


def _norm_mod(x, gain, shift, scale):
    y = x * lax.rsqrt(jnp.mean(x * x, axis=-1, keepdims=True) + NORM_EPS) * gain
    return y * (1.0 + scale) + shift


def _mod_row(ref, row):
    return ref[0, pl.ds(row, 1), :]


def _ada_kernel(c_ref, w_ref, b_ref, o_ref):
    c = c_ref[...]
    s = (c * jax.nn.sigmoid(c)).astype(BF16)
    o_ref[0] = _dot(s, w_ref[0].astype(BF16)) + b_ref[0]


def _ada_table(cond, ada_w, ada_b):
    depth, d, n = ada_w.shape
    r = cond.shape[0]
    tn = n // 4
    return pl.pallas_call(
        _ada_kernel,
        grid=(depth, n // tn),
        in_specs=[pl.BlockSpec((r, d), lambda i, j: (0, 0)),
                  pl.BlockSpec((1, d, tn), lambda i, j: (i, 0, j)),
                  pl.BlockSpec((1, 1, tn), lambda i, j: (i, 0, j))],
        out_specs=pl.BlockSpec((1, r, tn), lambda i, j: (i, 0, j)),
        out_shape=jax.ShapeDtypeStruct((depth, r, n), F32),
        compiler_params=_params("parallel", "parallel"),
        name="ada_table",
    )(cond, ada_w, ada_b.reshape(depth, 1, n))


def _rope_block(blk, cos, sin_signed):
    lane = lax.broadcasted_iota(jnp.int32, blk.shape, 1)
    first_half = (lane & 32) == 0
    partner = jnp.where(first_half, pltpu.roll(blk, LANES - 32, 1), pltpu.roll(blk, 32, 1))
    return blk * cos + partner * sin_signed


def _proj_kernel(*refs, ctx_row, n_out, rope_cols, qscale, chunk):
    if rope_cols:
        x_ref, sh_ref, sc_ref, g_ref, w_ref, cos_ref, sin_ref, o_ref = refs
    else:
        x_ref, sh_ref, sc_ref, g_ref, w_ref, o_ref = refs
    row = pl.program_id(0) if ctx_row is None else ctx_row
    u = _norm_mod(x_ref[0], g_ref[...], _mod_row(sh_ref, row), _mod_row(sc_ref, row)).astype(BF16)
    for c0 in range(0, n_out, chunk):
        acc = _dot(u, w_ref[:, c0:c0 + chunk])
        for l0 in range(0, chunk, LANES):
            col = c0 + l0
            blk = acc[:, l0:l0 + LANES]
            if col < rope_cols:
                blk = _rope_block(blk, cos_ref[...], sin_ref[...])
            if col < qscale[0]:
                blk = blk * qscale[1]
            o_ref[0, :, col:col + LANES] = blk.astype(o_ref.dtype)


def _project(h, mods, layer, gain, w, *, ctx_row=None, tm, rope=None, rope_cols=0, qscale):
    b, t, d = h.shape
    n = w.shape[1]
    r = mods.shape[1]
    chunk = 512
    assert t % tm == 0 and n % chunk == 0
    in_specs = [pl.BlockSpec((1, tm, d), lambda bi, i: (bi, i, 0)),
                pl.BlockSpec((1, r, d), lambda bi, i: (layer, 0, 0)),
                pl.BlockSpec((1, r, d), lambda bi, i: (layer, 0, 1)),
                pl.BlockSpec((1, d), lambda bi, i: (0, 0)),
                pl.BlockSpec((d, n), lambda bi, i: (0, 0))]
    args = [h, mods, mods, gain.reshape(1, d), w]
    if rope_cols:
        in_specs += [pl.BlockSpec((tm, LANES), lambda bi, i: (i, 0))] * 2
        args += list(rope)
    return pl.pallas_call(
        functools.partial(_proj_kernel, ctx_row=ctx_row, n_out=n, rope_cols=rope_cols, qscale=qscale, chunk=chunk),
        grid=(b, t // tm),
        in_specs=in_specs,
        out_specs=pl.BlockSpec((1, tm, n), lambda bi, i: (bi, i, 0)),
        out_shape=jax.ShapeDtypeStruct((b, t, n), BF16),
        compiler_params=_params("parallel", "parallel"),
        name="project",
    )(*args)


def _log_sigmoid(x):
    return jnp.minimum(x, 0.0) - jnp.log1p(jnp.exp(-jnp.abs(x)))


def _lane_scan(x, op, fill, reverse):
    lane = lax.broadcasted_iota(jnp.int32, x.shape, 1)
    k = 1
    while k < LANES:
        if reverse:
            shifted, ok = pltpu.roll(x, LANES - k, 1), lane < LANES - k
        else:
            shifted, ok = pltpu.roll(x, k, 1), lane >= k
        x = op(x, jnp.where(ok, shifted, fill))
        k *= 2
    return x


def _proj_mlstm_kernel(x_ref, sh_ref, sc_ref, g_ref, w_ref, wt_ref, gb_ref, o_ref, kt_ref, gt_ref, ct_ref, ws_ref,
                       *, ctx_row, n_out, chunk):
    row = pl.program_id(0) if ctx_row is None else ctx_row
    u = _norm_mod(x_ref[0], g_ref[...], _mod_row(sh_ref, row), _mod_row(sc_ref, row)).astype(BF16)
    for c0 in range(0, n_out, chunk):
        o_ref[0, :, c0:c0 + chunk] = _dot(u, w_ref[:, c0:c0 + chunk])
    ut = _dot_nt(wt_ref[...], u)
    nk = A_HEADS * A_DQK
    row8 = lax.broadcasted_iota(jnp.int32, (A_GATES, LANES), 0) & 7
    fwd = row8 < 4
    is_cum = (row8 & 2) != 0
    for ci in range(u.shape[0] // A_CHUNK):
        cols = slice(ci * A_CHUNK, (ci + 1) * A_CHUNK)
        kt_ref[0, ci] = ut[:nk, cols] * (A_DQK ** -0.5)
        x = ut[nk:, cols] + gb_ref[...]
        lf = _log_sigmoid(x)
        cum = jnp.where(fwd, _lane_scan(lf, jnp.add, 0.0, False), _lane_scan(lf, jnp.add, 0.0, True))
        gt_ref[0, ci] = jnp.where(is_cum, cum, x)
        cum_up = pltpu.roll(cum, A_GATES - 2, 0)
        r = x - cum_up
        cmax = jnp.where(fwd, _lane_scan(r, jnp.maximum, -jnp.inf, False), _lane_scan(r, jnp.maximum, -jnp.inf, True))
        b_last = jnp.where(fwd, jnp.broadcast_to(cum_up[:, A_CHUNK - 1:A_CHUNK], cum_up.shape),
                           jnp.broadcast_to(cum_up[:, 0:1], cum_up.shape))
        a = (b_last - cum_up) + x
        g = jnp.broadcast_to(jnp.max(a, axis=-1, keepdims=True), a.shape)
        ct_ref[0, ci] = jnp.where(is_cum, pltpu.roll(b_last, 2, 0), cmax)
        ws_ref[0, ci] = jnp.where(is_cum, pltpu.roll(g, 2, 0), jnp.exp(a - g))


def _project_mlstm(h, mods, layer, gain, w, wt, gb, *, ctx_row=None, tm):
    b, t, d = h.shape
    n = w.shape[1]
    r = mods.shape[1]
    nt = wt.shape[0]
    nk = A_HEADS * A_DQK
    chunk = 512
    cpt = tm // A_CHUNK
    assert t % tm == 0 and n % chunk == 0 and tm % A_CHUNK == 0
    return pl.pallas_call(
        functools.partial(_proj_mlstm_kernel, ctx_row=ctx_row, n_out=n, chunk=chunk),
        grid=(b, t // tm),
        in_specs=[pl.BlockSpec((1, tm, d), lambda bi, i: (bi, i, 0)),
                  pl.BlockSpec((1, r, d), lambda bi, i: (layer, 0, 0)),
                  pl.BlockSpec((1, r, d), lambda bi, i: (layer, 0, 1)),
                  pl.BlockSpec((1, d), lambda bi, i: (0, 0)),
                  pl.BlockSpec((d, n), lambda bi, i: (0, 0)),
                  pl.BlockSpec((nt, d), lambda bi, i: (0, 0)),
                  pl.BlockSpec((A_GATES, LANES), lambda bi, i: (0, 0))],
        out_specs=[pl.BlockSpec((1, tm, n), lambda bi, i: (bi, i, 0)),
                   pl.BlockSpec((1, cpt, nk, A_CHUNK), lambda bi, i: (bi, i, 0, 0)),
                   ] + [pl.BlockSpec((1, cpt, A_GATES, A_CHUNK), lambda bi, i: (bi, i, 0, 0))] * 3,
        out_shape=[jax.ShapeDtypeStruct((b, t, n), F32),
                   jax.ShapeDtypeStruct((b, t // A_CHUNK, nk, A_CHUNK), F32),
                   ] + [jax.ShapeDtypeStruct((b, t // A_CHUNK, A_GATES, A_CHUNK), F32)] * 3,
        compiler_params=_params("parallel", "parallel"),
        name="project_mlstm",
    )(h, mods, mods, gain.reshape(1, d), w, wt, gb)


def _mlstm_kernel(*refs, n_ctx, n_lat, need_ctx):
    if need_ctx:
        (ql, vl, ol, ktl, gtl, ctl, wsl, qc, vc, oc, ktc, gtc, ctc, wsc, hn_ref, yl_ref, yc_ref,
         kv_scr, cbd_scr, cst_scr, g_scr, bl_scr, m0_scr) = refs
    else:
        (ql, vl, ol, ktl, gtl, ctl, wsl, qc, vc, ktc, gtc, ctc, wsc, hn_ref, yl_ref,
         kv_scr, cbd_scr, cst_scr, g_scr, bl_scr, m0_scr) = refs
        oc = yc_ref = None
    L = A_CHUNK
    n_all = n_ctx + n_lat
    ones_v = jnp.ones((L, A_DV), BF16)

    def v_ext(v2, hh):
        return jnp.concatenate([v2[:, hh * A_DV:(hh + 1) * A_DV].astype(BF16), ones_v], axis=1)

    def rows(gt, dr, hh):
        return gt[dr * 4 + hh:dr * 4 + hh + 1, :], gt[dr * 4 + 2 + hh:dr * 4 + 3 + hh, :]

    def contrib(c, ct, ws, kt, v2):
        for hh in range(2):
            vx = v_ext(v2, hh)
            kth = kt[hh * A_DQK:(hh + 1) * A_DQK, :]
            for dr in range(2):
                w, g = rows(ws, dr, hh)
                idx = c * 4 + dr * 2 + hh
                kv_scr[idx] = _dot((kth * w).astype(BF16), vx)
                g_scr[pl.ds(idx, 1), :] = g
                bl_scr[pl.ds(idx, 1), :] = rows(ct, dr, hh)[1]

    for c in range(n_ctx):
        contrib(c, ctc[0, c], wsc[0, c], ktc[0, c], vc[0, c * L:(c + 1) * L, :])

    def contrib_lat(c, carry):
        contrib(c + n_ctx, ctl[0, c], wsl[0, c], ktl[0, c], vl[0, pl.ds(pl.multiple_of(c * L, L), L), :])
        return carry

    lax.fori_loop(0, n_lat, contrib_lat, 0, unroll=4)

    cst_scr[...] = jnp.zeros_like(cst_scr)
    zpad = jnp.zeros((A_DQK, 2 * A_DV), BF16)

    def scan_step(i, ms):
        c_bwd = jnp.where(i < n_ctx, n_ctx - 1 - i, n_all - 1 - (i - n_ctx))
        new_ms = []
        for dr, c in ((0, i), (1, c_bwd)):
            c0s = [cst_scr[dr * 2 + hh] for hh in range(2)]
            for hh in range(2):
                idx = c * 4 + dr * 2 + hh
                c0b = c0s[hh].astype(BF16)
                cbd_scr[idx] = jnp.concatenate([zpad, c0b] if hh else [c0b, zpad], axis=0)
                m0 = ms[dr * 2 + hh]
                m0_scr[pl.ds(idx, 1), :] = m0
                g, b_last = g_scr[pl.ds(idx, 1), :], bl_scr[pl.ds(idx, 1), :]
                m_new = jnp.maximum(b_last + m0, g)
                decay = jnp.exp(b_last + m0 - m_new)
                inject = jnp.exp(g - m_new)
                decay, inject = (jnp.concatenate([z, z], axis=1) for z in (decay, inject))
                cst_scr[dr * 2 + hh] = decay * c0s[hh] + inject * kv_scr[idx]
                new_ms.append(m_new)
        return tuple(new_ms)

    lax.fori_loop(0, n_all, scan_step, tuple(jnp.zeros((1, L), F32) for _ in range(4)))

    t_i = lax.broadcasted_iota(jnp.int32, (L, L), 0)
    s_i = lax.broadcasted_iota(jnp.int32, (L, L), 1)
    masks = (s_i <= t_i, s_i >= t_i)
    zk = jnp.zeros((A_DQK, L), BF16)
    zrows = jnp.zeros((L - 2 * SUBLANES, L), F32)

    def outputs(c, gt, ct, kt, q2, v2, o2, store):
        colm = jnp.concatenate([gt, ct, zrows], axis=0).T
        qb = q2.astype(BF16)
        ktb = kt.astype(BF16)
        kt_bd = jnp.concatenate([jnp.concatenate([ktb[:A_DQK], zk], axis=1),
                                 jnp.concatenate([zk, ktb[A_DQK:]], axis=1)], axis=0)
        s2 = _dot(qb, kt_bd)
        vxs = [v_ext(v2, hh) for hh in range(2)]
        hsum = [None, None]
        for dr in range(2):
            for hh in range(2):
                idx = c * 4 + dr * 2 + hh
                li, cum = rows(gt, dr, hh)
                m0 = m0_scr[pl.ds(idx, 1), :]
                cmax = jnp.broadcast_to(colm[:, 8 + dr * 4 + hh:9 + dr * 4 + hh], (L, L))
                cum_t = jnp.broadcast_to(colm[:, dr * 4 + 2 + hh:dr * 4 + 3 + hh], (L, L))
                mm = jnp.maximum(cmax, m0)
                p = jnp.where(masks[dr], jnp.exp((li - cum) - mm), 0.0)
                wq = (p * s2[:, hh * L:(hh + 1) * L]).astype(BF16)
                carry = jnp.exp(m0 - mm)
                lhs = jnp.concatenate([wq, (q2 * carry).astype(BF16)], axis=1)
                ne = _dot(lhs, jnp.concatenate([vxs[hh], cbd_scr[idx]], axis=0))
                h = ne[:, :A_DV] / jnp.maximum(jnp.abs(ne[:, A_DV:]), jnp.exp(-(cum_t + mm)))
                hsum[hh] = h if dr == 0 else hsum[hh] + h
        for hh in range(2):
            hs = hsum[hh]
            hn = hs * lax.rsqrt(jnp.mean(hs * hs, axis=-1, keepdims=True) + NORM_EPS)
            hn = hn * hn_ref[:, hh * A_DV:(hh + 1) * A_DV]
            store(hh, (hn * jax.nn.sigmoid(o2[:, hh * A_DV:(hh + 1) * A_DV])).astype(BF16))

    if need_ctx:
        for c in range(n_ctx):
            sl = slice(c * L, (c + 1) * L)

            def store_ctx(hh, y, sl=sl):
                yc_ref[0, sl, hh * A_DV:(hh + 1) * A_DV] = y

            outputs(c, gtc[0, c], ctc[0, c], ktc[0, c], qc[0, sl, :], vc[0, sl, :], oc[0, sl, :], store_ctx)

    def outputs_lat(c, carry):
        src = pl.ds(pl.multiple_of(c * L, L), L)

        def store_lat(hh, y):
            yl_ref[0, src, hh * A_DV:(hh + 1) * A_DV] = y

        outputs(c + n_ctx, gtl[0, c], ctl[0, c], ktl[0, c], ql[0, src, :], vl[0, src, :], ol[0, src, :], store_lat)
        return carry

    lax.fori_loop(0, n_lat, outputs_lat, 0)


def _mlstm_mix(p, kt, gt, ct, ws, pc, ktc, gtc, ctc, wsc, head_norm, need_ctx):
    b, t, _ = p.shape
    nc = pc.shape[1]
    L = A_CHUNK
    n_lat, n_ctx = t // L, nc // L
    n_all = n_lat + n_ctx
    qw, vw = 2 * A_DQK, 2 * A_DV
    v_blk = A_HEADS * A_DQK // vw
    o_blk = v_blk + A_PAIRS

    def specs(rows, nch, with_o):
        s = [pl.BlockSpec((1, rows, qw), lambda bi, hp: (bi, 0, hp)),
             pl.BlockSpec((1, rows, vw), lambda bi, hp: (bi, 0, v_blk + hp))]
        if with_o:
            s.append(pl.BlockSpec((1, rows, vw), lambda bi, hp: (bi, 0, o_blk + hp)))
        s.append(pl.BlockSpec((1, nch, qw, L), lambda bi, hp: (bi, 0, hp, 0)))
        s += [pl.BlockSpec((1, nch, SUBLANES, L), lambda bi, hp: (bi, 0, hp, 0))] * 3
        return s

    in_specs = specs(t, n_lat, True) + specs(nc, n_ctx, need_ctx)
    in_specs.append(pl.BlockSpec((1, vw), lambda bi, hp: (0, hp)))
    args = [p, p, p, kt, gt, ct, ws] + ([pc, pc, pc] if need_ctx else [pc, pc]) + [ktc, gtc, ctc, wsc, head_norm]
    out_specs = [pl.BlockSpec((1, t, vw), lambda bi, hp: (bi, 0, hp))]
    out_shape = [jax.ShapeDtypeStruct((b, t, A_HEADS * A_DV), BF16)]
    if need_ctx:
        out_specs.append(pl.BlockSpec((1, nc, vw), lambda bi, hp: (bi, 0, hp)))
        out_shape.append(jax.ShapeDtypeStruct((b, nc, A_HEADS * A_DV), BF16))
    n_rows = -(-n_all * 4 // SUBLANES) * SUBLANES
    outs = pl.pallas_call(
        functools.partial(_mlstm_kernel, n_ctx=n_ctx, n_lat=n_lat, need_ctx=need_ctx),
        grid=(b, A_PAIRS),
        in_specs=in_specs,
        out_specs=out_specs,
        out_shape=out_shape,
        scratch_shapes=[pltpu.VMEM((n_all * 4, A_DQK, vw), F32),
                        pltpu.VMEM((n_all * 4, qw, vw), BF16),
                        pltpu.VMEM((4, A_DQK, vw), F32),
                        pltpu.VMEM((n_rows, L), F32), pltpu.VMEM((n_rows, L), F32), pltpu.VMEM((n_rows, L), F32)],
        compiler_params=_params("parallel", "parallel"),
        name="mlstm_mix",
    )(*args)
    return (outs[0], outs[1]) if need_ctx else (outs[0], None)


def _swa_attend(q4, k, v, valid, sink_col):
    s = _dot_nt(q4, k)
    if valid is not None:
        s = jnp.where(valid, s, -jnp.inf)
    m = jnp.maximum(jnp.max(s, axis=-1, keepdims=True), sink_col)
    p = jnp.exp2(s - m)
    den = jnp.sum(p, axis=-1, keepdims=True) + jnp.exp2(sink_col - m)
    return _dot(p.astype(BF16), v) / den


def _swa_heads(q, keys, vals, valid, sink_ref, o_ref, rows):
    row_i = lax.broadcasted_iota(jnp.int32, (SWA_GROUP * rows, 1), 0)
    for hk in range(SWA_KV_HEADS):
        ks = slice(hk * SWA_DH, (hk + 1) * SWA_DH)
        k = jnp.concatenate([x[:, ks] for x in keys], axis=0)
        v = jnp.concatenate([x[:, ks] for x in vals], axis=0)
        q4 = jnp.concatenate([q[:, (hk * SWA_GROUP + g) * SWA_DH:(hk * SWA_GROUP + g + 1) * SWA_DH]
                              for g in range(SWA_GROUP)], axis=0)
        sink_col = jnp.zeros((SWA_GROUP * rows, 1), F32)
        for g in range(SWA_GROUP):
            head = hk * SWA_GROUP + g
            sink_col = jnp.where((row_i >= g * rows) & (row_i < (g + 1) * rows),
                                 sink_ref[:, head:head + 1] * LOG2E, sink_col)
        o4 = _swa_attend(q4, k, v, valid, sink_col)
        for g in range(0, SWA_GROUP, 2):
            col = (hk * SWA_GROUP + g) * SWA_DH
            pair = jnp.concatenate([o4[g * rows:(g + 1) * rows], o4[(g + 1) * rows:(g + 2) * rows]], axis=1)
            o_ref[0, :, col:col + 2 * SWA_DH] = pair.astype(o_ref.dtype)


def _swa_lat_kernel(q_ref, kp_ref, kc_ref, kn_ref, vp_ref, vc_ref, vn_ref, kx_ref, vx_ref, sink_ref, o_ref, *, nblk):
    j = pl.program_id(1)
    L = SWA_BLOCK
    nc = kx_ref.shape[1]
    qi = lax.broadcasted_iota(jnp.int32, (L, L), 0)
    ki = lax.broadcasted_iota(jnp.int32, (L, L), 1)
    valid = jnp.concatenate([(ki >= qi) & (j > 0), jnp.ones((L, L), jnp.bool_), (ki <= qi) & (j < nblk - 1),
                             jnp.ones((L, nc), jnp.bool_)], axis=1)
    valid = jnp.concatenate([valid] * SWA_GROUP, axis=0)
    _swa_heads(q_ref[0], [kp_ref[0], kc_ref[0], kn_ref[0], kx_ref[0]],
               [vp_ref[0], vc_ref[0], vn_ref[0], vx_ref[0]], valid, sink_ref, o_ref, L)


def _swa_ctx_kernel(q_ref, kx_ref, vx_ref, sink_ref, o_ref):
    _swa_heads(q_ref[0], [kx_ref[0]], [vx_ref[0]], None, sink_ref, o_ref, q_ref.shape[1])


def _swa_mix(p, pc, sink, need_ctx):
    b, t, _ = p.shape
    nc = pc.shape[1]
    L = SWA_BLOCK
    nblk = t // L
    qw = SWA_HEADS * SWA_DH
    kvw = SWA_KV_HEADS * SWA_DH
    kblk, vblk = qw // kvw, qw // kvw + 1
    sink2 = sink.reshape(1, SWA_HEADS)

    def kv_spec(col, shift):
        return pl.BlockSpec((1, L, kvw), lambda bi, j: (bi, jnp.clip(j + shift, 0, nblk - 1), col))

    y = pl.pallas_call(
        functools.partial(_swa_lat_kernel, nblk=nblk),
        grid=(b, nblk),
        in_specs=[pl.BlockSpec((1, L, qw), lambda bi, j: (bi, j, 0)),
                  kv_spec(kblk, -1), kv_spec(kblk, 0), kv_spec(kblk, 1),
                  kv_spec(vblk, -1), kv_spec(vblk, 0), kv_spec(vblk, 1),
                  pl.BlockSpec((1, nc, kvw), lambda bi, j: (bi, 0, kblk)),
                  pl.BlockSpec((1, nc, kvw), lambda bi, j: (bi, 0, vblk)),
                  pl.BlockSpec((1, SWA_HEADS), lambda bi, j: (0, 0))],
        out_specs=pl.BlockSpec((1, L, qw), lambda bi, j: (bi, j, 0)),
        out_shape=jax.ShapeDtypeStruct((b, t, qw), BF16),
        compiler_params=_params("parallel", "parallel"),
        name="swa_mix",
    )(p, p, p, p, p, p, p, pc, pc, sink2)
    if not need_ctx:
        return y, None
    yc = pl.pallas_call(
        _swa_ctx_kernel,
        grid=(b,),
        in_specs=[pl.BlockSpec((1, nc, qw), lambda bi: (bi, 0, 0)),
                  pl.BlockSpec((1, nc, kvw), lambda bi: (bi, 0, kblk)),
                  pl.BlockSpec((1, nc, kvw), lambda bi: (bi, 0, vblk)),
                  pl.BlockSpec((1, SWA_HEADS), lambda bi: (0, 0))],
        out_specs=pl.BlockSpec((1, nc, qw), lambda bi: (bi, 0, 0)),
        out_shape=jax.ShapeDtypeStruct((b, nc, qw), BF16),
        compiler_params=_params("parallel"),
        name="swa_ctx_mix",
    )(pc, pc, pc, sink2)
    return y, yc


def _diff_kernel(*refs, lam_init, with_lat, sub_rows):
    if with_lat:
        q_ref, kx_ref, vx_ref, kl_ref, vl_ref, lam_ref, hn_ref, o_ref, k_scr, v_scr = refs
    else:
        q_ref, kx_ref, vx_ref, lam_ref, hn_ref, o_ref, k_scr, v_scr = refs
    nc = kx_ref.shape[1]

    @pl.when(pl.program_id(2) == 0)
    def _():
        k_scr[0:nc, :] = kx_ref[0]
        v_scr[0:nc, 0:DIFF_DV] = vx_ref[0]
        if with_lat:
            k_scr[nc:, :] = kl_ref[0]
            v_scr[nc:, 0:DIFF_DV] = vl_ref[0]
        v_scr[:, DIFF_DV:] = jnp.ones((v_scr.shape[0], DIFF_DV), BF16)

    lam = (jnp.exp(jnp.sum(lam_ref[0:1, :] * lam_ref[1:2, :], axis=-1, keepdims=True))
           - jnp.exp(jnp.sum(lam_ref[2:3, :] * lam_ref[3:4, :], axis=-1, keepdims=True)) + lam_init)
    tq = q_ref.shape[1]
    sub = min(tq, sub_rows)
    lane = lax.broadcasted_iota(jnp.int32, (sub, 2 * DIFF_DH), 1)
    for r0 in range(0, tq, sub):
        q = q_ref[0, r0:r0 + sub, :]
        outs = []
        for m in range(2):
            qm = jnp.where((lane >= DIFF_DH) if m else (lane < DIFF_DH), q, jnp.zeros_like(q))
            s = _dot_nt(qm, k_scr[...])
            p = jnp.exp2(s - jnp.max(s, axis=-1, keepdims=True)).astype(BF16)
            ne = _dot(p, v_scr[...])
            outs.append(ne[:, :DIFF_DV] / ne[:, DIFF_DV:])
        od = outs[0] - lam * outs[1]
        od = od * lax.rsqrt(jnp.mean(od * od, axis=-1, keepdims=True) + NORM_EPS)
        o_ref[0, r0:r0 + sub, :] = (od * hn_ref[...] * (1.0 - lam_init)).astype(o_ref.dtype)


def _diff_mix(p, pc, lam, head_norm, lam_init, need_ctx, tq):
    b, t, _ = p.shape
    nc = pc.shape[1]
    w = DIFF_DV
    kblk, vblk = DIFF_HEADS, 2 * DIFF_HEADS
    out_w = DIFF_HEADS * DIFF_DV

    def call(q_arr, n_q, tile, with_lat, name):
        n_keys = nc + (t if with_lat else 0)
        in_specs = [pl.BlockSpec((1, tile, w), lambda bi, h, i: (bi, i, h)),
                    pl.BlockSpec((1, nc, w), lambda bi, h, i: (bi, 0, kblk + h)),
                    pl.BlockSpec((1, nc, w), lambda bi, h, i: (bi, 0, vblk + h))]
        args = [q_arr, pc, pc]
        if with_lat:
            in_specs += [pl.BlockSpec((1, t, w), lambda bi, h, i: (bi, 0, kblk + h)),
                         pl.BlockSpec((1, t, w), lambda bi, h, i: (bi, 0, vblk + h))]
            args += [p, p]
        in_specs += [pl.BlockSpec((4, DIFF_DH), lambda bi, h, i: (0, 0)),
                     pl.BlockSpec((1, w), lambda bi, h, i: (0, h))]
        args += [lam, head_norm]
        return pl.pallas_call(
            functools.partial(_diff_kernel, lam_init=lam_init, with_lat=with_lat, sub_rows=256),
            grid=(b, DIFF_HEADS, n_q // tile),
            in_specs=in_specs,
            out_specs=pl.BlockSpec((1, tile, w), lambda bi, h, i: (bi, i, h)),
            out_shape=jax.ShapeDtypeStruct((b, n_q, out_w), BF16),
            scratch_shapes=[pltpu.VMEM((n_keys, w), BF16), pltpu.VMEM((n_keys, 2 * w), BF16)],
            compiler_params=_params("parallel", "parallel", "arbitrary"),
            name=name,
        )(*args)

    y = call(p, t, tq, True, "diff_mix")
    yc = call(pc, nc, nc, False, "diff_ctx_mix") if need_ctx else None
    return y, yc


def _post_kernel(*refs, ctx_row, sub_rows, ff_chunk, final):
    if final:
        h_ref, y_ref, g2_ref, sh_ref, sc_ref, g5_ref, gain_ref, wo_ref, w1_ref, w2_ref, fn_ref, o_ref = refs
    else:
        h_ref, y_ref, g2_ref, sh_ref, sc_ref, g5_ref, gain_ref, wo_ref, w1_ref, w2_ref, o_ref = refs
    row = pl.program_id(0) if ctx_row is None else ctx_row
    g2, g5 = _mod_row(g2_ref, row), _mod_row(g5_ref, row)
    shift, scale = _mod_row(sh_ref, row), _mod_row(sc_ref, row)
    tm = h_ref.shape[1]
    ff = w1_ref.shape[1]
    for r0 in range(0, tm, sub_rows):
        rs = slice(r0, r0 + sub_rows)
        h1 = h_ref[0, rs, :] + g2 * _dot(y_ref[0, rs, :], wo_ref[...])
        u = _norm_mod(h1, gain_ref[...], shift, scale).astype(BF16)
        acc = None
        for c0 in range(0, ff, ff_chunk):
            hidden = jnp.square(jnp.maximum(_dot(u, w1_ref[:, c0:c0 + ff_chunk]), 0.0)).astype(BF16)
            part = _dot(hidden, w2_ref[c0:c0 + ff_chunk, :])
            acc = part if acc is None else acc + part
        out = h1 + g5 * acc
        if final:
            out = out * lax.rsqrt(jnp.mean(out * out, axis=-1, keepdims=True) + NORM_EPS) * fn_ref[...]
        o_ref[0, rs, :] = out


def _post(h, y, mods, layer, gain, wo, w1, w2, *, ctx_row=None, tm, final_gain=None):
    b, t, d = h.shape
    dy = y.shape[2]
    ff = w1.shape[1]
    r = mods.shape[1]
    assert t % tm == 0
    final = final_gain is not None

    def mod_spec(k):
        return pl.BlockSpec((1, r, d), lambda bi, i: (layer, 0, k))

    def resident(shape):
        return pl.BlockSpec(shape, lambda bi, i: (0, 0), pipeline_mode=pl.Buffered(1))

    in_specs = [pl.BlockSpec((1, tm, d), lambda bi, i: (bi, i, 0)),
                pl.BlockSpec((1, tm, dy), lambda bi, i: (bi, i, 0)),
                mod_spec(2), mod_spec(3), mod_spec(4), mod_spec(5),
                pl.BlockSpec((1, d), lambda bi, i: (0, 0)),
                resident((dy, d)), resident((d, ff)), resident((ff, d))]
    args = [h, y, mods, mods, mods, mods, gain.reshape(1, d), wo, w1, w2]
    if final:
        in_specs.append(pl.BlockSpec((1, d), lambda bi, i: (0, 0)))
        args.append(final_gain.reshape(1, d))
    return pl.pallas_call(
        functools.partial(_post_kernel, ctx_row=ctx_row, sub_rows=min(tm, 512), ff_chunk=1024, final=final),
        grid=(b, t // tm),
        in_specs=in_specs,
        out_specs=pl.BlockSpec((1, tm, d), lambda bi, i: (bi, i, 0)),
        out_shape=jax.ShapeDtypeStruct((b, t, d), F32),
        compiler_params=_params("parallel", "parallel"),
        name="post",
    )(*args)


def _rope_tables(n_tok, head_dim):
    rows = n_tok // GRID_W
    row = jnp.repeat(jnp.arange(rows, dtype=jnp.int32), GRID_W).astype(F32)
    col = jnp.tile(jnp.arange(GRID_W, dtype=jnp.int32), rows).astype(F32)
    quarter = head_dim // 4
    inv = ROPE_BASE ** (-jnp.arange(quarter, dtype=F32) / quarter)
    ang = jnp.concatenate([row[:, None] * inv, col[:, None] * inv], axis=-1)
    cos, sin = jnp.cos(ang), jnp.sin(ang)
    return jnp.tile(cos, (1, 4)), jnp.tile(jnp.concatenate([-sin, sin], axis=-1), (1, 2))


def _mlstm_weights(w_in, gate_b):
    d = w_in.shape[0]
    nk = A_HEADS * A_DQK
    main = 2 * nk + 2 * A_HEADS * A_DV
    w = jnp.concatenate([w_in[:, :nk], w_in[:, 2 * nk:main]], axis=1).astype(BF16)
    wg = jnp.transpose(w_in[:, main:].reshape(d, 4, A_PAIRS, 2), (0, 2, 1, 3)).reshape(d, A_GATES)
    wt = jnp.concatenate([w_in[:, nk:2 * nk], wg], axis=1).T.astype(BF16)
    gb = jnp.transpose(gate_b.astype(F32).reshape(4, A_PAIRS, 2), (1, 0, 2)).reshape(A_GATES, 1)
    return w, wt, jnp.broadcast_to(gb, (A_GATES, LANES))


def kernel(x, c, ctx, c_ctx, ada_w, ada_b, norm_mix, norm_ffn, ffn_w1, ffn_w2, mlstm_w_in, mlstm_gate_b, mlstm_head_norm, mlstm_w_out, swa_w_in, swa_sink, swa_w_out, diff_w_in, diff_lambda_q1, diff_lambda_k1, diff_lambda_q2, diff_lambda_k2, diff_head_norm, diff_w_out, final_norm):
    bsz, n_tok, d = x.shape
    n_ctx = ctx.shape[1]
    depth = ada_w.shape[0]
    rows = -(-(bsz + 1) // SUBLANES) * SUBLANES
    cond = jnp.concatenate([c, c_ctx[None, :], jnp.zeros((rows - bsz - 1, d), F32)], axis=0)
    mods = _ada_table(cond, ada_w, ada_b)
    rope = _rope_tables(n_tok, SWA_DH)
    tm_lat, tm_post = 512, 512

    h, hc = x, ctx
    for i in range(depth):
        kind, slot = i % N_MIXERS, i // N_MIXERS
        need_ctx = i < depth - 1
        if kind == 0:
            w, wt, gb = _mlstm_weights(mlstm_w_in[slot], mlstm_gate_b[slot])
            proj = functools.partial(_project_mlstm, mods=mods, layer=i, gain=norm_mix[i], w=w, wt=wt, gb=gb)
            lat = proj(h, tm=tm_lat)
            cx = proj(hc, tm=n_ctx, ctx_row=bsz)
            y, yc = _mlstm_mix(*lat, *cx, mlstm_head_norm[slot].reshape(1, -1), need_ctx)
            wo = mlstm_w_out[slot]
        else:
            proj = functools.partial(_project, mods=mods, layer=i, gain=norm_mix[i])
            if kind == 1:
                w = swa_w_in[slot].astype(BF16)
                rc = (SWA_HEADS + SWA_KV_HEADS) * SWA_DH
                qs = (SWA_HEADS * SWA_DH, SWA_DH ** -0.5 * LOG2E)
            else:
                w = diff_w_in[slot].astype(BF16)
                rc = 4 * DIFF_HEADS * DIFF_DH
                qs = (2 * DIFF_HEADS * DIFF_DH, DIFF_DH ** -0.5 * LOG2E)
            p = proj(h, w=w, tm=tm_lat, rope=rope, rope_cols=rc, qscale=qs)
            pc = proj(hc, w=w, tm=n_ctx, ctx_row=bsz, qscale=qs)
            if kind == 1:
                y, yc = _swa_mix(p, pc, swa_sink[slot], need_ctx)
                wo = swa_w_out[slot]
            else:
                lam = jnp.stack([diff_lambda_q1[slot], diff_lambda_k1[slot], diff_lambda_q2[slot], diff_lambda_k2[slot]])
                lam_init = 0.8 - 0.6 * math.exp(-0.3 * i)
                y, yc = _diff_mix(p, pc, lam.astype(F32), diff_head_norm[slot].reshape(1, -1), lam_init, need_ctx, tq=512)
                wo = diff_w_out[slot]
        post = functools.partial(_post, mods=mods, layer=i, gain=norm_ffn[i], wo=wo.astype(BF16),
                                 w1=ffn_w1[i].astype(BF16), w2=ffn_w2[i].astype(BF16))
        h = post(h, y, tm=tm_post, final_gain=None if need_ctx else final_norm)
        if need_ctx:
            hc = post(hc, yc, tm=n_ctx, ctx_row=bsz)
    return h
```

```python
import functools
import math

import jax
import jax.numpy as jnp
from jax import lax
from jax.experimental import pallas as pl
from jax.experimental.pallas import tpu as pltpu

F32 = jnp.float32
BF16 = jnp.bfloat16

LANES = 128
SUBLANES = 8
VMEM_LIMIT_BYTES = 56 * 1024 * 1024
LOG2E = math.log2(math.e)

NORM_EPS = 1e-6
ROPE_BASE = 10000.0
GRID_W = 64
N_MIXERS = 3

A_HEADS = 8
A_DQK = 64
A_DV = 128
A_CHUNK = 128
A_PAIRS = A_HEADS // 2
A_GATES = 4 * A_HEADS

SWA_HEADS = 16
SWA_KV_HEADS = 4
SWA_DH = 64
SWA_GROUP = SWA_HEADS // SWA_KV_HEADS
SWA_BLOCK = 128

DIFF_HEADS = 8
DIFF_DH = 64
DIFF_DV = 128


def _params(*sem):
    return pltpu.CompilerParams(dimension_semantics=sem, vmem_limit_bytes=VMEM_LIMIT_BYTES)


def _dot(a, b):
    return jnp.dot(a, b, preferred_element_type=F32)


def _dot_nt(a, b):
    return lax.dot_general(a, b, (((1,), (1,)), ((), ())), preferred_element_type=F32)


def _norm_mod(x, gain, shift, scale):
    y = x * lax.rsqrt(jnp.mean(x * x, axis=-1, keepdims=True) + NORM_EPS) * gain
    return y * (1.0 + scale) + shift


def _mod_row(ref, row):
    return ref[0, pl.ds(row, 1), :]


def _ada_kernel(c_ref, w_ref, b_ref, o_ref):
    c = c_ref[...]
    s = (c * jax.nn.sigmoid(c)).astype(BF16)
    o_ref[0] = _dot(s, w_ref[0].astype(BF16)) + b_ref[0]


def _ada_table(cond, ada_w, ada_b):
    depth, d, n = ada_w.shape
    r = cond.shape[0]
    tn = n // 4
    return pl.pallas_call(
        _ada_kernel,
        grid=(depth, n // tn),
        in_specs=[pl.BlockSpec((r, d), lambda i, j: (0, 0)),
                  pl.BlockSpec((1, d, tn), lambda i, j: (i, 0, j)),
                  pl.BlockSpec((1, 1, tn), lambda i, j: (i, 0, j))],
        out_specs=pl.BlockSpec((1, r, tn), lambda i, j: (i, 0, j)),
        out_shape=jax.ShapeDtypeStruct((depth, r, n), F32),
        compiler_params=_params("parallel", "parallel"),
        name="ada_table",
    )(cond, ada_w, ada_b.reshape(depth, 1, n))


def _rope_block(blk, cos, sin_signed):
    lane = lax.broadcasted_iota(jnp.int32, blk.shape, 1)
    first_half = (lane & 32) == 0
    partner = jnp.where(first_half, pltpu.roll(blk, LANES - 32, 1), pltpu.roll(blk, 32, 1))
    return blk * cos + partner * sin_signed


def _proj_kernel(*refs, ctx_row, n_out, rope_cols, qscale, chunk):
    if rope_cols:
        x_ref, sh_ref, sc_ref, g_ref, w_ref, cos_ref, sin_ref, o_ref = refs
    else:
        x_ref, sh_ref, sc_ref, g_ref, w_ref, o_ref = refs
    row = pl.program_id(0) if ctx_row is None else ctx_row
    u = _norm_mod(x_ref[0], g_ref[...], _mod_row(sh_ref, row), _mod_row(sc_ref, row)).astype(BF16)
    for c0 in range(0, n_out, chunk):
        acc = _dot(u, w_ref[:, c0:c0 + chunk])
        for l0 in range(0, chunk, LANES):
            col = c0 + l0
            blk = acc[:, l0:l0 + LANES]
            if col < rope_cols:
                blk = _rope_block(blk, cos_ref[...], sin_ref[...])
            if col < qscale[0]:
                blk = blk * qscale[1]
            o_ref[0, :, col:col + LANES] = blk.astype(o_ref.dtype)


def _project(h, mods, layer, gain, w, *, ctx_row=None, tm, rope=None, rope_cols=0, qscale):
    b, t, d = h.shape
    n = w.shape[1]
    r = mods.shape[1]
    chunk = 512
    assert t % tm == 0 and n % chunk == 0
    in_specs = [pl.BlockSpec((1, tm, d), lambda bi, i: (bi, i, 0)),
                pl.BlockSpec((1, r, d), lambda bi, i: (layer, 0, 0)),
                pl.BlockSpec((1, r, d), lambda bi, i: (layer, 0, 1)),
                pl.BlockSpec((1, d), lambda bi, i: (0, 0)),
                pl.BlockSpec((d, n), lambda bi, i: (0, 0))]
    args = [h, mods, mods, gain.reshape(1, d), w]
    if rope_cols:
        in_specs += [pl.BlockSpec((tm, LANES), lambda bi, i: (i, 0))] * 2
        args += list(rope)
    return pl.pallas_call(
        functools.partial(_proj_kernel, ctx_row=ctx_row, n_out=n, rope_cols=rope_cols, qscale=qscale, chunk=chunk),
        grid=(b, t // tm),
        in_specs=in_specs,
        out_specs=pl.BlockSpec((1, tm, n), lambda bi, i: (bi, i, 0)),
        out_shape=jax.ShapeDtypeStruct((b, t, n), BF16),
        compiler_params=_params("parallel", "parallel"),
        name="project",
    )(*args)


def _log_sigmoid(x):
    return jnp.minimum(x, 0.0) - jnp.log1p(jnp.exp(-jnp.abs(x)))


def _lane_scan(x, op, fill, reverse):
    lane = lax.broadcasted_iota(jnp.int32, x.shape, 1)
    k = 1
    while k < LANES:
        if reverse:
            shifted, ok = pltpu.roll(x, LANES - k, 1), lane < LANES - k
        else:
            shifted, ok = pltpu.roll(x, k, 1), lane >= k
        x = op(x, jnp.where(ok, shifted, fill))
        k *= 2
    return x


def _proj_mlstm_kernel(x_ref, sh_ref, sc_ref, g_ref, w_ref, wt_ref, gb_ref, o_ref, kt_ref, gt_ref, ct_ref, ws_ref,
                       *, ctx_row, n_out, chunk):
    row = pl.program_id(0) if ctx_row is None else ctx_row
    u = _norm_mod(x_ref[0], g_ref[...], _mod_row(sh_ref, row), _mod_row(sc_ref, row)).astype(BF16)
    ut = _dot_nt(wt_ref[...], u)
    for c0 in range(0, n_out, chunk):
        o_ref[0, :, c0:c0 + chunk] = _dot(u, w_ref[:, c0:c0 + chunk])
    nk = A_HEADS * A_DQK
    row8 = lax.broadcasted_iota(jnp.int32, (A_GATES, LANES), 0) & 7
    fwd = row8 < 4
    is_cum = (row8 & 2) != 0
    for ci in range(u.shape[0] // A_CHUNK):
        cols = slice(ci * A_CHUNK, (ci + 1) * A_CHUNK)
        kt_ref[0, ci] = ut[:nk, cols] * (A_DQK ** -0.5)
        x = ut[nk:, cols] + gb_ref[...]
        lf = _log_sigmoid(x)
        cum = jnp.where(fwd, _lane_scan(lf, jnp.add, 0.0, False), _lane_scan(lf, jnp.add, 0.0, True))
        gt_ref[0, ci] = jnp.where(is_cum, cum, x)
        cum_up = pltpu.roll(cum, A_GATES - 2, 0)
        r = x - cum_up
        cmax = jnp.where(fwd, _lane_scan(r, jnp.maximum, -jnp.inf, False), _lane_scan(r, jnp.maximum, -jnp.inf, True))
        b_last = jnp.where(fwd, jnp.broadcast_to(cum_up[:, A_CHUNK - 1:A_CHUNK], cum_up.shape),
                           jnp.broadcast_to(cum_up[:, 0:1], cum_up.shape))
        a = (b_last - cum_up) + x
        g = jnp.broadcast_to(jnp.max(a, axis=-1, keepdims=True), a.shape)
        ct_ref[0, ci] = jnp.where(is_cum, pltpu.roll(b_last, 2, 0), cmax)
        ws_ref[0, ci] = jnp.where(is_cum, pltpu.roll(g, 2, 0), jnp.exp(a - g))


def _project_mlstm(h, mods, layer, gain, w, wt, gb, *, ctx_row=None, tm):
    b, t, d = h.shape
    n = w.shape[1]
    r = mods.shape[1]
    nt = wt.shape[0]
    nk = A_HEADS * A_DQK
    chunk = 512
    cpt = tm // A_CHUNK
    assert t % tm == 0 and n % chunk == 0 and tm % A_CHUNK == 0
    return pl.pallas_call(
        functools.partial(_proj_mlstm_kernel, ctx_row=ctx_row, n_out=n, chunk=chunk),
        grid=(b, t // tm),
        in_specs=[pl.BlockSpec((1, tm, d), lambda bi, i: (bi, i, 0)),
                  pl.BlockSpec((1, r, d), lambda bi, i: (layer, 0, 0)),
                  pl.BlockSpec((1, r, d), lambda bi, i: (layer, 0, 1)),
                  pl.BlockSpec((1, d), lambda bi, i: (0, 0)),
                  pl.BlockSpec((d, n), lambda bi, i: (0, 0)),
                  pl.BlockSpec((nt, d), lambda bi, i: (0, 0)),
                  pl.BlockSpec((A_GATES, LANES), lambda bi, i: (0, 0))],
        out_specs=[pl.BlockSpec((1, tm, n), lambda bi, i: (bi, i, 0)),
                   pl.BlockSpec((1, cpt, nk, A_CHUNK), lambda bi, i: (bi, i, 0, 0)),
                   ] + [pl.BlockSpec((1, cpt, A_GATES, A_CHUNK), lambda bi, i: (bi, i, 0, 0))] * 3,
        out_shape=[jax.ShapeDtypeStruct((b, t, n), F32),
                   jax.ShapeDtypeStruct((b, t // A_CHUNK, nk, A_CHUNK), F32),
                   ] + [jax.ShapeDtypeStruct((b, t // A_CHUNK, A_GATES, A_CHUNK), F32)] * 3,
        compiler_params=_params("parallel", "parallel"),
        name="project_mlstm",
    )(h, mods, mods, gain.reshape(1, d), w, wt, gb)


def _mlstm_kernel(*refs, n_ctx, n_lat, need_ctx):
    if need_ctx:
        (ql, vl, ol, ktl, gtl, ctl, wsl, qc, vc, oc, ktc, gtc, ctc, wsc, hn_ref, yl_ref, yc_ref,
         kv_scr, cbd_scr, cst_scr, g_scr, bl_scr, m0_scr) = refs
    else:
        (ql, vl, ol, ktl, gtl, ctl, wsl, qc, vc, ktc, gtc, ctc, wsc, hn_ref, yl_ref,
         kv_scr, cbd_scr, cst_scr, g_scr, bl_scr, m0_scr) = refs
        oc = yc_ref = None
    L = A_CHUNK
    n_all = n_ctx + n_lat
    ones_v = jnp.ones((L, A_DV), BF16)

    def v_ext(v2, hh):
        return jnp.concatenate([v2[:, hh * A_DV:(hh + 1) * A_DV].astype(BF16), ones_v], axis=1)

    def rows(gt, dr, hh):
        return gt[dr * 4 + hh:dr * 4 + hh + 1, :], gt[dr * 4 + 2 + hh:dr * 4 + 3 + hh, :]

    def contrib(c, ct, ws, kt, v2):
        for hh in range(2):
            vx = v_ext(v2, hh)
            kth = kt[hh * A_DQK:(hh + 1) * A_DQK, :]
            for dr in range(2):
                w, g = rows(ws, dr, hh)
                idx = c * 4 + dr * 2 + hh
                kv_scr[idx] = _dot((kth * w).astype(BF16), vx)
                g_scr[pl.ds(idx, 1), :] = g
                bl_scr[pl.ds(idx, 1), :] = rows(ct, dr, hh)[1]

    for c in range(n_ctx):
        contrib(c, ctc[0, c], wsc[0, c], ktc[0, c], vc[0, c * L:(c + 1) * L, :])

    def contrib_lat(c, carry):
        contrib(c + n_ctx, ctl[0, c], wsl[0, c], ktl[0, c], vl[0, pl.ds(pl.multiple_of(c * L, L), L), :])
        return carry

    lax.fori_loop(0, n_lat, contrib_lat, 0, unroll=4)

    cst_scr[...] = jnp.zeros_like(cst_scr)
    zpad = jnp.zeros((A_DQK, 2 * A_DV), BF16)

    def scan_step(i, ms):
        c_bwd = jnp.where(i < n_ctx, n_ctx - 1 - i, n_all - 1 - (i - n_ctx))
        new_ms = []
        for dr, c in ((0, i), (1, c_bwd)):
            c0s = [cst_scr[dr * 2 + hh] for hh in range(2)]
            for hh in range(2):
                idx = c * 4 + dr * 2 + hh
                c0b = c0s[hh].astype(BF16)
                cbd_scr[idx] = jnp.concatenate([zpad, c0b] if hh else [c0b, zpad], axis=0)
                m0 = ms[dr * 2 + hh]
                m0_scr[pl.ds(idx, 1), :] = m0
                g, b_last = g_scr[pl.ds(idx, 1), :], bl_scr[pl.ds(idx, 1), :]
                m_new = jnp.maximum(b_last + m0, g)
                decay = jnp.exp(b_last + m0 - m_new)
                inject = jnp.exp(g - m_new)
                decay, inject = (jnp.concatenate([z, z], axis=1) for z in (decay, inject))
                cst_scr[dr * 2 + hh] = decay * c0s[hh] + inject * kv_scr[idx]
                new_ms.append(m_new)
        return tuple(new_ms)

    lax.fori_loop(0, n_all, scan_step, tuple(jnp.zeros((1, L), F32) for _ in range(4)))

    t_i = lax.broadcasted_iota(jnp.int32, (L, L), 0)
    s_i = lax.broadcasted_iota(jnp.int32, (L, L), 1)
    masks = (s_i <= t_i, s_i >= t_i)
    zk = jnp.zeros((A_DQK, L), BF16)
    zrows = jnp.zeros((L - 2 * SUBLANES, L), F32)

    def outputs(c, gt, ct, kt, q2, v2, o2, store):
        colm = jnp.concatenate([gt, ct, zrows], axis=0).T
        qb = q2.astype(BF16)
        ktb = kt.astype(BF16)
        kt_bd = jnp.concatenate([jnp.concatenate([ktb[:A_DQK], zk], axis=1),
                                 jnp.concatenate([zk, ktb[A_DQK:]], axis=1)], axis=0)
        s2 = _dot(qb, kt_bd)
        vxs = [v_ext(v2, hh) for hh in range(2)]
        hsum = [None, None]
        for dr in range(2):
            for hh in range(2):
                idx = c * 4 + dr * 2 + hh
                li, cum = rows(gt, dr, hh)
                m0 = m0_scr[pl.ds(idx, 1), :]
                cmax = jnp.broadcast_to(colm[:, 8 + dr * 4 + hh:9 + dr * 4 + hh], (L, L))
                cum_t = jnp.broadcast_to(colm[:, dr * 4 + 2 + hh:dr * 4 + 3 + hh], (L, L))
                mm = jnp.maximum(cmax, m0)
                p = jnp.where(masks[dr], jnp.exp((li - cum) - mm), 0.0)
                wq = (p * s2[:, hh * L:(hh + 1) * L]).astype(BF16)
                carry = jnp.exp(m0 - mm)
                lhs = jnp.concatenate([wq, (q2 * carry).astype(BF16)], axis=1)
                ne = _dot(lhs, jnp.concatenate([vxs[hh], cbd_scr[idx]], axis=0))
                h = ne[:, :A_DV] / jnp.maximum(jnp.abs(ne[:, A_DV:]), jnp.exp(-(cum_t + mm)))
                hsum[hh] = h if dr == 0 else hsum[hh] + h
        for hh in range(2):
            hs = hsum[hh]
            hn = hs * lax.rsqrt(jnp.mean(hs * hs, axis=-1, keepdims=True) + NORM_EPS)
            hn = hn * hn_ref[:, hh * A_DV:(hh + 1) * A_DV]
            store(hh, (hn * jax.nn.sigmoid(o2[:, hh * A_DV:(hh + 1) * A_DV])).astype(BF16))

    if need_ctx:
        for c in range(n_ctx):
            sl = slice(c * L, (c + 1) * L)

            def store_ctx(hh, y, sl=sl):
                yc_ref[0, sl, hh * A_DV:(hh + 1) * A_DV] = y

            outputs(c, gtc[0, c], ctc[0, c], ktc[0, c], qc[0, sl, :], vc[0, sl, :], oc[0, sl, :], store_ctx)

    def outputs_lat(c, carry):
        src = pl.ds(pl.multiple_of(c * L, L), L)

        def store_lat(hh, y):
            yl_ref[0, src, hh * A_DV:(hh + 1) * A_DV] = y

        outputs(c + n_ctx, gtl[0, c], ctl[0, c], ktl[0, c], ql[0, src, :], vl[0, src, :], ol[0, src, :], store_lat)
        return carry

    lax.fori_loop(0, n_lat, outputs_lat, 0)


def _mlstm_mix(p, kt, gt, ct, ws, pc, ktc, gtc, ctc, wsc, head_norm, need_ctx):
    b, t, _ = p.shape
    nc = pc.shape[1]
    L = A_CHUNK
    n_lat, n_ctx = t // L, nc // L
    n_all = n_lat + n_ctx
    qw, vw = 2 * A_DQK, 2 * A_DV
    v_blk = A_HEADS * A_DQK // vw
    o_blk = v_blk + A_PAIRS

    def specs(rows, nch, with_o):
        s = [pl.BlockSpec((1, rows, qw), lambda bi, hp: (bi, 0, hp)),
             pl.BlockSpec((1, rows, vw), lambda bi, hp: (bi, 0, v_blk + hp))]
        if with_o:
            s.append(pl.BlockSpec((1, rows, vw), lambda bi, hp: (bi, 0, o_blk + hp)))
        s.append(pl.BlockSpec((1, nch, qw, L), lambda bi, hp: (bi, 0, hp, 0)))
        s += [pl.BlockSpec((1, nch, SUBLANES, L), lambda bi, hp: (bi, 0, hp, 0))] * 3
        return s

    in_specs = specs(t, n_lat, True) + specs(nc, n_ctx, need_ctx)
    in_specs.append(pl.BlockSpec((1, vw), lambda bi, hp: (0, hp)))
    args = [p, p, p, kt, gt, ct, ws] + ([pc, pc, pc] if need_ctx else [pc, pc]) + [ktc, gtc, ctc, wsc, head_norm]
    out_specs = [pl.BlockSpec((1, t, vw), lambda bi, hp: (bi, 0, hp))]
    out_shape = [jax.ShapeDtypeStruct((b, t, A_HEADS * A_DV), BF16)]
    if need_ctx:
        out_specs.append(pl.BlockSpec((1, nc, vw), lambda bi, hp: (bi, 0, hp)))
        out_shape.append(jax.ShapeDtypeStruct((b, nc, A_HEADS * A_DV), BF16))
    n_rows = -(-n_all * 4 // SUBLANES) * SUBLANES
    outs = pl.pallas_call(
        functools.partial(_mlstm_kernel, n_ctx=n_ctx, n_lat=n_lat, need_ctx=need_ctx),
        grid=(b, A_PAIRS),
        in_specs=in_specs,
        out_specs=out_specs,
        out_shape=out_shape,
        scratch_shapes=[pltpu.VMEM((n_all * 4, A_DQK, vw), F32),
                        pltpu.VMEM((n_all * 4, qw, vw), BF16),
                        pltpu.VMEM((4, A_DQK, vw), F32),
                        pltpu.VMEM((n_rows, L), F32), pltpu.VMEM((n_rows, L), F32), pltpu.VMEM((n_rows, L), F32)],
        compiler_params=_params("parallel", "parallel"),
        name="mlstm_mix",
    )(*args)
    return (outs[0], outs[1]) if need_ctx else (outs[0], None)


def _swa_attend(q4, k, v, valid, sink_col):
    s = _dot_nt(q4, k)
    if valid is not None:
        s = jnp.where(valid, s, -jnp.inf)
    m = jnp.maximum(jnp.max(s, axis=-1, keepdims=True), sink_col)
    p = jnp.exp2(s - m)
    den = jnp.sum(p, axis=-1, keepdims=True) + jnp.exp2(sink_col - m)
    return _dot(p.astype(BF16), v) / den


def _swa_heads(q, keys, vals, valid, sink_ref, o_ref, rows):
    row_i = lax.broadcasted_iota(jnp.int32, (SWA_GROUP * rows, 1), 0)
    for hk in range(SWA_KV_HEADS):
        ks = slice(hk * SWA_DH, (hk + 1) * SWA_DH)
        k = jnp.concatenate([x[:, ks] for x in keys], axis=0)
        v = jnp.concatenate([x[:, ks] for x in vals], axis=0)
        q4 = jnp.concatenate([q[:, (hk * SWA_GROUP + g) * SWA_DH:(hk * SWA_GROUP + g + 1) * SWA_DH]
                              for g in range(SWA_GROUP)], axis=0)
        sink_col = jnp.zeros((SWA_GROUP * rows, 1), F32)
        for g in range(SWA_GROUP):
            head = hk * SWA_GROUP + g
            sink_col = jnp.where((row_i >= g * rows) & (row_i < (g + 1) * rows),
                                 sink_ref[:, head:head + 1] * LOG2E, sink_col)
        o4 = _swa_attend(q4, k, v, valid, sink_col)
        for g in range(0, SWA_GROUP, 2):
            col = (hk * SWA_GROUP + g) * SWA_DH
            pair = jnp.concatenate([o4[g * rows:(g + 1) * rows], o4[(g + 1) * rows:(g + 2) * rows]], axis=1)
            o_ref[0, :, col:col + 2 * SWA_DH] = pair.astype(o_ref.dtype)


def _swa_lat_kernel(*refs, n_tok, nq):
    nkb = nq + 2
    q_ref, k_refs, v_refs = refs[0], refs[1:1 + nkb], refs[1 + nkb:1 + 2 * nkb]
    kx_ref, vx_ref, sink_ref, o_ref = refs[1 + 2 * nkb:]
    j = pl.program_id(1)
    L = SWA_BLOCK
    nc = kx_ref.shape[1]
    qi = lax.broadcasted_iota(jnp.int32, (nq * L, nkb * L), 0)
    ki = lax.broadcasted_iota(jnp.int32, (nq * L, nkb * L), 1)
    rel = ki - L - qi
    k_lo = (1 - j * nq) * L
    valid = (jnp.abs(rel) <= L) & (ki >= k_lo) & (ki < k_lo + n_tok)
    valid = jnp.concatenate([valid, jnp.ones((nq * L, nc), jnp.bool_)], axis=1)
    valid = jnp.concatenate([valid] * SWA_GROUP, axis=0)
    _swa_heads(q_ref[0], [r[0] for r in k_refs] + [kx_ref[0]], [r[0] for r in v_refs] + [vx_ref[0]],
               valid, sink_ref, o_ref, nq * L)


def _swa_ctx_kernel(q_ref, kx_ref, vx_ref, sink_ref, o_ref):
    _swa_heads(q_ref[0], [kx_ref[0]], [vx_ref[0]], None, sink_ref, o_ref, q_ref.shape[1])


def _swa_mix(p, pc, sink, need_ctx):
    b, t, _ = p.shape
    nc = pc.shape[1]
    L = SWA_BLOCK
    nblk = t // L
    nq = 1
    qw = SWA_HEADS * SWA_DH
    kvw = SWA_KV_HEADS * SWA_DH
    kblk, vblk = qw // kvw, qw // kvw + 1
    sink2 = sink.reshape(1, SWA_HEADS)

    def kv_spec(col, shift):
        return pl.BlockSpec((1, L, kvw), lambda bi, j: (bi, jnp.clip(j * nq + shift, 0, nblk - 1), col))

    shifts = range(-1, nq + 1)
    y = pl.pallas_call(
        functools.partial(_swa_lat_kernel, n_tok=t, nq=nq),
        grid=(b, nblk // nq),
        in_specs=[pl.BlockSpec((1, nq * L, qw), lambda bi, j: (bi, j, 0))]
        + [kv_spec(kblk, s) for s in shifts] + [kv_spec(vblk, s) for s in shifts]
        + [pl.BlockSpec((1, nc, kvw), lambda bi, j: (bi, 0, kblk)),
           pl.BlockSpec((1, nc, kvw), lambda bi, j: (bi, 0, vblk)),
           pl.BlockSpec((1, SWA_HEADS), lambda bi, j: (0, 0))],
        out_specs=pl.BlockSpec((1, nq * L, qw), lambda bi, j: (bi, j, 0)),
        out_shape=jax.ShapeDtypeStruct((b, t, qw), BF16),
        compiler_params=_params("parallel", "parallel"),
        name="swa_mix",
    )(*([p] * (1 + 2 * (nq + 2)) + [pc, pc, sink2]))
    if not need_ctx:
        return y, None
    yc = pl.pallas_call(
        _swa_ctx_kernel,
        grid=(b,),
        in_specs=[pl.BlockSpec((1, nc, qw), lambda bi: (bi, 0, 0)),
                  pl.BlockSpec((1, nc, kvw), lambda bi: (bi, 0, kblk)),
                  pl.BlockSpec((1, nc, kvw), lambda bi: (bi, 0, vblk)),
                  pl.BlockSpec((1, SWA_HEADS), lambda bi: (0, 0))],
        out_specs=pl.BlockSpec((1, nc, qw), lambda bi: (bi, 0, 0)),
        out_shape=jax.ShapeDtypeStruct((b, nc, qw), BF16),
        compiler_params=_params("parallel"),
        name="swa_ctx_mix",
    )(pc, pc, pc, sink2)
    return y, yc


def _diff_kernel(*refs, lam_init, with_lat, sub_rows):
    if with_lat:
        q_ref, kx_ref, vx_ref, kl_ref, vl_ref, lam_ref, hn_ref, o_ref, k_scr, v_scr = refs
    else:
        q_ref, kx_ref, vx_ref, lam_ref, hn_ref, o_ref, k_scr, v_scr = refs
    nc = kx_ref.shape[1]

    @pl.when(pl.program_id(2) == 0)
    def _():
        k_scr[0:nc, :] = kx_ref[0]
        v_scr[0:nc, 0:DIFF_DV] = vx_ref[0]
        if with_lat:
            k_scr[nc:, :] = kl_ref[0]
            v_scr[nc:, 0:DIFF_DV] = vl_ref[0]
        v_scr[:, DIFF_DV:] = jnp.ones((v_scr.shape[0], DIFF_DV), BF16)

    lam = (jnp.exp(jnp.sum(lam_ref[0:1, :] * lam_ref[1:2, :], axis=-1, keepdims=True))
           - jnp.exp(jnp.sum(lam_ref[2:3, :] * lam_ref[3:4, :], axis=-1, keepdims=True)) + lam_init)
    tq = q_ref.shape[1]
    sub = min(tq, sub_rows)
    lane = lax.broadcasted_iota(jnp.int32, (sub, 2 * DIFF_DH), 1)
    for r0 in range(0, tq, sub):
        q = q_ref[0, r0:r0 + sub, :]
        outs = []
        for m in range(2):
            qm = jnp.where((lane >= DIFF_DH) if m else (lane < DIFF_DH), q, jnp.zeros_like(q))
            s = _dot_nt(qm, k_scr[...])
            p = jnp.exp2(s - jnp.max(s, axis=-1, keepdims=True)).astype(BF16)
            ne = _dot(p, v_scr[...])
            outs.append(ne[:, :DIFF_DV] / ne[:, DIFF_DV:])
        od = outs[0] - lam * outs[1]
        od = od * lax.rsqrt(jnp.mean(od * od, axis=-1, keepdims=True) + NORM_EPS)
        o_ref[0, r0:r0 + sub, :] = (od * hn_ref[...] * (1.0 - lam_init)).astype(o_ref.dtype)


def _diff_mix(p, pc, lam, head_norm, lam_init, need_ctx, tq):
    b, t, _ = p.shape
    nc = pc.shape[1]
    w = DIFF_DV
    kblk, vblk = DIFF_HEADS, 2 * DIFF_HEADS
    out_w = DIFF_HEADS * DIFF_DV

    def call(q_arr, n_q, tile, with_lat, name):
        n_keys = nc + (t if with_lat else 0)
        in_specs = [pl.BlockSpec((1, tile, w), lambda bi, h, i: (bi, i, h)),
                    pl.BlockSpec((1, nc, w), lambda bi, h, i: (bi, 0, kblk + h)),
                    pl.BlockSpec((1, nc, w), lambda bi, h, i: (bi, 0, vblk + h))]
        args = [q_arr, pc, pc]
        if with_lat:
            in_specs += [pl.BlockSpec((1, t, w), lambda bi, h, i: (bi, 0, kblk + h)),
                         pl.BlockSpec((1, t, w), lambda bi, h, i: (bi, 0, vblk + h))]
            args += [p, p]
        in_specs += [pl.BlockSpec((4, DIFF_DH), lambda bi, h, i: (0, 0)),
                     pl.BlockSpec((1, w), lambda bi, h, i: (0, h))]
        args += [lam, head_norm]
        return pl.pallas_call(
            functools.partial(_diff_kernel, lam_init=lam_init, with_lat=with_lat, sub_rows=128),
            grid=(b, DIFF_HEADS, n_q // tile),
            in_specs=in_specs,
            out_specs=pl.BlockSpec((1, tile, w), lambda bi, h, i: (bi, i, h)),
            out_shape=jax.ShapeDtypeStruct((b, n_q, out_w), BF16),
            scratch_shapes=[pltpu.VMEM((n_keys, w), BF16), pltpu.VMEM((n_keys, 2 * w), BF16)],
            compiler_params=_params("parallel", "parallel", "arbitrary"),
            name=name,
        )(*args)

    y = call(p, t, tq, True, "diff_mix")
    yc = call(pc, nc, nc, False, "diff_ctx_mix") if need_ctx else None
    return y, yc


def _post_kernel(*refs, ctx_row, sub_rows, ff_chunk, final):
    if final:
        h_ref, y_ref, g2_ref, sh_ref, sc_ref, g5_ref, gain_ref, wo_ref, w1_ref, w2_ref, fn_ref, o_ref = refs
    else:
        h_ref, y_ref, g2_ref, sh_ref, sc_ref, g5_ref, gain_ref, wo_ref, w1_ref, w2_ref, o_ref = refs
    row = pl.program_id(0) if ctx_row is None else ctx_row
    g2, g5 = _mod_row(g2_ref, row), _mod_row(g5_ref, row)
    shift, scale = _mod_row(sh_ref, row), _mod_row(sc_ref, row)
    tm = h_ref.shape[1]
    ff = w1_ref.shape[1]
    for r0 in range(0, tm, sub_rows):
        rs = slice(r0, r0 + sub_rows)
        h1 = h_ref[0, rs, :] + g2 * _dot(y_ref[0, rs, :], wo_ref[...])
        u = _norm_mod(h1, gain_ref[...], shift, scale).astype(BF16)
        acc = None
        for c0 in range(0, ff, ff_chunk):
            hidden = jnp.square(jnp.maximum(_dot(u, w1_ref[:, c0:c0 + ff_chunk]), 0.0)).astype(BF16)
            part = _dot(hidden, w2_ref[c0:c0 + ff_chunk, :])
            acc = part if acc is None else acc + part
        out = h1 + g5 * acc
        if final:
            out = out * lax.rsqrt(jnp.mean(out * out, axis=-1, keepdims=True) + NORM_EPS) * fn_ref[...]
        o_ref[0, rs, :] = out


def _post(h, y, mods, layer, gain, wo, w1, w2, *, ctx_row=None, tm, final_gain=None):
    b, t, d = h.shape
    dy = y.shape[2]
    ff = w1.shape[1]
    r = mods.shape[1]
    assert t % tm == 0
    final = final_gain is not None

    def mod_spec(k):
        return pl.BlockSpec((1, r, d), lambda bi, i: (layer, 0, k))

    def resident(shape):
        return pl.BlockSpec(shape, lambda bi, i: (0, 0), pipeline_mode=pl.Buffered(1))

    in_specs = [pl.BlockSpec((1, tm, d), lambda bi, i: (bi, i, 0)),
                pl.BlockSpec((1, tm, dy), lambda bi, i: (bi, i, 0)),
                mod_spec(2), mod_spec(3), mod_spec(4), mod_spec(5),
                pl.BlockSpec((1, d), lambda bi, i: (0, 0)),
                resident((dy, d)), resident((d, ff)), resident((ff, d))]
    args = [h, y, mods, mods, mods, mods, gain.reshape(1, d), wo, w1, w2]
    if final:
        in_specs.append(pl.BlockSpec((1, d), lambda bi, i: (0, 0)))
        args.append(final_gain.reshape(1, d))
    return pl.pallas_call(
        functools.partial(_post_kernel, ctx_row=ctx_row, sub_rows=min(tm, 512), ff_chunk=1024, final=final),
        grid=(b, t // tm),
        in_specs=in_specs,
        out_specs=pl.BlockSpec((1, tm, d), lambda bi, i: (bi, i, 0)),
        out_shape=jax.ShapeDtypeStruct((b, t, d), F32),
        compiler_params=_params("parallel", "parallel"),
        name="post",
    )(*args)


def _rope_tables(n_tok, head_dim):
    rows = n_tok // GRID_W
    row = jnp.repeat(jnp.arange(rows, dtype=jnp.int32), GRID_W).astype(F32)
    col = jnp.tile(jnp.arange(GRID_W, dtype=jnp.int32), rows).astype(F32)
    quarter = head_dim // 4
    inv = ROPE_BASE ** (-jnp.arange(quarter, dtype=F32) / quarter)
    ang = jnp.concatenate([row[:, None] * inv, col[:, None] * inv], axis=-1)
    cos, sin = jnp.cos(ang), jnp.sin(ang)
    return jnp.tile(cos, (1, 4)), jnp.tile(jnp.concatenate([-sin, sin], axis=-1), (1, 2))


def _mlstm_weights(w_in, gate_b):
    d = w_in.shape[0]
    nk = A_HEADS * A_DQK
    main = 2 * nk + 2 * A_HEADS * A_DV
    w = jnp.concatenate([w_in[:, :nk], w_in[:, 2 * nk:main]], axis=1).astype(BF16)
    wg = jnp.transpose(w_in[:, main:].reshape(d, 4, A_PAIRS, 2), (0, 2, 1, 3)).reshape(d, A_GATES)
    wt = jnp.concatenate([w_in[:, nk:2 * nk], wg], axis=1).T.astype(BF16)
    gb = jnp.transpose(gate_b.astype(F32).reshape(4, A_PAIRS, 2), (1, 0, 2)).reshape(A_GATES, 1)
    return w, wt, jnp.broadcast_to(gb, (A_GATES, LANES))


def kernel(x, c, ctx, c_ctx, ada_w, ada_b, norm_mix, norm_ffn, ffn_w1, ffn_w2, mlstm_w_in, mlstm_gate_b, mlstm_head_norm, mlstm_w_out, swa_w_in, swa_sink, swa_w_out, diff_w_in, diff_lambda_q1, diff_lambda_k1, diff_lambda_q2, diff_lambda_k2, diff_head_norm, diff_w_out, final_norm):
    bsz, n_tok, d = x.shape
    n_ctx = ctx.shape[1]
    depth = ada_w.shape[0]
    rows = -(-(bsz + 1) // SUBLANES) * SUBLANES
    cond = jnp.concatenate([c, c_ctx[None, :], jnp.zeros((rows - bsz - 1, d), F32)], axis=0)
    mods = _ada_table(cond, ada_w, ada_b)
    rope = _rope_tables(n_tok, SWA_DH)
    tm_lat, tm_post = 512, 512

    h, hc = x, ctx
    for i in range(depth):
        kind, slot = i % N_MIXERS, i // N_MIXERS
        need_ctx = i < depth - 1
        if kind == 0:
            w, wt, gb = _mlstm_weights(mlstm_w_in[slot], mlstm_gate_b[slot])
            proj = functools.partial(_project_mlstm, mods=mods, layer=i, gain=norm_mix[i], w=w, wt=wt, gb=gb)
            lat = proj(h, tm=tm_lat)
            cx = proj(hc, tm=n_ctx, ctx_row=bsz)
            y, yc = _mlstm_mix(*lat, *cx, mlstm_head_norm[slot].reshape(1, -1), need_ctx)
            wo = mlstm_w_out[slot]
        else:
            proj = functools.partial(_project, mods=mods, layer=i, gain=norm_mix[i])
            if kind == 1:
                w = swa_w_in[slot].astype(BF16)
                rc = (SWA_HEADS + SWA_KV_HEADS) * SWA_DH
                qs = (SWA_HEADS * SWA_DH, SWA_DH ** -0.5 * LOG2E)
            else:
                w = diff_w_in[slot].astype(BF16)
                rc = 4 * DIFF_HEADS * DIFF_DH
                qs = (2 * DIFF_HEADS * DIFF_DH, DIFF_DH ** -0.5 * LOG2E)
            p = proj(h, w=w, tm=tm_lat, rope=rope, rope_cols=rc, qscale=qs)
            pc = proj(hc, w=w, tm=n_ctx, ctx_row=bsz, qscale=qs)
            if kind == 1:
                y, yc = _swa_mix(p, pc, swa_sink[slot], need_ctx)
                wo = swa_w_out[slot]
            else:
                lam = jnp.stack([diff_lambda_q1[slot], diff_lambda_k1[slot], diff_lambda_q2[slot], diff_lambda_k2[slot]])
                lam_init = 0.8 - 0.6 * math.exp(-0.3 * i)
                y, yc = _diff_mix(p, pc, lam.astype(F32), diff_head_norm[slot].reshape(1, -1), lam_init, need_ctx, tq=min(n_tok, 2048))
                wo = diff_w_out[slot]
        post = functools.partial(_post, mods=mods, layer=i, gain=norm_ffn[i], wo=wo.astype(BF16),
                                 w1=ffn_w1[i].astype(BF16), w2=ffn_w2[i].astype(BF16))
        h = post(h, y, tm=tm_post, final_gain=None if need_ctx else final_norm)
        if need_ctx:
            hc = post(hc, yc, tm=n_ctx, ctx_row=bsz)
    return h
```

```python
import functools
import math

import jax
import jax.numpy as jnp
from jax import lax
from jax.experimental import pallas as pl
from jax.experimental.pallas import tpu as pltpu

F32 = jnp.float32
BF16 = jnp.bfloat16

LANES = 128
SUBLANES = 8
VMEM_LIMIT_BYTES = 56 * 1024 * 1024
LOG2E = math.log2(math.e)

NORM_EPS = 1e-6
ROPE_BASE = 10000.0
GRID_W = 64
N_MIXERS = 3

A_HEADS = 8
A_DQK = 64
A_DV = 128
A_CHUNK = 128
A_PAIRS = A_HEADS // 2
A_GATES = 4 * A_HEADS

SWA_HEADS = 16
SWA_KV_HEADS = 4
SWA_DH = 64
SWA_GROUP = SWA_HEADS // SWA_KV_HEADS
SWA_BLOCK = 128

DIFF_HEADS = 8
DIFF_DH = 64
DIFF_DV = 128


def _params(*sem):
    return pltpu.CompilerParams(dimension_semantics=sem, vmem_limit_bytes=VMEM_LIMIT_BYTES)


def _dot(a, b):
    return jnp.dot(a, b, preferred_element_type=F32)


def _dot_nt(a, b):
    return lax.dot_general(a, b, (((1,), (1,)), ((), ())), preferred_element_type=F32)


def _norm_mod(x, gain, shift, scale):
    y = x * lax.rsqrt(jnp.mean(x * x, axis=-1, keepdims=True) + NORM_EPS) * gain
    return y * (1.0 + scale) + shift


def _mod_row(ref, row):
    return ref[0, pl.ds(row, 1), :]


def _resident(shape):
    return pl.BlockSpec(shape, lambda *_: (0,) * len(shape), pipeline_mode=pl.Buffered(1))


def _ada_kernel(c_ref, w_ref, b_ref, o_ref):
    c = c_ref[...]
    s = (c * jax.nn.sigmoid(c)).astype(BF16)
    o_ref[0] = _dot(s, w_ref[0].astype(BF16)) + b_ref[0]


def _ada_table(cond, ada_w, ada_b):
    depth, d, n = ada_w.shape
    r = cond.shape[0]
    tn = n // 4
    return pl.pallas_call(
        _ada_kernel,
        grid=(depth, n // tn),
        in_specs=[pl.BlockSpec((r, d), lambda i, j: (0, 0)),
                  pl.BlockSpec((1, d, tn), lambda i, j: (i, 0, j)),
                  pl.BlockSpec((1, 1, tn), lambda i, j: (i, 0, j))],
        out_specs=pl.BlockSpec((1, r, tn), lambda i, j: (i, 0, j)),
        out_shape=jax.ShapeDtypeStruct((depth, r, n), F32),
        compiler_params=_params("parallel", "parallel"),
        name="ada_table",
    )(cond, ada_w, ada_b.reshape(depth, 1, n))


def _rope_block(blk, cos, sin_signed):
    lane = lax.broadcasted_iota(jnp.int32, blk.shape, 1)
    first_half = (lane & 32) == 0
    partner = jnp.where(first_half, pltpu.roll(blk, LANES - 32, 1), pltpu.roll(blk, 32, 1))
    return blk * cos + partner * sin_signed


def _proj_kernel(*refs, ctx_row, n_out, rope_cols, qscale, chunk):
    if rope_cols:
        x_ref, sh_ref, sc_ref, g_ref, w_ref, cos_ref, sin_ref, o_ref = refs
    else:
        x_ref, sh_ref, sc_ref, g_ref, w_ref, o_ref = refs
    row = pl.program_id(0) if ctx_row is None else ctx_row
    u = _norm_mod(x_ref[0], g_ref[...], _mod_row(sh_ref, row), _mod_row(sc_ref, row)).astype(BF16)
    for c0 in range(0, n_out, chunk):
        acc = _dot(u, w_ref[:, c0:c0 + chunk])
        for l0 in range(0, chunk, LANES):
            col = c0 + l0
            blk = acc[:, l0:l0 + LANES]
            if col < rope_cols:
                blk = _rope_block(blk, cos_ref[...], sin_ref[...])
            if col < qscale[0]:
                blk = blk * qscale[1]
            o_ref[0, :, col:col + LANES] = blk.astype(o_ref.dtype)


def _project(h, mods, layer, gain, w, *, ctx_row=None, tm, rope=None, rope_cols=0, qscale):
    b, t, d = h.shape
    n = w.shape[1]
    r = mods.shape[1]
    chunk = 512
    assert t % tm == 0 and n % chunk == 0
    in_specs = [pl.BlockSpec((1, tm, d), lambda bi, i: (bi, i, 0)),
                pl.BlockSpec((1, r, d), lambda bi, i: (layer, 0, 0)),
                pl.BlockSpec((1, r, d), lambda bi, i: (layer, 0, 1)),
                pl.BlockSpec((1, d), lambda bi, i: (0, 0)),
                _resident((d, n))]
    args = [h, mods, mods, gain.reshape(1, d), w]
    if rope_cols:
        in_specs += [pl.BlockSpec((tm, LANES), lambda bi, i: (i, 0))] * 2
        args += list(rope)
    return pl.pallas_call(
        functools.partial(_proj_kernel, ctx_row=ctx_row, n_out=n, rope_cols=rope_cols, qscale=qscale, chunk=chunk),
        grid=(b, t // tm),
        in_specs=in_specs,
        out_specs=pl.BlockSpec((1, tm, n), lambda bi, i: (bi, i, 0)),
        out_shape=jax.ShapeDtypeStruct((b, t, n), BF16),
        compiler_params=_params("parallel", "parallel"),
        name="project",
    )(*args)


def _log_sigmoid(x):
    return jnp.minimum(x, 0.0) - jnp.log1p(jnp.exp(-jnp.abs(x)))


def _lane_scan(x, op, fill, reverse):
    lane = lax.broadcasted_iota(jnp.int32, x.shape, 1)
    k = 1
    while k < LANES:
        if reverse:
            shifted, ok = pltpu.roll(x, LANES - k, 1), lane < LANES - k
        else:
            shifted, ok = pltpu.roll(x, k, 1), lane >= k
        x = op(x, jnp.where(ok, shifted, fill))
        k *= 2
    return x


def _proj_mlstm_kernel(x_ref, sh_ref, sc_ref, g_ref, w_ref, wt_ref, gb_ref, o_ref, kt_ref, gt_ref, ct_ref, ws_ref,
                       *, ctx_row, n_out, chunk):
    row = pl.program_id(0) if ctx_row is None else ctx_row
    u = _norm_mod(x_ref[0], g_ref[...], _mod_row(sh_ref, row), _mod_row(sc_ref, row)).astype(BF16)
    ut = _dot_nt(wt_ref[...], u)
    for c0 in range(0, n_out, chunk):
        o_ref[0, :, c0:c0 + chunk] = _dot(u, w_ref[:, c0:c0 + chunk])
    nk = A_HEADS * A_DQK
    row8 = lax.broadcasted_iota(jnp.int32, (A_GATES, LANES), 0) & 7
    fwd = row8 < 4
    is_cum = (row8 & 2) != 0
    for ci in range(u.shape[0] // A_CHUNK):
        cols = slice(ci * A_CHUNK, (ci + 1) * A_CHUNK)
        kt_ref[0, ci] = ut[:nk, cols] * (A_DQK ** -0.5)
        x = ut[nk:, cols] + gb_ref[...]
        lf = _log_sigmoid(x)
        cum = jnp.where(fwd, _lane_scan(lf, jnp.add, 0.0, False), _lane_scan(lf, jnp.add, 0.0, True))
        gt_ref[0, ci] = jnp.where(is_cum, cum, x)
        cum_up = pltpu.roll(cum, A_GATES - 2, 0)
        r = x - cum_up
        cmax = jnp.where(fwd, _lane_scan(r, jnp.maximum, -jnp.inf, False), _lane_scan(r, jnp.maximum, -jnp.inf, True))
        b_last = jnp.where(fwd, jnp.broadcast_to(cum_up[:, A_CHUNK - 1:A_CHUNK], cum_up.shape),
                           jnp.broadcast_to(cum_up[:, 0:1], cum_up.shape))
        a = (b_last - cum_up) + x
        g = jnp.broadcast_to(jnp.max(a, axis=-1, keepdims=True), a.shape)
        ct_ref[0, ci] = jnp.where(is_cum, pltpu.roll(b_last, 2, 0), cmax)
        ws_ref[0, ci] = jnp.where(is_cum, pltpu.roll(g, 2, 0), jnp.exp(a - g))


def _project_mlstm(h, mods, layer, gain, w, wt, gb, *, ctx_row=None, tm):
    b, t, d = h.shape
    n = w.shape[1]
    r = mods.shape[1]
    nt = wt.shape[0]
    nk = A_HEADS * A_DQK
    chunk = 512
    cpt = tm // A_CHUNK
    assert t % tm == 0 and n % chunk == 0 and tm % A_CHUNK == 0
    return pl.pallas_call(
        functools.partial(_proj_mlstm_kernel, ctx_row=ctx_row, n_out=n, chunk=chunk),
        grid=(b, t // tm),
        in_specs=[pl.BlockSpec((1, tm, d), lambda bi, i: (bi, i, 0)),
                  pl.BlockSpec((1, r, d), lambda bi, i: (layer, 0, 0)),
                  pl.BlockSpec((1, r, d), lambda bi, i: (layer, 0, 1)),
                  pl.BlockSpec((1, d), lambda bi, i: (0, 0)),
                  _resident((d, n)), _resident((nt, d)),
                  pl.BlockSpec((A_GATES, LANES), lambda bi, i: (0, 0))],
        out_specs=[pl.BlockSpec((1, tm, n), lambda bi, i: (bi, i, 0)),
                   pl.BlockSpec((1, cpt, nk, A_CHUNK), lambda bi, i: (bi, i, 0, 0)),
                   ] + [pl.BlockSpec((1, cpt, A_GATES, A_CHUNK), lambda bi, i: (bi, i, 0, 0))] * 3,
        out_shape=[jax.ShapeDtypeStruct((b, t, n), F32),
                   jax.ShapeDtypeStruct((b, t // A_CHUNK, nk, A_CHUNK), F32),
                   ] + [jax.ShapeDtypeStruct((b, t // A_CHUNK, A_GATES, A_CHUNK), F32)] * 3,
        compiler_params=_params("parallel", "parallel"),
        name="project_mlstm",
    )(h, mods, mods, gain.reshape(1, d), w, wt, gb)


def _mlstm_kernel(*refs, n_ctx, n_lat, need_ctx):
    if need_ctx:
        (ql, vl, ol, ktl, gtl, ctl, wsl, qc, vc, oc, ktc, gtc, ctc, wsc, hn_ref, yl_ref, yc_ref,
         kv_scr, cbd_scr, cst_scr, g_scr, bl_scr, m0_scr, hs_scr) = refs
    else:
        (ql, vl, ol, ktl, gtl, ctl, wsl, qc, vc, ktc, gtc, ctc, wsc, hn_ref, yl_ref,
         kv_scr, cbd_scr, cst_scr, g_scr, bl_scr, m0_scr, hs_scr) = refs
        oc = yc_ref = None
    L = A_CHUNK
    n_all = n_ctx + n_lat
    ones_v = jnp.ones((L, A_DV), BF16)

    def v_ext(v2, hh):
        return jnp.concatenate([v2[:, hh * A_DV:(hh + 1) * A_DV].astype(BF16), ones_v], axis=1)

    def rows(gt, dr, hh):
        return gt[dr * 4 + hh:dr * 4 + hh + 1, :], gt[dr * 4 + 2 + hh:dr * 4 + 3 + hh, :]

    def contrib(c, ct, ws, kt, v2):
        for hh in range(2):
            vx = v_ext(v2, hh)
            kth = kt[hh * A_DQK:(hh + 1) * A_DQK, :]
            for dr in range(2):
                w, g = rows(ws, dr, hh)
                idx = c * 4 + dr * 2 + hh
                kv_scr[idx] = _dot((kth * w).astype(BF16), vx)
                g_scr[pl.ds(idx, 1), :] = g
                bl_scr[pl.ds(idx, 1), :] = rows(ct, dr, hh)[1]

    for c in range(n_ctx):
        contrib(c, ctc[0, c], wsc[0, c], ktc[0, c], vc[0, c * L:(c + 1) * L, :])

    def contrib_lat(c, carry):
        contrib(c + n_ctx, ctl[0, c], wsl[0, c], ktl[0, c], vl[0, pl.ds(pl.multiple_of(c * L, L), L), :])
        return carry

    lax.fori_loop(0, n_lat, contrib_lat, 0, unroll=4)

    cst_scr[...] = jnp.zeros_like(cst_scr)
    zpad = jnp.zeros((A_DQK, 2 * A_DV), BF16)

    def scan_step(i, ms):
        c_bwd = jnp.where(i < n_ctx, n_ctx - 1 - i, n_all - 1 - (i - n_ctx))
        new_ms = []
        for dr, c in ((0, i), (1, c_bwd)):
            c0s = [cst_scr[dr * 2 + hh] for hh in range(2)]
            for hh in range(2):
                idx = c * 4 + dr * 2 + hh
                c0b = c0s[hh].astype(BF16)
                cbd_scr[idx] = jnp.concatenate([zpad, c0b] if hh else [c0b, zpad], axis=0)
                m0 = ms[dr * 2 + hh]
                m0_scr[pl.ds(idx, 1), :] = m0
                g, b_last = g_scr[pl.ds(idx, 1), :], bl_scr[pl.ds(idx, 1), :]
                m_new = jnp.maximum(b_last + m0, g)
                decay = jnp.exp(b_last + m0 - m_new)
                inject = jnp.exp(g - m_new)
                decay, inject = (jnp.concatenate([z, z], axis=1) for z in (decay, inject))
                cst_scr[dr * 2 + hh] = decay * c0s[hh] + inject * kv_scr[idx]
                new_ms.append(m_new)
        return tuple(new_ms)

    lax.fori_loop(0, n_all, scan_step, tuple(jnp.zeros((1, L), F32) for _ in range(4)))

    t_i = lax.broadcasted_iota(jnp.int32, (L, L), 0)
    s_i = lax.broadcasted_iota(jnp.int32, (L, L), 1)
    masks = (s_i <= t_i, s_i >= t_i)
    zk = jnp.zeros((A_DQK, L), BF16)
    zrows = jnp.zeros((L - 2 * SUBLANES, L), F32)

    def mix(c, gt, ct, kt, q2, v2):
        colm = jnp.concatenate([gt, ct, zrows], axis=0).T
        qb = q2.astype(BF16)
        ktb = kt.astype(BF16)
        kt_bd = jnp.concatenate([jnp.concatenate([ktb[:A_DQK], zk], axis=1),
                                 jnp.concatenate([zk, ktb[A_DQK:]], axis=1)], axis=0)
        s2 = _dot(qb, kt_bd)
        vxs = [v_ext(v2, hh) for hh in range(2)]
        hsum = [None, None]
        for dr in range(2):
            for hh in range(2):
                idx = c * 4 + dr * 2 + hh
                li, cum = rows(gt, dr, hh)
                m0 = m0_scr[pl.ds(idx, 1), :]
                cmax = jnp.broadcast_to(colm[:, 8 + dr * 4 + hh:9 + dr * 4 + hh], (L, L))
                cum_t = jnp.broadcast_to(colm[:, dr * 4 + 2 + hh:dr * 4 + 3 + hh], (L, L))
                mm = jnp.maximum(cmax, m0)
                p = jnp.where(masks[dr], jnp.exp((li - cum) - mm), 0.0)
                wq = (p * s2[:, hh * L:(hh + 1) * L]).astype(BF16)
                carry = jnp.exp(m0 - mm)
                lhs = jnp.concatenate([wq, (q2 * carry).astype(BF16)], axis=1)
                ne = _dot(lhs, jnp.concatenate([vxs[hh], cbd_scr[idx]], axis=0))
                h = ne[:, :A_DV] / jnp.maximum(jnp.abs(ne[:, A_DV:]), jnp.exp(-(cum_t + mm)))
                hsum[hh] = h if dr == 0 else hsum[hh] + h
        return hsum

    def finish(hs2, o2):
        ys = []
        for hh in range(2):
            hs = hs2[:, hh * A_DV:(hh + 1) * A_DV]
            hn = hs * lax.rsqrt(jnp.mean(hs * hs, axis=-1, keepdims=True) + NORM_EPS)
            hn = hn * hn_ref[:, hh * A_DV:(hh + 1) * A_DV]
            ys.append(hn * jax.nn.sigmoid(o2[:, hh * A_DV:(hh + 1) * A_DV]))
        return jnp.concatenate(ys, axis=1).astype(BF16)

    if need_ctx:
        for c in range(n_ctx):
            sl = slice(c * L, (c + 1) * L)
            hs = mix(c, gtc[0, c], ctc[0, c], ktc[0, c], qc[0, sl, :], vc[0, sl, :])
            yc_ref[0, sl, :] = finish(jnp.concatenate(hs, axis=1), oc[0, sl, :])

    def mix_lat(c):
        src = pl.ds(c * L if isinstance(c, int) else pl.multiple_of(c * L, L), L)
        hs = mix(c + n_ctx, gtl[0, c], ctl[0, c], ktl[0, c], ql[0, src, :], vl[0, src, :])
        hs_scr[...] = jnp.concatenate(hs, axis=1)

    def finish_lat(c):
        src = pl.ds(c * L if isinstance(c, int) else pl.multiple_of(c * L, L), L)
        yl_ref[0, src, :] = finish(hs_scr[...], ol[0, src, :])

    def pipelined(c, carry):
        finish_lat(c - 1)
        mix_lat(c)
        return carry

    mix_lat(0)
    lax.fori_loop(1, n_lat, pipelined, 0)
    finish_lat(n_lat - 1)


def _mlstm_mix(p, kt, gt, ct, ws, pc, ktc, gtc, ctc, wsc, head_norm, need_ctx):
    b, t, _ = p.shape
    nc = pc.shape[1]
    L = A_CHUNK
    n_lat, n_ctx = t // L, nc // L
    n_all = n_lat + n_ctx
    qw, vw = 2 * A_DQK, 2 * A_DV
    v_blk = A_HEADS * A_DQK // vw
    o_blk = v_blk + A_PAIRS

    def specs(rows, nch, with_o):
        s = [pl.BlockSpec((1, rows, qw), lambda bi, hp: (bi, 0, hp)),
             pl.BlockSpec((1, rows, vw), lambda bi, hp: (bi, 0, v_blk + hp))]
        if with_o:
            s.append(pl.BlockSpec((1, rows, vw), lambda bi, hp: (bi, 0, o_blk + hp)))
        s.append(pl.BlockSpec((1, nch, qw, L), lambda bi, hp: (bi, 0, hp, 0)))
        s += [pl.BlockSpec((1, nch, SUBLANES, L), lambda bi, hp: (bi, 0, hp, 0))] * 3
        return s

    in_specs = specs(t, n_lat, True) + specs(nc, n_ctx, need_ctx)
    in_specs.append(pl.BlockSpec((1, vw), lambda bi, hp: (0, hp)))
    args = [p, p, p, kt, gt, ct, ws] + ([pc, pc, pc] if need_ctx else [pc, pc]) + [ktc, gtc, ctc, wsc, head_norm]
    out_specs = [pl.BlockSpec((1, t, vw), lambda bi, hp: (bi, 0, hp))]
    out_shape = [jax.ShapeDtypeStruct((b, t, A_HEADS * A_DV), BF16)]
    if need_ctx:
        out_specs.append(pl.BlockSpec((1, nc, vw), lambda bi, hp: (bi, 0, hp)))
        out_shape.append(jax.ShapeDtypeStruct((b, nc, A_HEADS * A_DV), BF16))
    n_rows = -(-n_all * 4 // SUBLANES) * SUBLANES
    outs = pl.pallas_call(
        functools.partial(_mlstm_kernel, n_ctx=n_ctx, n_lat=n_lat, need_ctx=need_ctx),
        grid=(b, A_PAIRS),
        in_specs=in_specs,
        out_specs=out_specs,
        out_shape=out_shape,
        scratch_shapes=[pltpu.VMEM((n_all * 4, A_DQK, vw), F32),
                        pltpu.VMEM((n_all * 4, qw, vw), BF16),
                        pltpu.VMEM((4, A_DQK, vw), F32),
                        pltpu.VMEM((n_rows, L), F32), pltpu.VMEM((n_rows, L), F32), pltpu.VMEM((n_rows, L), F32),
                        pltpu.VMEM((L, vw), F32)],
        compiler_params=_params("parallel", "parallel"),
        name="mlstm_mix",
    )(*args)
    return (outs[0], outs[1]) if need_ctx else (outs[0], None)


def _swa_attend(q4, k, v, valid, sink_col):
    s = _dot_nt(q4, k)
    if valid is not None:
        s = jnp.where(valid, s, -jnp.inf)
    m = jnp.maximum(jnp.max(s, axis=-1, keepdims=True), sink_col)
    p = jnp.exp2(s - m)
    den = jnp.sum(p, axis=-1, keepdims=True) + jnp.exp2(sink_col - m)
    return _dot(p.astype(BF16), v) / den


def _swa_heads(q, keys, vals, valid, sink_ref, o_ref, rows):
    row_i = lax.broadcasted_iota(jnp.int32, (SWA_GROUP * rows, 1), 0)
    for hk in range(SWA_KV_HEADS):
        ks = slice(hk * SWA_DH, (hk + 1) * SWA_DH)
        k = jnp.concatenate([x[:, ks] for x in keys], axis=0)
        v = jnp.concatenate([x[:, ks] for x in vals], axis=0)
        q4 = jnp.concatenate([q[:, (hk * SWA_GROUP + g) * SWA_DH:(hk * SWA_GROUP + g + 1) * SWA_DH]
                              for g in range(SWA_GROUP)], axis=0)
        sink_col = jnp.zeros((SWA_GROUP * rows, 1), F32)
        for g in range(SWA_GROUP):
            head = hk * SWA_GROUP + g
            sink_col = jnp.where((row_i >= g * rows) & (row_i < (g + 1) * rows),
                                 sink_ref[:, head:head + 1] * LOG2E, sink_col)
        o4 = _swa_attend(q4, k, v, valid, sink_col)
        for g in range(0, SWA_GROUP, 2):
            col = (hk * SWA_GROUP + g) * SWA_DH
            pair = jnp.concatenate([o4[g * rows:(g + 1) * rows], o4[(g + 1) * rows:(g + 2) * rows]], axis=1)
            o_ref[0, :, col:col + 2 * SWA_DH] = pair.astype(o_ref.dtype)


def _swa_lat_kernel(*refs, n_tok, nq):
    nkb = nq + 2
    q_ref, k_refs, v_refs = refs[0], refs[1:1 + nkb], refs[1 + nkb:1 + 2 * nkb]
    kx_ref, vx_ref, sink_ref, o_ref = refs[1 + 2 * nkb:]
    j = pl.program_id(1)
    L = SWA_BLOCK
    nc = kx_ref.shape[1]
    qi = lax.broadcasted_iota(jnp.int32, (nq * L, nkb * L), 0)
    ki = lax.broadcasted_iota(jnp.int32, (nq * L, nkb * L), 1)
    rel = ki - L - qi
    k_lo = (1 - j * nq) * L
    valid = (jnp.abs(rel) <= L) & (ki >= k_lo) & (ki < k_lo + n_tok)
    valid = jnp.concatenate([valid, jnp.ones((nq * L, nc), jnp.bool_)], axis=1)
    valid = jnp.concatenate([valid] * SWA_GROUP, axis=0)
    _swa_heads(q_ref[0], [r[0] for r in k_refs] + [kx_ref[0]], [r[0] for r in v_refs] + [vx_ref[0]],
               valid, sink_ref, o_ref, nq * L)


def _swa_ctx_kernel(q_ref, kx_ref, vx_ref, sink_ref, o_ref):
    _swa_heads(q_ref[0], [kx_ref[0]], [vx_ref[0]], None, sink_ref, o_ref, q_ref.shape[1])


def _swa_mix(p, pc, sink, need_ctx):
    b, t, _ = p.shape
    nc = pc.shape[1]
    L = SWA_BLOCK
    nblk = t // L
    nq = 1
    qw = SWA_HEADS * SWA_DH
    kvw = SWA_KV_HEADS * SWA_DH
    kblk, vblk = qw // kvw, qw // kvw + 1
    sink2 = sink.reshape(1, SWA_HEADS)

    def kv_spec(col, shift):
        return pl.BlockSpec((1, L, kvw), lambda bi, j: (bi, jnp.clip(j * nq + shift, 0, nblk - 1), col))

    shifts = range(-1, nq + 1)
    y = pl.pallas_call(
        functools.partial(_swa_lat_kernel, n_tok=t, nq=nq),
        grid=(b, nblk // nq),
        in_specs=[pl.BlockSpec((1, nq * L, qw), lambda bi, j: (bi, j, 0))]
        + [kv_spec(kblk, s) for s in shifts] + [kv_spec(vblk, s) for s in shifts]
        + [pl.BlockSpec((1, nc, kvw), lambda bi, j: (bi, 0, kblk)),
           pl.BlockSpec((1, nc, kvw), lambda bi, j: (bi, 0, vblk)),
           pl.BlockSpec((1, SWA_HEADS), lambda bi, j: (0, 0))],
        out_specs=pl.BlockSpec((1, nq * L, qw), lambda bi, j: (bi, j, 0)),
        out_shape=jax.ShapeDtypeStruct((b, t, qw), BF16),
        compiler_params=_params("parallel", "parallel"),
        name="swa_mix",
    )(*([p] * (1 + 2 * (nq + 2)) + [pc, pc, sink2]))
    if not need_ctx:
        return y, None
    yc = pl.pallas_call(
        _swa_ctx_kernel,
        grid=(b,),
        in_specs=[pl.BlockSpec((1, nc, qw), lambda bi: (bi, 0, 0)),
                  pl.BlockSpec((1, nc, kvw), lambda bi: (bi, 0, kblk)),
                  pl.BlockSpec((1, nc, kvw), lambda bi: (bi, 0, vblk)),
                  pl.BlockSpec((1, SWA_HEADS), lambda bi: (0, 0))],
        out_specs=pl.BlockSpec((1, nc, qw), lambda bi: (bi, 0, 0)),
        out_shape=jax.ShapeDtypeStruct((b, nc, qw), BF16),
        compiler_params=_params("parallel"),
        name="swa_ctx_mix",
    )(pc, pc, pc, sink2)
    return y, yc


def _diff_kernel(*refs, lam_init, with_lat, sub_rows):
    if with_lat:
        q_ref, kx_ref, vx_ref, kl_ref, vl_ref, lam_ref, hn_ref, o_ref, k_scr, v_scr = refs
    else:
        q_ref, kx_ref, vx_ref, lam_ref, hn_ref, o_ref, k_scr, v_scr = refs
    nc = kx_ref.shape[1]

    @pl.when(pl.program_id(2) == 0)
    def _():
        k_scr[0:nc, :] = kx_ref[0]
        v_scr[0:nc, 0:DIFF_DV] = vx_ref[0]
        if with_lat:
            k_scr[nc:, :] = kl_ref[0]
            v_scr[nc:, 0:DIFF_DV] = vl_ref[0]
        v_scr[:, DIFF_DV:] = jnp.ones((v_scr.shape[0], DIFF_DV), BF16)

    lam = (jnp.exp(jnp.sum(lam_ref[0:1, :] * lam_ref[1:2, :], axis=-1, keepdims=True))
           - jnp.exp(jnp.sum(lam_ref[2:3, :] * lam_ref[3:4, :], axis=-1, keepdims=True)) + lam_init)
    tq = q_ref.shape[1]
    sub = min(tq, sub_rows)
    lane = lax.broadcasted_iota(jnp.int32, (sub, 2 * DIFF_DH), 1)
    for r0 in range(0, tq, sub):
        q = q_ref[0, r0:r0 + sub, :]
        outs = []
        for m in range(2):
            qm = jnp.where((lane >= DIFF_DH) if m else (lane < DIFF_DH), q, jnp.zeros_like(q))
            s = _dot_nt(qm, k_scr[...])
            p = jnp.exp2(s - jnp.max(s, axis=-1, keepdims=True)).astype(BF16)
            ne = _dot(p, v_scr[...])
            outs.append(ne[:, :DIFF_DV] / ne[:, DIFF_DV:])
        od = outs[0] - lam * outs[1]
        od = od * lax.rsqrt(jnp.mean(od * od, axis=-1, keepdims=True) + NORM_EPS)
        o_ref[0, r0:r0 + sub, :] = (od * hn_ref[...] * (1.0 - lam_init)).astype(o_ref.dtype)


def _diff_mix(p, pc, lam, head_norm, lam_init, need_ctx, tq):
    b, t, _ = p.shape
    nc = pc.shape[1]
    w = DIFF_DV
    kblk, vblk = DIFF_HEADS, 2 * DIFF_HEADS
    out_w = DIFF_HEADS * DIFF_DV

    def call(q_arr, n_q, tile, with_lat, name):
        n_keys = nc + (t if with_lat else 0)
        in_specs = [pl.BlockSpec((1, tile, w), lambda bi, h, i: (bi, i, h)),
                    pl.BlockSpec((1, nc, w), lambda bi, h, i: (bi, 0, kblk + h)),
                    pl.BlockSpec((1, nc, w), lambda bi, h, i: (bi, 0, vblk + h))]
        args = [q_arr, pc, pc]
        if with_lat:
            in_specs += [pl.BlockSpec((1, t, w), lambda bi, h, i: (bi, 0, kblk + h)),
                         pl.BlockSpec((1, t, w), lambda bi, h, i: (bi, 0, vblk + h))]
            args += [p, p]
        in_specs += [pl.BlockSpec((4, DIFF_DH), lambda bi, h, i: (0, 0)),
                     pl.BlockSpec((1, w), lambda bi, h, i: (0, h))]
        args += [lam, head_norm]
        return pl.pallas_call(
            functools.partial(_diff_kernel, lam_init=lam_init, with_lat=with_lat, sub_rows=128),
            grid=(b, DIFF_HEADS, n_q // tile),
            in_specs=in_specs,
            out_specs=pl.BlockSpec((1, tile, w), lambda bi, h, i: (bi, i, h)),
            out_shape=jax.ShapeDtypeStruct((b, n_q, out_w), BF16),
            scratch_shapes=[pltpu.VMEM((n_keys, w), BF16), pltpu.VMEM((n_keys, 2 * w), BF16)],
            compiler_params=_params("parallel", "parallel", "arbitrary"),
            name=name,
        )(*args)

    y = call(p, t, tq, True, "diff_mix")
    yc = call(pc, nc, nc, False, "diff_ctx_mix") if need_ctx else None
    return y, yc


def _post_kernel(*refs, ctx_row, sub_rows, ff_chunk, final):
    if final:
        h_ref, y_ref, g2_ref, sh_ref, sc_ref, g5_ref, gain_ref, wo_ref, w1_ref, w2_ref, fn_ref, o_ref = refs
    else:
        h_ref, y_ref, g2_ref, sh_ref, sc_ref, g5_ref, gain_ref, wo_ref, w1_ref, w2_ref, o_ref = refs
    row = pl.program_id(0) if ctx_row is None else ctx_row
    g2, g5 = _mod_row(g2_ref, row), _mod_row(g5_ref, row)
    shift, scale = _mod_row(sh_ref, row), _mod_row(sc_ref, row)
    tm = h_ref.shape[1]
    ff = w1_ref.shape[1]
    for r0 in range(0, tm, sub_rows):
        rs = slice(r0, r0 + sub_rows)
        h1 = h_ref[0, rs, :] + g2 * _dot(y_ref[0, rs, :], wo_ref[...])
        u = _norm_mod(h1, gain_ref[...], shift, scale).astype(BF16)
        acc = None
        for c0 in range(0, ff, ff_chunk):
            hidden = jnp.square(jnp.maximum(_dot(u, w1_ref[:, c0:c0 + ff_chunk]), 0.0)).astype(BF16)
            part = _dot(hidden, w2_ref[c0:c0 + ff_chunk, :])
            acc = part if acc is None else acc + part
        out = h1 + g5 * acc
        if final:
            out = out * lax.rsqrt(jnp.mean(out * out, axis=-1, keepdims=True) + NORM_EPS) * fn_ref[...]
        o_ref[0, rs, :] = out


def _post(h, y, mods, layer, gain, wo, w1, w2, *, ctx_row=None, tm, final_gain=None):
    b, t, d = h.shape
    dy = y.shape[2]
    ff = w1.shape[1]
    r = mods.shape[1]
    assert t % tm == 0
    final = final_gain is not None

    def mod_spec(k):
        return pl.BlockSpec((1, r, d), lambda bi, i: (layer, 0, k))

    in_specs = [pl.BlockSpec((1, tm, d), lambda bi, i: (bi, i, 0)),
                pl.BlockSpec((1, tm, dy), lambda bi, i: (bi, i, 0)),
                mod_spec(2), mod_spec(3), mod_spec(4), mod_spec(5),
                pl.BlockSpec((1, d), lambda bi, i: (0, 0)),
                _resident((dy, d)), _resident((d, ff)), _resident((ff, d))]
    args = [h, y, mods, mods, mods, mods, gain.reshape(1, d), wo, w1, w2]
    if final:
        in_specs.append(pl.BlockSpec((1, d), lambda bi, i: (0, 0)))
        args.append(final_gain.reshape(1, d))
    return pl.pallas_call(
        functools.partial(_post_kernel, ctx_row=ctx_row, sub_rows=min(tm, 512), ff_chunk=1024, final=final),
        grid=(b, t // tm),
        in_specs=in_specs,
        out_specs=pl.BlockSpec((1, tm, d), lambda bi, i: (bi, i, 0)),
        out_shape=jax.ShapeDtypeStruct((b, t, d), F32),
        compiler_params=_params("parallel", "parallel"),
        name="post",
    )(*args)


def _rope_tables(n_tok, head_dim):
    rows = n_tok // GRID_W
    row = jnp.repeat(jnp.arange(rows, dtype=jnp.int32), GRID_W).astype(F32)
    col = jnp.tile(jnp.arange(GRID_W, dtype=jnp.int32), rows).astype(F32)
    quarter = head_dim // 4
    inv = ROPE_BASE ** (-jnp.arange(quarter, dtype=F32) / quarter)
    ang = jnp.concatenate([row[:, None] * inv, col[:, None] * inv], axis=-1)
    cos, sin = jnp.cos(ang), jnp.sin(ang)
    return jnp.tile(cos, (1, 4)), jnp.tile(jnp.concatenate([-sin, sin], axis=-1), (1, 2))


def _mlstm_weights(w_in, gate_b):
    d = w_in.shape[0]
    nk = A_HEADS * A_DQK
    main = 2 * nk + 2 * A_HEADS * A_DV
    w = jnp.concatenate([w_in[:, :nk], w_in[:, 2 * nk:main]], axis=1).astype(BF16)
    wg = jnp.transpose(w_in[:, main:].reshape(d, 4, A_PAIRS, 2), (0, 2, 1, 3)).reshape(d, A_GATES)
    wt = jnp.concatenate([w_in[:, nk:2 * nk], wg], axis=1).T.astype(BF16)
    gb = jnp.transpose(gate_b.astype(F32).reshape(4, A_PAIRS, 2), (1, 0, 2)).reshape(A_GATES, 1)
    return w, wt, jnp.broadcast_to(gb, (A_GATES, LANES))


def kernel(x, c, ctx, c_ctx, ada_w, ada_b, norm_mix, norm_ffn, ffn_w1, ffn_w2, mlstm_w_in, mlstm_gate_b, mlstm_head_norm, mlstm_w_out, swa_w_in, swa_sink, swa_w_out, diff_w_in, diff_lambda_q1, diff_lambda_k1, diff_lambda_q2, diff_lambda_k2, diff_head_norm, diff_w_out, final_norm):
    bsz, n_tok, d = x.shape
    n_ctx = ctx.shape[1]
    depth = ada_w.shape[0]
    rows = -(-(bsz + 1) // SUBLANES) * SUBLANES
    cond = jnp.concatenate([c, c_ctx[None, :], jnp.zeros((rows - bsz - 1, d), F32)], axis=0)
    mods = _ada_table(cond, ada_w, ada_b)
    rope = _rope_tables(n_tok, SWA_DH)
    tm_lat, tm_post = min(n_tok, 1024), 512

    h, hc = x, ctx
    for i in range(depth):
        kind, slot = i % N_MIXERS, i // N_MIXERS
        need_ctx = i < depth - 1
        if kind == 0:
            w, wt, gb = _mlstm_weights(mlstm_w_in[slot], mlstm_gate_b[slot])
            proj = functools.partial(_project_mlstm, mods=mods, layer=i, gain=norm_mix[i], w=w, wt=wt, gb=gb)
            lat = proj(h, tm=tm_lat)
            cx = proj(hc, tm=n_ctx, ctx_row=bsz)
            y, yc = _mlstm_mix(*lat, *cx, mlstm_head_norm[slot].reshape(1, -1), need_ctx)
            wo = mlstm_w_out[slot]
        else:
            proj = functools.partial(_project, mods=mods, layer=i, gain=norm_mix[i])
            if kind == 1:
                w = swa_w_in[slot].astype(BF16)
                rc = (SWA_HEADS + SWA_KV_HEADS) * SWA_DH
                qs = (SWA_HEADS * SWA_DH, SWA_DH ** -0.5 * LOG2E)
            else:
                w = diff_w_in[slot].astype(BF16)
                rc = 4 * DIFF_HEADS * DIFF_DH
                qs = (2 * DIFF_HEADS * DIFF_DH, DIFF_DH ** -0.5 * LOG2E)
            p = proj(h, w=w, tm=tm_lat, rope=rope, rope_cols=rc, qscale=qs)
            pc = proj(hc, w=w, tm=n_ctx, ctx_row=bsz, qscale=qs)
            if kind == 1:
                y, yc = _swa_mix(p, pc, swa_sink[slot], need_ctx)
                wo = swa_w_out[slot]
            else:
                lam = jnp.stack([diff_lambda_q1[slot], diff_lambda_k1[slot], diff_lambda_q2[slot], diff_lambda_k2[slot]])
                lam_init = 0.8 - 0.6 * math.exp(-0.3 * i)
                y, yc = _diff_mix(p, pc, lam.astype(F32), diff_head_norm[slot].reshape(1, -1), lam_init, need_ctx, tq=min(n_tok, 2048))
                wo = diff_w_out[slot]
        post = functools.partial(_post, mods=mods, layer=i, gain=norm_ffn[i], wo=wo.astype(BF16),
                                 w1=ffn_w1[i].astype(BF16), w2=ffn_w2[i].astype(BF16))
        h = post(h, y, tm=tm_post, final_gain=None if need_ctx else final_norm)
        if need_ctx:
            hc = post(hc, yc, tm=n_ctx, ctx_row=bsz)
    return h
```

```python
import functools
import math

import jax
import jax.numpy as jnp
from jax import lax
from jax.experimental import pallas as pl
from jax.experimental.pallas import tpu as pltpu

F32 = jnp.float32
BF16 = jnp.bfloat16

LANES = 128
SUBLANES = 8
VMEM_LIMIT_BYTES = 56 * 1024 * 1024
LOG2E = math.log2(math.e)

NORM_EPS = 1e-6
ROPE_BASE = 10000.0
GRID_W = 64
N_MIXERS = 3

A_HEADS = 8
A_DQK = 64
A_DV = 128
A_CHUNK = 128
A_PAIRS = A_HEADS // 2
A_GATES = 4 * A_HEADS

SWA_HEADS = 16
SWA_KV_HEADS = 4
SWA_DH = 64
SWA_GROUP = SWA_HEADS // SWA_KV_HEADS
SWA_BLOCK = 128

DIFF_HEADS = 8
DIFF_DH = 64
DIFF_DV = 128


def _params(*sem):
    return pltpu.CompilerParams(dimension_semantics=sem, vmem_limit_bytes=VMEM_LIMIT_BYTES)


def _dot(a, b):
    return jnp.dot(a, b, preferred_element_type=F32)


def _dot_nt(a, b):
    return lax.dot_general(a, b, (((1,), (1,)), ((), ())), preferred_element_type=F32)


def _norm_mod(x, gain, shift, scale):
    y = x * lax.rsqrt(jnp.mean(x * x, axis=-1, keepdims=True) + NORM_EPS) * gain
    return y * (1.0 + scale) + shift


def _mod_row(ref, row):
    return ref[0, pl.ds(row, 1), :]


def _resident(shape):
    return pl.BlockSpec(shape, lambda *_: (0,) * len(shape), pipeline_mode=pl.Buffered(1))


def _ada_kernel(c_ref, w_ref, b_ref, o_ref):
    c = c_ref[...]
    s = (c * jax.nn.sigmoid(c)).astype(BF16)
    o_ref[0] = _dot(s, w_ref[0].astype(BF16)) + b_ref[0]


def _ada_table(cond, ada_w, ada_b):
    depth, d, n = ada_w.shape
    r = cond.shape[0]
    tn = n // 4
    return pl.pallas_call(
        _ada_kernel,
        grid=(depth, n // tn),
        in_specs=[pl.BlockSpec((r, d), lambda i, j: (0, 0)),
                  pl.BlockSpec((1, d, tn), lambda i, j: (i, 0, j)),
                  pl.BlockSpec((1, 1, tn), lambda i, j: (i, 0, j))],
        out_specs=pl.BlockSpec((1, r, tn), lambda i, j: (i, 0, j)),
        out_shape=jax.ShapeDtypeStruct((depth, r, n), F32),
        compiler_params=_params("parallel", "parallel"),
        name="ada_table",
    )(cond, ada_w, ada_b.reshape(depth, 1, n))


def _rope_block(blk, cos, sin_signed):
    lane = lax.broadcasted_iota(jnp.int32, blk.shape, 1)
    first_half = (lane & 32) == 0
    partner = jnp.where(first_half, pltpu.roll(blk, LANES - 32, 1), pltpu.roll(blk, 32, 1))
    return blk * cos + partner * sin_signed


def _proj_kernel(*refs, ctx_row, n_out, rope_cols, qscale, chunk):
    if rope_cols:
        x_ref, sh_ref, sc_ref, g_ref, w_ref, cos_ref, sin_ref, o_ref = refs
    else:
        x_ref, sh_ref, sc_ref, g_ref, w_ref, o_ref = refs
    row = pl.program_id(0) if ctx_row is None else ctx_row
    u = _norm_mod(x_ref[0], g_ref[...], _mod_row(sh_ref, row), _mod_row(sc_ref, row)).astype(BF16)
    for c0 in range(0, n_out, chunk):
        acc = _dot(u, w_ref[:, c0:c0 + chunk])
        for l0 in range(0, chunk, LANES):
            col = c0 + l0
            blk = acc[:, l0:l0 + LANES]
            if col < rope_cols:
                blk = _rope_block(blk, cos_ref[...], sin_ref[...])
            if col < qscale[0]:
                blk = blk * qscale[1]
            o_ref[0, :, col:col + LANES] = blk.astype(o_ref.dtype)


def _project(h, mods, layer, gain, w, *, ctx_row=None, tm, rope=None, rope_cols=0, qscale):
    b, t, d = h.shape
    n = w.shape[1]
    r = mods.shape[1]
    chunk = 512
    assert t % tm == 0 and n % chunk == 0
    in_specs = [pl.BlockSpec((1, tm, d), lambda bi, i: (bi, i, 0)),
                pl.BlockSpec((1, r, d), lambda bi, i: (layer, 0, 0)),
                pl.BlockSpec((1, r, d), lambda bi, i: (layer, 0, 1)),
                pl.BlockSpec((1, d), lambda bi, i: (0, 0)),
                _resident((d, n))]
    args = [h, mods, mods, gain.reshape(1, d), w]
    if rope_cols:
        in_specs += [pl.BlockSpec((tm, LANES), lambda bi, i: (i, 0))] * 2
        args += list(rope)
    return pl.pallas_call(
        functools.partial(_proj_kernel, ctx_row=ctx_row, n_out=n, rope_cols=rope_cols, qscale=qscale, chunk=chunk),
        grid=(b, t // tm),
        in_specs=in_specs,
        out_specs=pl.BlockSpec((1, tm, n), lambda bi, i: (bi, i, 0)),
        out_shape=jax.ShapeDtypeStruct((b, t, n), BF16),
        compiler_params=_params("parallel", "parallel"),
        name="project",
    )(*args)


def _log_sigmoid(x):
    return jnp.minimum(x, 0.0) - jnp.log1p(jnp.exp(-jnp.abs(x)))


def _lane_scan(x, op, fill, reverse):
    lane = lax.broadcasted_iota(jnp.int32, x.shape, 1)
    k = 1
    while k < LANES:
        if reverse:
            shifted, ok = pltpu.roll(x, LANES - k, 1), lane < LANES - k
        else:
            shifted, ok = pltpu.roll(x, k, 1), lane >= k
        x = op(x, jnp.where(ok, shifted, fill))
        k *= 2
    return x


def _proj_mlstm_kernel(x_ref, sh_ref, sc_ref, g_ref, w_ref, wt_ref, gb_ref, o_ref, kt_ref, gt_ref, ct_ref, ws_ref,
                       *, ctx_row, n_out, chunk):
    row = pl.program_id(0) if ctx_row is None else ctx_row
    u = _norm_mod(x_ref[0], g_ref[...], _mod_row(sh_ref, row), _mod_row(sc_ref, row)).astype(BF16)
    ut = _dot_nt(wt_ref[...], u)
    for c0 in range(0, n_out, chunk):
        o_ref[0, :, c0:c0 + chunk] = _dot(u, w_ref[:, c0:c0 + chunk])
    nk = A_HEADS * A_DQK
    row8 = lax.broadcasted_iota(jnp.int32, (A_GATES, LANES), 0) & 7
    fwd = row8 < 4
    is_cum = (row8 & 2) != 0
    for ci in range(u.shape[0] // A_CHUNK):
        cols = slice(ci * A_CHUNK, (ci + 1) * A_CHUNK)
        kt_ref[0, ci] = ut[:nk, cols] * (A_DQK ** -0.5)
        x = ut[nk:, cols] + gb_ref[...]
        lf = _log_sigmoid(x)
        cum = jnp.where(fwd, _lane_scan(lf, jnp.add, 0.0, False), _lane_scan(lf, jnp.add, 0.0, True))
        gt_ref[0, ci] = jnp.where(is_cum, cum, x)
        cum_up = pltpu.roll(cum, A_GATES - 2, 0)
        r = x - cum_up
        cmax = jnp.where(fwd, _lane_scan(r, jnp.maximum, -jnp.inf, False), _lane_scan(r, jnp.maximum, -jnp.inf, True))
        b_last = jnp.where(fwd, jnp.broadcast_to(cum_up[:, A_CHUNK - 1:A_CHUNK], cum_up.shape),
                           jnp.broadcast_to(cum_up[:, 0:1], cum_up.shape))
        a = (b_last - cum_up) + x
        g = jnp.broadcast_to(jnp.max(a, axis=-1, keepdims=True), a.shape)
        ct_ref[0, ci] = jnp.where(is_cum, pltpu.roll(b_last, 2, 0), cmax)
        ws_ref[0, ci] = jnp.where(is_cum, pltpu.roll(g, 2, 0), jnp.exp(a - g))


def _project_mlstm(h, mods, layer, gain, w, wt, gb, *, ctx_row=None, tm):
    b, t, d = h.shape
    n = w.shape[1]
    r = mods.shape[1]
    nt = wt.shape[0]
    nk = A_HEADS * A_DQK
    chunk = 512
    cpt = tm // A_CHUNK
    assert t % tm == 0 and n % chunk == 0 and tm % A_CHUNK == 0
    return pl.pallas_call(
        functools.partial(_proj_mlstm_kernel, ctx_row=ctx_row, n_out=n, chunk=chunk),
        grid=(b, t // tm),
        in_specs=[pl.BlockSpec((1, tm, d), lambda bi, i: (bi, i, 0)),
                  pl.BlockSpec((1, r, d), lambda bi, i: (layer, 0, 0)),
                  pl.BlockSpec((1, r, d), lambda bi, i: (layer, 0, 1)),
                  pl.BlockSpec((1, d), lambda bi, i: (0, 0)),
                  _resident((d, n)), _resident((nt, d)),
                  pl.BlockSpec((A_GATES, LANES), lambda bi, i: (0, 0))],
        out_specs=[pl.BlockSpec((1, tm, n), lambda bi, i: (bi, i, 0)),
                   pl.BlockSpec((1, cpt, nk, A_CHUNK), lambda bi, i: (bi, i, 0, 0)),
                   ] + [pl.BlockSpec((1, cpt, A_GATES, A_CHUNK), lambda bi, i: (bi, i, 0, 0))] * 3,
        out_shape=[jax.ShapeDtypeStruct((b, t, n), F32),
                   jax.ShapeDtypeStruct((b, t // A_CHUNK, nk, A_CHUNK), F32),
                   ] + [jax.ShapeDtypeStruct((b, t // A_CHUNK, A_GATES, A_CHUNK), F32)] * 3,
        compiler_params=_params("parallel", "parallel"),
        name="project_mlstm",
    )(h, mods, mods, gain.reshape(1, d), w, wt, gb)


def _mlstm_kernel(*refs, n_ctx, n_lat, need_ctx):
    if need_ctx:
        (ql, vl, ol, ktl, gtl, ctl, wsl, qc, vc, oc, ktc, gtc, ctc, wsc, hn_ref, yl_ref, yc_ref,
         kv_scr, cbd_scr, cst_scr, g_scr, bl_scr, m0_scr, hs_scr) = refs
    else:
        (ql, vl, ol, ktl, gtl, ctl, wsl, qc, vc, ktc, gtc, ctc, wsc, hn_ref, yl_ref,
         kv_scr, cbd_scr, cst_scr, g_scr, bl_scr, m0_scr, hs_scr) = refs
        oc = yc_ref = None
    L = A_CHUNK
    n_all = n_ctx + n_lat
    ones_v = jnp.ones((L, A_DV), BF16)

    def v_ext(v2, hh):
        return jnp.concatenate([v2[:, hh * A_DV:(hh + 1) * A_DV].astype(BF16), ones_v], axis=1)

    def rows(gt, dr, hh):
        return gt[dr * 4 + hh:dr * 4 + hh + 1, :], gt[dr * 4 + 2 + hh:dr * 4 + 3 + hh, :]

    def contrib(c, ct, ws, kt, v2):
        for hh in range(2):
            vx = v_ext(v2, hh)
            kth = kt[hh * A_DQK:(hh + 1) * A_DQK, :]
            for dr in range(2):
                w, g = rows(ws, dr, hh)
                idx = c * 4 + dr * 2 + hh
                kv_scr[idx] = _dot((kth * w).astype(BF16), vx)
                g_scr[pl.ds(idx, 1), :] = g
                bl_scr[pl.ds(idx, 1), :] = rows(ct, dr, hh)[1]

    for c in range(n_ctx):
        contrib(c, ctc[0, c], wsc[0, c], ktc[0, c], vc[0, c * L:(c + 1) * L, :])

    def contrib_lat(c, carry):
        contrib(c + n_ctx, ctl[0, c], wsl[0, c], ktl[0, c], vl[0, pl.ds(pl.multiple_of(c * L, L), L), :])
        return carry

    lax.fori_loop(0, n_lat, contrib_lat, 0, unroll=4)

    cst_scr[...] = jnp.zeros_like(cst_scr)
    zpad = jnp.zeros((A_DQK, 2 * A_DV), BF16)

    def scan_step(i, ms):
        c_bwd = jnp.where(i < n_ctx, n_ctx - 1 - i, n_all - 1 - (i - n_ctx))
        new_ms = []
        for dr, c in ((0, i), (1, c_bwd)):
            c0s = [cst_scr[dr * 2 + hh] for hh in range(2)]
            for hh in range(2):
                idx = c * 4 + dr * 2 + hh
                c0b = c0s[hh].astype(BF16)
                cbd_scr[idx] = jnp.concatenate([zpad, c0b] if hh else [c0b, zpad], axis=0)
                m0 = ms[dr * 2 + hh]
                m0_scr[pl.ds(idx, 1), :] = m0
                g, b_last = g_scr[pl.ds(idx, 1), :], bl_scr[pl.ds(idx, 1), :]
                m_new = jnp.maximum(b_last + m0, g)
                decay = jnp.exp(b_last + m0 - m_new)
                inject = jnp.exp(g - m_new)
                decay, inject = (jnp.concatenate([z, z], axis=1) for z in (decay, inject))
                cst_scr[dr * 2 + hh] = decay * c0s[hh] + inject * kv_scr[idx]
                new_ms.append(m_new)
        return tuple(new_ms)

    lax.fori_loop(0, n_all, scan_step, tuple(jnp.zeros((1, L), F32) for _ in range(4)))

    t_i = lax.broadcasted_iota(jnp.int32, (L, L), 0)
    s_i = lax.broadcasted_iota(jnp.int32, (L, L), 1)
    masks = (s_i <= t_i, s_i >= t_i)
    zk = jnp.zeros((A_DQK, L), BF16)
    zrows = jnp.zeros((L - 2 * SUBLANES, L), F32)

    def mix(c, gt, ct, kt, q2, v2):
        colm = jnp.concatenate([gt, ct, zrows], axis=0).T
        qb = q2.astype(BF16)
        ktb = kt.astype(BF16)
        kt_bd = jnp.concatenate([jnp.concatenate([ktb[:A_DQK], zk], axis=1),
                                 jnp.concatenate([zk, ktb[A_DQK:]], axis=1)], axis=0)
        s2 = _dot(qb, kt_bd)
        vxs = [v_ext(v2, hh) for hh in range(2)]
        hsum = [None, None]
        for dr in range(2):
            for hh in range(2):
                idx = c * 4 + dr * 2 + hh
                li, cum = rows(gt, dr, hh)
                m0 = m0_scr[pl.ds(idx, 1), :]
                cmax = jnp.broadcast_to(colm[:, 8 + dr * 4 + hh:9 + dr * 4 + hh], (L, L))
                cum_t = jnp.broadcast_to(colm[:, dr * 4 + 2 + hh:dr * 4 + 3 + hh], (L, L))
                mm = jnp.maximum(cmax, m0)
                p = jnp.where(masks[dr], jnp.exp((li - cum) - mm), 0.0)
                wq = (p * s2[:, hh * L:(hh + 1) * L]).astype(BF16)
                carry = jnp.exp(m0 - mm)
                lhs = jnp.concatenate([wq, (q2 * carry).astype(BF16)], axis=1)
                ne = _dot(lhs, jnp.concatenate([vxs[hh], cbd_scr[idx]], axis=0))
                h = ne[:, :A_DV] / jnp.maximum(jnp.abs(ne[:, A_DV:]), jnp.exp(-(cum_t + mm)))
                hsum[hh] = h if dr == 0 else hsum[hh] + h
        return hsum

    def finish(hs2, o2):
        ys = []
        for hh in range(2):
            hs = hs2[:, hh * A_DV:(hh + 1) * A_DV]
            hn = hs * lax.rsqrt(jnp.mean(hs * hs, axis=-1, keepdims=True) + NORM_EPS)
            hn = hn * hn_ref[:, hh * A_DV:(hh + 1) * A_DV]
            ys.append(hn * jax.nn.sigmoid(o2[:, hh * A_DV:(hh + 1) * A_DV]))
        return jnp.concatenate(ys, axis=1).astype(BF16)

    if need_ctx:
        for c in range(n_ctx):
            sl = slice(c * L, (c + 1) * L)
            hs = mix(c, gtc[0, c], ctc[0, c], ktc[0, c], qc[0, sl, :], vc[0, sl, :])
            yc_ref[0, sl, :] = finish(jnp.concatenate(hs, axis=1), oc[0, sl, :])

    def mix_lat(c):
        src = pl.ds(c * L if isinstance(c, int) else pl.multiple_of(c * L, L), L)
        hs = mix(c + n_ctx, gtl[0, c], ctl[0, c], ktl[0, c], ql[0, src, :], vl[0, src, :])
        hs_scr[...] = jnp.concatenate(hs, axis=1)

    def finish_lat(c):
        src = pl.ds(c * L if isinstance(c, int) else pl.multiple_of(c * L, L), L)
        yl_ref[0, src, :] = finish(hs_scr[...], ol[0, src, :])

    def pipelined(c, carry):
        finish_lat(c - 1)
        mix_lat(c)
        return carry

    mix_lat(0)
    lax.fori_loop(1, n_lat, pipelined, 0)
    finish_lat(n_lat - 1)


def _mlstm_mix(p, kt, gt, ct, ws, pc, ktc, gtc, ctc, wsc, head_norm, need_ctx):
    b, t, _ = p.shape
    nc = pc.shape[1]
    L = A_CHUNK
    n_lat, n_ctx = t // L, nc // L
    n_all = n_lat + n_ctx
    qw, vw = 2 * A_DQK, 2 * A_DV
    v_blk = A_HEADS * A_DQK // vw
    o_blk = v_blk + A_PAIRS

    def specs(rows, nch, with_o):
        s = [pl.BlockSpec((1, rows, qw), lambda bi, hp: (bi, 0, hp)),
             pl.BlockSpec((1, rows, vw), lambda bi, hp: (bi, 0, v_blk + hp))]
        if with_o:
            s.append(pl.BlockSpec((1, rows, vw), lambda bi, hp: (bi, 0, o_blk + hp)))
        s.append(pl.BlockSpec((1, nch, qw, L), lambda bi, hp: (bi, 0, hp, 0)))
        s += [pl.BlockSpec((1, nch, SUBLANES, L), lambda bi, hp: (bi, 0, hp, 0))] * 3
        return s

    in_specs = specs(t, n_lat, True) + specs(nc, n_ctx, need_ctx)
    in_specs.append(pl.BlockSpec((1, vw), lambda bi, hp: (0, hp)))
    args = [p, p, p, kt, gt, ct, ws] + ([pc, pc, pc] if need_ctx else [pc, pc]) + [ktc, gtc, ctc, wsc, head_norm]
    out_specs = [pl.BlockSpec((1, t, vw), lambda bi, hp: (bi, 0, hp))]
    out_shape = [jax.ShapeDtypeStruct((b, t, A_HEADS * A_DV), BF16)]
    if need_ctx:
        out_specs.append(pl.BlockSpec((1, nc, vw), lambda bi, hp: (bi, 0, hp)))
        out_shape.append(jax.ShapeDtypeStruct((b, nc, A_HEADS * A_DV), BF16))
    n_rows = -(-n_all * 4 // SUBLANES) * SUBLANES
    outs = pl.pallas_call(
        functools.partial(_mlstm_kernel, n_ctx=n_ctx, n_lat=n_lat, need_ctx=need_ctx),
        grid=(b, A_PAIRS),
        in_specs=in_specs,
        out_specs=out_specs,
        out_shape=out_shape,
        scratch_shapes=[pltpu.VMEM((n_all * 4, A_DQK, vw), F32),
                        pltpu.VMEM((n_all * 4, qw, vw), BF16),
                        pltpu.VMEM((4, A_DQK, vw), F32),
                        pltpu.VMEM((n_rows, L), F32), pltpu.VMEM((n_rows, L), F32), pltpu.VMEM((n_rows, L), F32),
                        pltpu.VMEM((L, vw), F32)],
        compiler_params=_params("parallel", "parallel"),
        name="mlstm_mix",
    )(*args)
    return (outs[0], outs[1]) if need_ctx else (outs[0], None)


def _swa_attend(q4, k, v, valid, sink_col):
    s = _dot_nt(q4, k)
    if valid is not None:
        s = jnp.where(valid, s, -jnp.inf)
    m = jnp.maximum(jnp.max(s, axis=-1, keepdims=True), sink_col)
    p = jnp.exp2(s - m)
    den = jnp.sum(p, axis=-1, keepdims=True) + jnp.exp2(sink_col - m)
    return _dot(p.astype(BF16), v) / den


def _swa_heads(q, keys, vals, valid, sink_ref, o_ref, rows):
    row_i = lax.broadcasted_iota(jnp.int32, (SWA_GROUP * rows, 1), 0)
    for hk in range(SWA_KV_HEADS):
        ks = slice(hk * SWA_DH, (hk + 1) * SWA_DH)
        k = jnp.concatenate([x[:, ks] for x in keys], axis=0)
        v = jnp.concatenate([x[:, ks] for x in vals], axis=0)
        q4 = jnp.concatenate([q[:, (hk * SWA_GROUP + g) * SWA_DH:(hk * SWA_GROUP + g + 1) * SWA_DH]
                              for g in range(SWA_GROUP)], axis=0)
        sink_col = jnp.zeros((SWA_GROUP * rows, 1), F32)
        for g in range(SWA_GROUP):
            head = hk * SWA_GROUP + g
            sink_col = jnp.where((row_i >= g * rows) & (row_i < (g + 1) * rows),
                                 sink_ref[:, head:head + 1] * LOG2E, sink_col)
        o4 = _swa_attend(q4, k, v, valid, sink_col)
        for g in range(0, SWA_GROUP, 2):
            col = (hk * SWA_GROUP + g) * SWA_DH
            pair = jnp.concatenate([o4[g * rows:(g + 1) * rows], o4[(g + 1) * rows:(g + 2) * rows]], axis=1)
            o_ref[0, :, col:col + 2 * SWA_DH] = pair.astype(o_ref.dtype)


def _swa_kernel(*refs, n_tok, nq, need_ctx):
    nkb = nq + 2
    q_ref, k_refs, v_refs = refs[0], refs[1:1 + nkb], refs[1 + nkb:1 + 2 * nkb]
    if need_ctx:
        kx_ref, vx_ref, sink_ref, qx_ref, o_ref, ox_ref = refs[1 + 2 * nkb:]
    else:
        kx_ref, vx_ref, sink_ref, o_ref = refs[1 + 2 * nkb:]
    j = pl.program_id(1)
    if need_ctx:
        @pl.when(j == 0)
        def _():
            _swa_heads(qx_ref[0], [kx_ref[0]], [vx_ref[0]], None, sink_ref, ox_ref, qx_ref.shape[1])
    L = SWA_BLOCK
    nc = kx_ref.shape[1]
    qi = lax.broadcasted_iota(jnp.int32, (nq * L, nkb * L), 0)
    ki = lax.broadcasted_iota(jnp.int32, (nq * L, nkb * L), 1)
    rel = ki - L - qi
    k_lo = (1 - j * nq) * L
    valid = (jnp.abs(rel) <= L) & (ki >= k_lo) & (ki < k_lo + n_tok)
    valid = jnp.concatenate([valid, jnp.ones((nq * L, nc), jnp.bool_)], axis=1)
    valid = jnp.concatenate([valid] * SWA_GROUP, axis=0)
    _swa_heads(q_ref[0], [r[0] for r in k_refs] + [kx_ref[0]], [r[0] for r in v_refs] + [vx_ref[0]],
               valid, sink_ref, o_ref, nq * L)


def _swa_mix(p, pc, sink, need_ctx):
    b, t, _ = p.shape
    nc = pc.shape[1]
    L = SWA_BLOCK
    nblk = t // L
    nq = 1
    qw = SWA_HEADS * SWA_DH
    kvw = SWA_KV_HEADS * SWA_DH
    kblk, vblk = qw // kvw, qw // kvw + 1
    sink2 = sink.reshape(1, SWA_HEADS)

    def kv_spec(col, shift):
        return pl.BlockSpec((1, L, kvw), lambda bi, j: (bi, jnp.clip(j * nq + shift, 0, nblk - 1), col))

    shifts = range(-1, nq + 1)
    in_specs = ([pl.BlockSpec((1, nq * L, qw), lambda bi, j: (bi, j, 0))]
                + [kv_spec(kblk, s) for s in shifts] + [kv_spec(vblk, s) for s in shifts]
                + [pl.BlockSpec((1, nc, kvw), lambda bi, j: (bi, 0, kblk)),
                   pl.BlockSpec((1, nc, kvw), lambda bi, j: (bi, 0, vblk)),
                   pl.BlockSpec((1, SWA_HEADS), lambda bi, j: (0, 0))])
    args = [p] * (1 + 2 * (nq + 2)) + [pc, pc, sink2]
    out_specs = [pl.BlockSpec((1, nq * L, qw), lambda bi, j: (bi, j, 0))]
    out_shape = [jax.ShapeDtypeStruct((b, t, qw), BF16)]
    if need_ctx:
        in_specs.append(pl.BlockSpec((1, nc, qw), lambda bi, j: (bi, 0, 0)))
        args.append(pc)
        out_specs.append(pl.BlockSpec((1, nc, qw), lambda bi, j: (bi, 0, 0)))
        out_shape.append(jax.ShapeDtypeStruct((b, nc, qw), BF16))
    outs = pl.pallas_call(
        functools.partial(_swa_kernel, n_tok=t, nq=nq, need_ctx=need_ctx),
        grid=(b, nblk // nq),
        in_specs=in_specs,
        out_specs=out_specs,
        out_shape=out_shape,
        compiler_params=_params("parallel", "arbitrary"),
        name="swa_mix",
    )(*args)
    return (outs[0], outs[1]) if need_ctx else (outs[0], None)


def _diff_rows(q, k, v, lam, hn, lam_init):
    lane = lax.broadcasted_iota(jnp.int32, q.shape, 1)
    outs = []
    for m in range(2):
        qm = jnp.where((lane >= DIFF_DH) if m else (lane < DIFF_DH), q, jnp.zeros_like(q))
        s = _dot_nt(qm, k)
        p = jnp.exp2(s - jnp.max(s, axis=-1, keepdims=True)).astype(BF16)
        ne = _dot(p, v)
        outs.append(ne[:, :DIFF_DV] / ne[:, DIFF_DV:])
    od = outs[0] - lam * outs[1]
    od = od * lax.rsqrt(jnp.mean(od * od, axis=-1, keepdims=True) + NORM_EPS)
    return od * hn * (1.0 - lam_init)


def _diff_kernel(*refs, lam_init, need_ctx, sub_rows):
    if need_ctx:
        q_ref, qx_ref, kx_ref, vx_ref, kl_ref, vl_ref, lam_ref, hn_ref, o_ref, ox_ref, k_scr, v_scr = refs
    else:
        q_ref, kx_ref, vx_ref, kl_ref, vl_ref, lam_ref, hn_ref, o_ref, k_scr, v_scr = refs
    nc = kx_ref.shape[1]
    lam = (jnp.exp(jnp.sum(lam_ref[0:1, :] * lam_ref[1:2, :], axis=-1, keepdims=True))
           - jnp.exp(jnp.sum(lam_ref[2:3, :] * lam_ref[3:4, :], axis=-1, keepdims=True)) + lam_init)
    hn = hn_ref[...]

    @pl.when(pl.program_id(2) == 0)
    def _():
        k_scr[0:nc, :] = kx_ref[0]
        v_scr[0:nc, 0:DIFF_DV] = vx_ref[0]
        k_scr[nc:, :] = kl_ref[0]
        v_scr[nc:, 0:DIFF_DV] = vl_ref[0]
        v_scr[:, DIFF_DV:] = jnp.ones((v_scr.shape[0], DIFF_DV), BF16)
        if need_ctx:
            for r0 in range(0, nc, sub_rows):
                rs = slice(r0, min(nc, r0 + sub_rows))
                ox_ref[0, rs, :] = _diff_rows(qx_ref[0, rs, :], k_scr[0:nc, :], v_scr[0:nc, :], lam, hn,
                                              lam_init).astype(ox_ref.dtype)

    tq = q_ref.shape[1]
    for r0 in range(0, tq, sub_rows):
        rs = slice(r0, min(tq, r0 + sub_rows))
        o_ref[0, rs, :] = _diff_rows(q_ref[0, rs, :], k_scr[...], v_scr[...], lam, hn, lam_init).astype(o_ref.dtype)


def _diff_mix(p, pc, lam, head_norm, lam_init, need_ctx, tq):
    b, t, _ = p.shape
    nc = pc.shape[1]
    w = DIFF_DV
    kblk, vblk = DIFF_HEADS, 2 * DIFF_HEADS
    out_w = DIFF_HEADS * DIFF_DV

    def col_spec(rows, col):
        return pl.BlockSpec((1, rows, w), lambda bi, h, i: (bi, 0, col + h))

    in_specs = [pl.BlockSpec((1, tq, w), lambda bi, h, i: (bi, i, h))]
    args = [p]
    out_specs = [pl.BlockSpec((1, tq, w), lambda bi, h, i: (bi, i, h))]
    out_shape = [jax.ShapeDtypeStruct((b, t, out_w), BF16)]
    if need_ctx:
        in_specs.append(col_spec(nc, 0))
        args.append(pc)
        out_specs.append(col_spec(nc, 0))
        out_shape.append(jax.ShapeDtypeStruct((b, nc, out_w), BF16))
    in_specs += [col_spec(nc, kblk), col_spec(nc, vblk), col_spec(t, kblk), col_spec(t, vblk),
                 pl.BlockSpec((4, DIFF_DH), lambda bi, h, i: (0, 0)),
                 pl.BlockSpec((1, w), lambda bi, h, i: (0, h))]
    args += [pc, pc, p, p, lam, head_norm]
    outs = pl.pallas_call(
        functools.partial(_diff_kernel, lam_init=lam_init, need_ctx=need_ctx, sub_rows=128),
        grid=(b, DIFF_HEADS, t // tq),
        in_specs=in_specs,
        out_specs=out_specs,
        out_shape=out_shape,
        scratch_shapes=[pltpu.VMEM((nc + t, w), BF16), pltpu.VMEM((nc + t, 2 * w), BF16)],
        compiler_params=_params("parallel", "parallel", "arbitrary"),
        name="diff_mix",
    )(*args)
    return (outs[0], outs[1]) if need_ctx else (outs[0], None)


def _post_kernel(*refs, ctx_row, sub_rows, ff_chunk, final, n_cast):
    h_ref, y_ref, g2_ref, sh_ref, sc_ref, g5_ref, gain_ref, wo_ref, w1_ref, w2_ref = refs[:10]
    rest = refs[10:]
    if final:
        fn_ref, rest = rest[0], rest[1:]
    cast_in, o_ref, cast_out = rest[:n_cast], rest[n_cast], rest[n_cast + 1:]
    for src, dst in zip(cast_in, cast_out):
        dst[...] = src[...].astype(dst.dtype)
    row = pl.program_id(0) if ctx_row is None else ctx_row
    g2, g5 = _mod_row(g2_ref, row), _mod_row(g5_ref, row)
    shift, scale = _mod_row(sh_ref, row), _mod_row(sc_ref, row)
    tm = h_ref.shape[1]
    ff = w1_ref.shape[1]
    for r0 in range(0, tm, sub_rows):
        rs = slice(r0, r0 + sub_rows)
        h1 = h_ref[0, rs, :] + g2 * _dot(y_ref[0, rs, :], wo_ref[...])
        u = _norm_mod(h1, gain_ref[...], shift, scale).astype(BF16)
        acc = None
        for c0 in range(0, ff, ff_chunk):
            hidden = jnp.square(jnp.maximum(_dot(u, w1_ref[:, c0:c0 + ff_chunk]), 0.0)).astype(BF16)
            part = _dot(hidden, w2_ref[c0:c0 + ff_chunk, :])
            acc = part if acc is None else acc + part
        out = h1 + g5 * acc
        if final:
            out = out * lax.rsqrt(jnp.mean(out * out, axis=-1, keepdims=True) + NORM_EPS) * fn_ref[...]
        o_ref[0, rs, :] = out


def _post(h, y, mods, layer, gain, wo, w1, w2, *, ctx_row=None, tm, final_gain=None, cast_along=()):
    b, t, d = h.shape
    dy = y.shape[2]
    ff = w1.shape[1]
    r = mods.shape[1]
    assert t % tm == 0
    final = final_gain is not None
    n_i = t // tm
    n_steps = b * n_i

    def mod_spec(k):
        return pl.BlockSpec((1, r, d), lambda bi, i: (layer, 0, k))

    in_specs = [pl.BlockSpec((1, tm, d), lambda bi, i: (bi, i, 0)),
                pl.BlockSpec((1, tm, dy), lambda bi, i: (bi, i, 0)),
                mod_spec(2), mod_spec(3), mod_spec(4), mod_spec(5),
                pl.BlockSpec((1, d), lambda bi, i: (0, 0)),
                _resident((dy, d)), _resident((d, ff)), _resident((ff, d))]
    args = [h, y, mods, mods, mods, mods, gain.reshape(1, d), wo, w1, w2]
    if final:
        in_specs.append(pl.BlockSpec((1, d), lambda bi, i: (0, 0)))
        args.append(final_gain.reshape(1, d))
    out_specs = [pl.BlockSpec((1, tm, d), lambda bi, i: (bi, i, 0))]
    out_shape = [jax.ShapeDtypeStruct((b, t, d), F32)]
    for wnext in cast_along:
        rows, cols = wnext.shape
        assert rows % (n_steps * 2 * SUBLANES) == 0
        slab = pl.BlockSpec((rows // n_steps, cols), lambda bi, i: (bi * n_i + i, 0))
        in_specs.append(slab)
        args.append(wnext)
        out_specs.append(slab)
        out_shape.append(jax.ShapeDtypeStruct(wnext.shape, BF16))
    outs = pl.pallas_call(
        functools.partial(_post_kernel, ctx_row=ctx_row, sub_rows=min(tm, 512), ff_chunk=1024, final=final,
                          n_cast=len(cast_along)),
        grid=(b, n_i),
        in_specs=in_specs,
        out_specs=out_specs,
        out_shape=out_shape,
        compiler_params=_params("parallel", "parallel"),
        name="post",
    )(*args)
    return outs[0] if not cast_along else tuple(outs)


def _rope_tables(n_tok, head_dim):
    rows = n_tok // GRID_W
    row = jnp.repeat(jnp.arange(rows, dtype=jnp.int32), GRID_W).astype(F32)
    col = jnp.tile(jnp.arange(GRID_W, dtype=jnp.int32), rows).astype(F32)
    quarter = head_dim // 4
    inv = ROPE_BASE ** (-jnp.arange(quarter, dtype=F32) / quarter)
    ang = jnp.concatenate([row[:, None] * inv, col[:, None] * inv], axis=-1)
    cos, sin = jnp.cos(ang), jnp.sin(ang)
    return jnp.tile(cos, (1, 4)), jnp.tile(jnp.concatenate([-sin, sin], axis=-1), (1, 2))


def _mlstm_weights(w_in, gate_b):
    d = w_in.shape[0]
    nk = A_HEADS * A_DQK
    main = 2 * nk + 2 * A_HEADS * A_DV
    w = jnp.concatenate([w_in[:, :nk], w_in[:, 2 * nk:main]], axis=1).astype(BF16)
    wg = jnp.transpose(w_in[:, main:].reshape(d, 4, A_PAIRS, 2), (0, 2, 1, 3)).reshape(d, A_GATES)
    wt = jnp.concatenate([w_in[:, nk:2 * nk], wg], axis=1).T.astype(BF16)
    gb = jnp.transpose(gate_b.astype(F32).reshape(4, A_PAIRS, 2), (1, 0, 2)).reshape(A_GATES, 1)
    return w, wt, jnp.broadcast_to(gb, (A_GATES, LANES))


def kernel(x, c, ctx, c_ctx, ada_w, ada_b, norm_mix, norm_ffn, ffn_w1, ffn_w2, mlstm_w_in, mlstm_gate_b, mlstm_head_norm, mlstm_w_out, swa_w_in, swa_sink, swa_w_out, diff_w_in, diff_lambda_q1, diff_lambda_k1, diff_lambda_q2, diff_lambda_k2, diff_head_norm, diff_w_out, final_norm):
    bsz, n_tok, d = x.shape
    n_ctx = ctx.shape[1]
    depth = ada_w.shape[0]
    rows = -(-(bsz + 1) // SUBLANES) * SUBLANES
    cond = jnp.concatenate([c, c_ctx[None, :], jnp.zeros((rows - bsz - 1, d), F32)], axis=0)
    mods = _ada_table(cond, ada_w, ada_b)
    rope = _rope_tables(n_tok, SWA_DH)
    tm_lat, tm_post = min(n_tok, 1024), 512

    h, hc = x, ctx
    w1, w2 = ffn_w1[0].astype(BF16), ffn_w2[0].astype(BF16)
    for i in range(depth):
        kind, slot = i % N_MIXERS, i // N_MIXERS
        need_ctx = i < depth - 1
        if kind == 0:
            w, wt, gb = _mlstm_weights(mlstm_w_in[slot], mlstm_gate_b[slot])
            proj = functools.partial(_project_mlstm, mods=mods, layer=i, gain=norm_mix[i], w=w, wt=wt, gb=gb)
            lat = proj(h, tm=tm_lat)
            cx = proj(hc, tm=n_ctx, ctx_row=bsz)
            y, yc = _mlstm_mix(*lat, *cx, mlstm_head_norm[slot].reshape(1, -1), need_ctx)
            wo = mlstm_w_out[slot]
        else:
            proj = functools.partial(_project, mods=mods, layer=i, gain=norm_mix[i])
            if kind == 1:
                w = swa_w_in[slot].astype(BF16)
                rc = (SWA_HEADS + SWA_KV_HEADS) * SWA_DH
                qs = (SWA_HEADS * SWA_DH, SWA_DH ** -0.5 * LOG2E)
            else:
                w = diff_w_in[slot].astype(BF16)
                rc = 4 * DIFF_HEADS * DIFF_DH
                qs = (2 * DIFF_HEADS * DIFF_DH, DIFF_DH ** -0.5 * LOG2E)
            p = proj(h, w=w, tm=tm_lat, rope=rope, rope_cols=rc, qscale=qs)
            pc = proj(hc, w=w, tm=n_ctx, ctx_row=bsz, qscale=qs)
            if kind == 1:
                y, yc = _swa_mix(p, pc, swa_sink[slot], need_ctx)
                wo = swa_w_out[slot]
            else:
                lam = jnp.stack([diff_lambda_q1[slot], diff_lambda_k1[slot], diff_lambda_q2[slot], diff_lambda_k2[slot]])
                lam_init = 0.8 - 0.6 * math.exp(-0.3 * i)
                y, yc = _diff_mix(p, pc, lam.astype(F32), diff_head_norm[slot].reshape(1, -1), lam_init, need_ctx, tq=min(n_tok, 2048))
                wo = diff_w_out[slot]
        post = functools.partial(_post, mods=mods, layer=i, gain=norm_ffn[i], wo=wo.astype(BF16), w1=w1, w2=w2)
        if i + 1 < depth:
            h, w1, w2 = post(h, y, tm=tm_post, cast_along=(ffn_w1[i + 1], ffn_w2[i + 1]))
        else:
            h = post(h, y, tm=tm_post, final_gain=final_norm)
        if need_ctx:
            hc = post(hc, yc, tm=n_ctx, ctx_row=bsz)
    return h
```

```python
import functools
import math

import jax
import jax.numpy as jnp
from jax import lax
from jax.experimental import pallas as pl
from jax.experimental.pallas import tpu as pltpu

F32 = jnp.float32
BF16 = jnp.bfloat16

LANES = 128
SUBLANES = 8
VMEM_LIMIT_BYTES = 56 * 1024 * 1024
LOG2E = math.log2(math.e)

NORM_EPS = 1e-6
ROPE_BASE = 10000.0
GRID_W = 64
N_MIXERS = 3

A_HEADS = 8
A_DQK = 64
A_DV = 128
A_CHUNK = 128
A_PAIRS = A_HEADS // 2
A_GATES = 4 * A_HEADS

SWA_HEADS = 16
SWA_KV_HEADS = 4
SWA_DH = 64
SWA_GROUP = SWA_HEADS // SWA_KV_HEADS
SWA_BLOCK = 128

DIFF_HEADS = 8
DIFF_DH = 64
DIFF_DV = 128


def _params(*sem):
    return pltpu.CompilerParams(dimension_semantics=sem, vmem_limit_bytes=VMEM_LIMIT_BYTES)


def _dot(a, b):
    return jnp.dot(a, b, preferred_element_type=F32)


def _dot_nt(a, b):
    return lax.dot_general(a, b, (((1,), (1,)), ((), ())), preferred_element_type=F32)


def _norm_mod(x, gain, shift, scale):
    y = x * lax.rsqrt(jnp.mean(x * x, axis=-1, keepdims=True) + NORM_EPS) * gain
    return y * (1.0 + scale) + shift


def _mod_row(ref, row):
    return ref[0, pl.ds(row, 1), :]


def _resident(shape):
    return pl.BlockSpec(shape, lambda *_: (0,) * len(shape), pipeline_mode=pl.Buffered(1))


def _ada_kernel(c_ref, w_ref, b_ref, o_ref):
    c = c_ref[...]
    s = (c * jax.nn.sigmoid(c)).astype(BF16)
    o_ref[0] = _dot(s, w_ref[0].astype(BF16)) + b_ref[0]


def _ada_table(cond, ada_w, ada_b):
    depth, d, n = ada_w.shape
    r = cond.shape[0]
    tn = n // 4
    return pl.pallas_call(
        _ada_kernel,
        grid=(depth, n // tn),
        in_specs=[pl.BlockSpec((r, d), lambda i, j: (0, 0)),
                  pl.BlockSpec((1, d, tn), lambda i, j: (i, 0, j)),
                  pl.BlockSpec((1, 1, tn), lambda i, j: (i, 0, j))],
        out_specs=pl.BlockSpec((1, r, tn), lambda i, j: (i, 0, j)),
        out_shape=jax.ShapeDtypeStruct((depth, r, n), F32),
        compiler_params=_params("parallel", "parallel"),
        name="ada_table",
    )(cond, ada_w, ada_b.reshape(depth, 1, n))


def _rope_block(blk, cos, sin_signed):
    lane = lax.broadcasted_iota(jnp.int32, blk.shape, 1)
    first_half = (lane & 32) == 0
    partner = jnp.where(first_half, pltpu.roll(blk, LANES - 32, 1), pltpu.roll(blk, 32, 1))
    return blk * cos + partner * sin_signed


def _proj_kernel(*refs, ctx_row, n_out, rope_cols, qscale, chunk):
    if rope_cols:
        x_ref, sh_ref, sc_ref, g_ref, w_ref, cos_ref, sin_ref, o_ref = refs
    else:
        x_ref, sh_ref, sc_ref, g_ref, w_ref, o_ref = refs
    row = pl.program_id(0) if ctx_row is None else ctx_row
    u = _norm_mod(x_ref[0], g_ref[...], _mod_row(sh_ref, row), _mod_row(sc_ref, row)).astype(BF16)
    for c0 in range(0, n_out, chunk):
        acc = _dot(u, w_ref[:, c0:c0 + chunk])
        for l0 in range(0, chunk, LANES):
            col = c0 + l0
            blk = acc[:, l0:l0 + LANES]
            if col < rope_cols:
                blk = _rope_block(blk, cos_ref[...], sin_ref[...])
            if col < qscale[0]:
                blk = blk * qscale[1]
            o_ref[0, :, col:col + LANES] = blk.astype(o_ref.dtype)


def _project(h, mods, layer, gain, w, *, ctx_row=None, tm, rope=None, rope_cols=0, qscale):
    b, t, d = h.shape
    n = w.shape[1]
    r = mods.shape[1]
    chunk = 512
    assert t % tm == 0 and n % chunk == 0
    in_specs = [pl.BlockSpec((1, tm, d), lambda bi, i: (bi, i, 0)),
                pl.BlockSpec((1, r, d), lambda bi, i: (layer, 0, 0)),
                pl.BlockSpec((1, r, d), lambda bi, i: (layer, 0, 1)),
                pl.BlockSpec((1, d), lambda bi, i: (0, 0)),
                _resident((d, n))]
    args = [h, mods, mods, gain.reshape(1, d), w]
    if rope_cols:
        in_specs += [pl.BlockSpec((tm, LANES), lambda bi, i: (i, 0))] * 2
        args += list(rope)
    return pl.pallas_call(
        functools.partial(_proj_kernel, ctx_row=ctx_row, n_out=n, rope_cols=rope_cols, qscale=qscale, chunk=chunk),
        grid=(b, t // tm),
        in_specs=in_specs,
        out_specs=pl.BlockSpec((1, tm, n), lambda bi, i: (bi, i, 0)),
        out_shape=jax.ShapeDtypeStruct((b, t, n), BF16),
        compiler_params=_params("parallel", "parallel"),
        name="project",
    )(*args)


def _log_sigmoid(x):
    return jnp.minimum(x, 0.0) - jnp.log1p(jnp.exp(-jnp.abs(x)))


def _lane_scan(x, op, fill, reverse):
    lane = lax.broadcasted_iota(jnp.int32, x.shape, 1)
    k = 1
    while k < LANES:
        if reverse:
            shifted, ok = pltpu.roll(x, LANES - k, 1), lane < LANES - k
        else:
            shifted, ok = pltpu.roll(x, k, 1), lane >= k
        x = op(x, jnp.where(ok, shifted, fill))
        k *= 2
    return x


def _proj_mlstm_kernel(x_ref, sh_ref, sc_ref, g_ref, w_ref, wt_ref, gb_ref, o_ref, kt_ref, gt_ref, ct_ref, ws_ref,
                       *, ctx_row, n_out, chunk):
    row = pl.program_id(0) if ctx_row is None else ctx_row
    u = _norm_mod(x_ref[0], g_ref[...], _mod_row(sh_ref, row), _mod_row(sc_ref, row)).astype(BF16)
    ut = _dot_nt(wt_ref[...], u)
    for c0 in range(0, n_out, chunk):
        o_ref[0, :, c0:c0 + chunk] = _dot(u, w_ref[:, c0:c0 + chunk])
    nk = A_HEADS * A_DQK
    row8 = lax.broadcasted_iota(jnp.int32, (A_GATES, LANES), 0) & 7
    fwd = row8 < 4
    is_cum = (row8 & 2) != 0
    for ci in range(u.shape[0] // A_CHUNK):
        cols = slice(ci * A_CHUNK, (ci + 1) * A_CHUNK)
        kt_ref[0, ci] = ut[:nk, cols] * (A_DQK ** -0.5)
        x = ut[nk:, cols] + gb_ref[...]
        lf = _log_sigmoid(x)
        cum = jnp.where(fwd, _lane_scan(lf, jnp.add, 0.0, False), _lane_scan(lf, jnp.add, 0.0, True))
        gt_ref[0, ci] = jnp.where(is_cum, cum, x)
        cum_up = pltpu.roll(cum, A_GATES - 2, 0)
        r = x - cum_up
        cmax = jnp.where(fwd, _lane_scan(r, jnp.maximum, -jnp.inf, False), _lane_scan(r, jnp.maximum, -jnp.inf, True))
        b_last = jnp.where(fwd, jnp.broadcast_to(cum_up[:, A_CHUNK - 1:A_CHUNK], cum_up.shape),
                           jnp.broadcast_to(cum_up[:, 0:1], cum_up.shape))
        a = (b_last - cum_up) + x
        g = jnp.broadcast_to(jnp.max(a, axis=-1, keepdims=True), a.shape)
        ct_ref[0, ci] = jnp.where(is_cum, pltpu.roll(b_last, 2, 0), cmax)
        ws_ref[0, ci] = jnp.where(is_cum, pltpu.roll(g, 2, 0), jnp.exp(a - g))


def _project_mlstm(h, mods, layer, gain, w, wt, gb, *, ctx_row=None, tm):
    b, t, d = h.shape
    n = w.shape[1]
    r = mods.shape[1]
    nt = wt.shape[0]
    nk = A_HEADS * A_DQK
    chunk = 512
    cpt = tm // A_CHUNK
    assert t % tm == 0 and n % chunk == 0 and tm % A_CHUNK == 0
    return pl.pallas_call(
        functools.partial(_proj_mlstm_kernel, ctx_row=ctx_row, n_out=n, chunk=chunk),
        grid=(b, t // tm),
        in_specs=[pl.BlockSpec((1, tm, d), lambda bi, i: (bi, i, 0)),
                  pl.BlockSpec((1, r, d), lambda bi, i: (layer, 0, 0)),
                  pl.BlockSpec((1, r, d), lambda bi, i: (layer, 0, 1)),
                  pl.BlockSpec((1, d), lambda bi, i: (0, 0)),
                  _resident((d, n)), _resident((nt, d)),
                  pl.BlockSpec((A_GATES, LANES), lambda bi, i: (0, 0))],
        out_specs=[pl.BlockSpec((1, tm, n), lambda bi, i: (bi, i, 0)),
                   pl.BlockSpec((1, cpt, nk, A_CHUNK), lambda bi, i: (bi, i, 0, 0)),
                   ] + [pl.BlockSpec((1, cpt, A_GATES, A_CHUNK), lambda bi, i: (bi, i, 0, 0))] * 3,
        out_shape=[jax.ShapeDtypeStruct((b, t, n), F32),
                   jax.ShapeDtypeStruct((b, t // A_CHUNK, nk, A_CHUNK), F32),
                   ] + [jax.ShapeDtypeStruct((b, t // A_CHUNK, A_GATES, A_CHUNK), F32)] * 3,
        compiler_params=_params("parallel", "parallel"),
        name="project_mlstm",
    )(h, mods, mods, gain.reshape(1, d), w, wt, gb)


def _mlstm_kernel(*refs, n_ctx, n_lat, need_ctx):
    if need_ctx:
        (ql, vl, ol, ktl, gtl, ctl, wsl, qc, vc, oc, ktc, gtc, ctc, wsc, hn_ref, yl_ref, yc_ref,
         kv_scr, cbd_scr, cst_scr, g_scr, bl_scr, m0_scr, hs_scr) = refs
    else:
        (ql, vl, ol, ktl, gtl, ctl, wsl, qc, vc, ktc, gtc, ctc, wsc, hn_ref, yl_ref,
         kv_scr, cbd_scr, cst_scr, g_scr, bl_scr, m0_scr, hs_scr) = refs
        oc = yc_ref = None
    L = A_CHUNK
    n_all = n_ctx + n_lat
    ones_v = jnp.ones((L, A_DV), BF16)

    def v_ext(v2, hh):
        return jnp.concatenate([v2[:, hh * A_DV:(hh + 1) * A_DV].astype(BF16), ones_v], axis=1)

    def rows(gt, dr, hh):
        return gt[dr * 4 + hh:dr * 4 + hh + 1, :], gt[dr * 4 + 2 + hh:dr * 4 + 3 + hh, :]

    def contrib(c, ct, ws, kt, v2):
        for hh in range(2):
            vx = v_ext(v2, hh)
            kth = kt[hh * A_DQK:(hh + 1) * A_DQK, :]
            for dr in range(2):
                w, g = rows(ws, dr, hh)
                idx = c * 4 + dr * 2 + hh
                kv_scr[idx] = _dot((kth * w).astype(BF16), vx)
                g_scr[pl.ds(idx, 1), :] = g
                bl_scr[pl.ds(idx, 1), :] = rows(ct, dr, hh)[1]

    for c in range(n_ctx):
        contrib(c, ctc[0, c], wsc[0, c], ktc[0, c], vc[0, c * L:(c + 1) * L, :])

    def contrib_lat(c, carry):
        contrib(c + n_ctx, ctl[0, c], wsl[0, c], ktl[0, c], vl[0, pl.ds(pl.multiple_of(c * L, L), L), :])
        return carry

    lax.fori_loop(0, n_lat, contrib_lat, 0, unroll=4)

    cst_scr[...] = jnp.zeros_like(cst_scr)
    zpad = jnp.zeros((A_DQK, 2 * A_DV), BF16)

    def scan_step(i, ms):
        c_bwd = jnp.where(i < n_ctx, n_ctx - 1 - i, n_all - 1 - (i - n_ctx))
        new_ms = []
        for dr, c in ((0, i), (1, c_bwd)):
            c0s = [cst_scr[dr * 2 + hh] for hh in range(2)]
            for hh in range(2):
                idx = c * 4 + dr * 2 + hh
                c0b = c0s[hh].astype(BF16)
                cbd_scr[idx] = jnp.concatenate([zpad, c0b] if hh else [c0b, zpad], axis=0)
                m0 = ms[dr * 2 + hh]
                m0_scr[pl.ds(idx, 1), :] = m0
                g, b_last = g_scr[pl.ds(idx, 1), :], bl_scr[pl.ds(idx, 1), :]
                m_new = jnp.maximum(b_last + m0, g)
                decay = jnp.exp(b_last + m0 - m_new)
                inject = jnp.exp(g - m_new)
                decay, inject = (jnp.concatenate([z, z], axis=1) for z in (decay, inject))
                cst_scr[dr * 2 + hh] = decay * c0s[hh] + inject * kv_scr[idx]
                new_ms.append(m_new)
        return tuple(new_ms)

    lax.fori_loop(0, n_all, scan_step, tuple(jnp.zeros((1, L), F32) for _ in range(4)))

    t_i = lax.broadcasted_iota(jnp.int32, (L, L), 0)
    s_i = lax.broadcasted_iota(jnp.int32, (L, L), 1)
    masks = (s_i <= t_i, s_i >= t_i)
    zk = jnp.zeros((A_DQK, L), BF16)
    zrows = jnp.zeros((L - 2 * SUBLANES, L), F32)

    def mix(c, gt, ct, kt, q2, v2):
        colm = jnp.concatenate([gt, ct, zrows], axis=0).T
        qb = q2.astype(BF16)
        ktb = kt.astype(BF16)
        kt_bd = jnp.concatenate([jnp.concatenate([ktb[:A_DQK], zk], axis=1),
                                 jnp.concatenate([zk, ktb[A_DQK:]], axis=1)], axis=0)
        s2 = _dot(qb, kt_bd)
        vxs = [v_ext(v2, hh) for hh in range(2)]
        hsum = [None, None]
        for dr in range(2):
            for hh in range(2):
                idx = c * 4 + dr * 2 + hh
                li, cum = rows(gt, dr, hh)
                m0 = m0_scr[pl.ds(idx, 1), :]
                cmax = jnp.broadcast_to(colm[:, 8 + dr * 4 + hh:9 + dr * 4 + hh], (L, L))
                cum_t = jnp.broadcast_to(colm[:, dr * 4 + 2 + hh:dr * 4 + 3 + hh], (L, L))
                mm = jnp.maximum(cmax, m0)
                p = jnp.where(masks[dr], jnp.exp((li - cum) - mm), 0.0)
                wq = (p * s2[:, hh * L:(hh + 1) * L]).astype(BF16)
                carry = jnp.exp(m0 - mm)
                lhs = jnp.concatenate([wq, (q2 * carry).astype(BF16)], axis=1)
                ne = _dot(lhs, jnp.concatenate([vxs[hh], cbd_scr[idx]], axis=0))
                h = ne[:, :A_DV] / jnp.maximum(jnp.abs(ne[:, A_DV:]), jnp.exp(-(cum_t + mm)))
                hsum[hh] = h if dr == 0 else hsum[hh] + h
        return hsum

    def finish(hs2, o2):
        ys = []
        for hh in range(2):
            hs = hs2[:, hh * A_DV:(hh + 1) * A_DV]
            hn = hs * lax.rsqrt(jnp.mean(hs * hs, axis=-1, keepdims=True) + NORM_EPS)
            hn = hn * hn_ref[:, hh * A_DV:(hh + 1) * A_DV]
            ys.append(hn * jax.nn.sigmoid(o2[:, hh * A_DV:(hh + 1) * A_DV]))
        return jnp.concatenate(ys, axis=1).astype(BF16)

    if need_ctx:
        for c in range(n_ctx):
            sl = slice(c * L, (c + 1) * L)
            hs = mix(c, gtc[0, c], ctc[0, c], ktc[0, c], qc[0, sl, :], vc[0, sl, :])
            yc_ref[0, sl, :] = finish(jnp.concatenate(hs, axis=1), oc[0, sl, :])

    def mix_lat(c):
        src = pl.ds(c * L if isinstance(c, int) else pl.multiple_of(c * L, L), L)
        hs = mix(c + n_ctx, gtl[0, c], ctl[0, c], ktl[0, c], ql[0, src, :], vl[0, src, :])
        hs_scr[...] = jnp.concatenate(hs, axis=1)

    def finish_lat(c):
        src = pl.ds(c * L if isinstance(c, int) else pl.multiple_of(c * L, L), L)
        yl_ref[0, src, :] = finish(hs_scr[...], ol[0, src, :])

    def pipelined(c, carry):
        finish_lat(c - 1)
        mix_lat(c)
        return carry

    mix_lat(0)
    lax.fori_loop(1, n_lat, pipelined, 0)
    finish_lat(n_lat - 1)


def _mlstm_mix(p, kt, gt, ct, ws, pc, ktc, gtc, ctc, wsc, head_norm, need_ctx):
    b, t, _ = p.shape
    nc = pc.shape[1]
    L = A_CHUNK
    n_lat, n_ctx = t // L, nc // L
    n_all = n_lat + n_ctx
    qw, vw = 2 * A_DQK, 2 * A_DV
    v_blk = A_HEADS * A_DQK // vw
    o_blk = v_blk + A_PAIRS

    def specs(rows, nch, with_o):
        s = [pl.BlockSpec((1, rows, qw), lambda bi, hp: (bi, 0, hp)),
             pl.BlockSpec((1, rows, vw), lambda bi, hp: (bi, 0, v_blk + hp))]
        if with_o:
            s.append(pl.BlockSpec((1, rows, vw), lambda bi, hp: (bi, 0, o_blk + hp)))
        s.append(pl.BlockSpec((1, nch, qw, L), lambda bi, hp: (bi, 0, hp, 0)))
        s += [pl.BlockSpec((1, nch, SUBLANES, L), lambda bi, hp: (bi, 0, hp, 0))] * 3
        return s

    in_specs = specs(t, n_lat, True) + specs(nc, n_ctx, need_ctx)
    in_specs.append(pl.BlockSpec((1, vw), lambda bi, hp: (0, hp)))
    args = [p, p, p, kt, gt, ct, ws] + ([pc, pc, pc] if need_ctx else [pc, pc]) + [ktc, gtc, ctc, wsc, head_norm]
    out_specs = [pl.BlockSpec((1, t, vw), lambda bi, hp: (bi, 0, hp))]
    out_shape = [jax.ShapeDtypeStruct((b, t, A_HEADS * A_DV), BF16)]
    if need_ctx:
        out_specs.append(pl.BlockSpec((1, nc, vw), lambda bi, hp: (bi, 0, hp)))
        out_shape.append(jax.ShapeDtypeStruct((b, nc, A_HEADS * A_DV), BF16))
    n_rows = -(-n_all * 4 // SUBLANES) * SUBLANES
    outs = pl.pallas_call(
        functools.partial(_mlstm_kernel, n_ctx=n_ctx, n_lat=n_lat, need_ctx=need_ctx),
        grid=(b, A_PAIRS),
        in_specs=in_specs,
        out_specs=out_specs,
        out_shape=out_shape,
        scratch_shapes=[pltpu.VMEM((n_all * 4, A_DQK, vw), F32),
                        pltpu.VMEM((n_all * 4, qw, vw), BF16),
                        pltpu.VMEM((4, A_DQK, vw), F32),
                        pltpu.VMEM((n_rows, L), F32), pltpu.VMEM((n_rows, L), F32), pltpu.VMEM((n_rows, L), F32),
                        pltpu.VMEM((L, vw), F32)],
        compiler_params=_params("parallel", "parallel"),
        name="mlstm_mix",
    )(*args)
    return (outs[0], outs[1]) if need_ctx else (outs[0], None)


def _swa_attend(q4, k, v, valid, sink_col):
    s = _dot_nt(q4, k)
    if valid is not None:
        s = jnp.where(valid, s, -jnp.inf)
    m = jnp.maximum(jnp.max(s, axis=-1, keepdims=True), sink_col)
    p = jnp.exp2(s - m)
    den = jnp.sum(p, axis=-1, keepdims=True) + jnp.exp2(sink_col - m)
    return _dot(p.astype(BF16), v) / den


def _swa_heads(q, keys, vals, valid, sink_ref, o_ref, rows, row0=0):
    row_i = lax.broadcasted_iota(jnp.int32, (SWA_GROUP * rows, 1), 0)
    for hk in range(SWA_KV_HEADS):
        ks = slice(hk * SWA_DH, (hk + 1) * SWA_DH)
        k = jnp.concatenate([x[:, ks] for x in keys], axis=0)
        v = jnp.concatenate([x[:, ks] for x in vals], axis=0)
        q4 = jnp.concatenate([q[:, (hk * SWA_GROUP + g) * SWA_DH:(hk * SWA_GROUP + g + 1) * SWA_DH]
                              for g in range(SWA_GROUP)], axis=0)
        sink_col = jnp.zeros((SWA_GROUP * rows, 1), F32)
        for g in range(SWA_GROUP):
            head = hk * SWA_GROUP + g
            sink_col = jnp.where((row_i >= g * rows) & (row_i < (g + 1) * rows),
                                 sink_ref[:, head:head + 1] * LOG2E, sink_col)
        o4 = _swa_attend(q4, k, v, valid, sink_col)
        for g in range(0, SWA_GROUP, 2):
            col = (hk * SWA_GROUP + g) * SWA_DH
            pair = jnp.concatenate([o4[g * rows:(g + 1) * rows], o4[(g + 1) * rows:(g + 2) * rows]], axis=1)
            o_ref[0, row0:row0 + rows, col:col + 2 * SWA_DH] = pair.astype(o_ref.dtype)


def _swa_kernel(*refs, n_tok, nqb, need_ctx):
    if need_ctx:
        q_ref, k_ref, v_ref, kx_ref, vx_ref, sink_ref, qx_ref, o_ref, ox_ref = refs
    else:
        q_ref, k_ref, v_ref, kx_ref, vx_ref, sink_ref, o_ref = refs
    step = pl.program_id(1)
    if need_ctx:
        @pl.when(step == 0)
        def _():
            _swa_heads(qx_ref[0], [kx_ref[0]], [vx_ref[0]], None, sink_ref, ox_ref, qx_ref.shape[1])
    L = SWA_BLOCK
    nc = kx_ref.shape[1]
    qi = lax.broadcasted_iota(jnp.int32, (L, 3 * L), 0)
    ki = lax.broadcasted_iota(jnp.int32, (L, 3 * L), 1)
    ctx_valid = jnp.ones((L, nc), jnp.bool_)
    for jb in range(nqb):
        j = step * nqb + jb
        start = pl.multiple_of(jnp.clip((j - 1) * L, 0, n_tok - 3 * L), L)
        rel = (start - j * L) + ki - qi
        valid = jnp.concatenate([jnp.abs(rel) <= L, ctx_valid], axis=1)
        valid = jnp.concatenate([valid] * SWA_GROUP, axis=0)
        _swa_heads(q_ref[0, jb * L:(jb + 1) * L, :], [k_ref[0, pl.ds(start, 3 * L), :], kx_ref[0]],
                   [v_ref[0, pl.ds(start, 3 * L), :], vx_ref[0]], valid, sink_ref, o_ref, L, row0=jb * L)


def _swa_mix(p, pc, sink, need_ctx):
    b, t, _ = p.shape
    nc = pc.shape[1]
    L = SWA_BLOCK
    nblk = t // L
    nqb = 4 if nblk % 4 == 0 else 1
    assert t >= 3 * L
    qw = SWA_HEADS * SWA_DH
    kvw = SWA_KV_HEADS * SWA_DH
    kblk, vblk = qw // kvw, qw // kvw + 1
    in_specs = [pl.BlockSpec((1, nqb * L, qw), lambda bi, j: (bi, j, 0)),
                pl.BlockSpec((1, t, kvw), lambda bi, j: (bi, 0, kblk)),
                pl.BlockSpec((1, t, kvw), lambda bi, j: (bi, 0, vblk)),
                pl.BlockSpec((1, nc, kvw), lambda bi, j: (bi, 0, kblk)),
                pl.BlockSpec((1, nc, kvw), lambda bi, j: (bi, 0, vblk)),
                pl.BlockSpec((1, SWA_HEADS), lambda bi, j: (0, 0))]
    args = [p, p, p, pc, pc, sink.reshape(1, SWA_HEADS)]
    out_specs = [pl.BlockSpec((1, nqb * L, qw), lambda bi, j: (bi, j, 0))]
    out_shape = [jax.ShapeDtypeStruct((b, t, qw), BF16)]
    if need_ctx:
        in_specs.append(pl.BlockSpec((1, nc, qw), lambda bi, j: (bi, 0, 0)))
        args.append(pc)
        out_specs.append(pl.BlockSpec((1, nc, qw), lambda bi, j: (bi, 0, 0)))
        out_shape.append(jax.ShapeDtypeStruct((b, nc, qw), BF16))
    outs = pl.pallas_call(
        functools.partial(_swa_kernel, n_tok=t, nqb=nqb, need_ctx=need_ctx),
        grid=(b, nblk // nqb),
        in_specs=in_specs,
        out_specs=out_specs,
        out_shape=out_shape,
        compiler_params=_params("parallel", "arbitrary"),
        name="swa_mix",
    )(*args)
    return (outs[0], outs[1]) if need_ctx else (outs[0], None)


def _diff_rows(q, k, v, lam, hn, lam_init):
    lane = lax.broadcasted_iota(jnp.int32, q.shape, 1)
    outs = []
    for m in range(2):
        qm = jnp.where((lane >= DIFF_DH) if m else (lane < DIFF_DH), q, jnp.zeros_like(q))
        s = _dot_nt(qm, k)
        p = jnp.exp2(s - jnp.max(s, axis=-1, keepdims=True)).astype(BF16)
        ne = _dot(p, v)
        outs.append(ne[:, :DIFF_DV] / ne[:, DIFF_DV:])
    od = outs[0] - lam * outs[1]
    od = od * lax.rsqrt(jnp.mean(od * od, axis=-1, keepdims=True) + NORM_EPS)
    return od * hn * (1.0 - lam_init)


def _diff_kernel(*refs, lam_init, need_ctx, sub_rows):
    if need_ctx:
        q_ref, qx_ref, kx_ref, vx_ref, kl_ref, vl_ref, lam_ref, hn_ref, o_ref, ox_ref, k_scr, v_scr = refs
    else:
        q_ref, kx_ref, vx_ref, kl_ref, vl_ref, lam_ref, hn_ref, o_ref, k_scr, v_scr = refs
    nc = kx_ref.shape[1]
    lam = (jnp.exp(jnp.sum(lam_ref[0:1, :] * lam_ref[1:2, :], axis=-1, keepdims=True))
           - jnp.exp(jnp.sum(lam_ref[2:3, :] * lam_ref[3:4, :], axis=-1, keepdims=True)) + lam_init)
    hn = hn_ref[...]

    @pl.when(pl.program_id(2) == 0)
    def _():
        k_scr[0:nc, :] = kx_ref[0]
        v_scr[0:nc, 0:DIFF_DV] = vx_ref[0]
        k_scr[nc:, :] = kl_ref[0]
        v_scr[nc:, 0:DIFF_DV] = vl_ref[0]
        v_scr[:, DIFF_DV:] = jnp.ones((v_scr.shape[0], DIFF_DV), BF16)
        if need_ctx:
            for r0 in range(0, nc, sub_rows):
                rs = slice(r0, min(nc, r0 + sub_rows))
                ox_ref[0, rs, :] = _diff_rows(qx_ref[0, rs, :], k_scr[0:nc, :], v_scr[0:nc, :], lam, hn,
                                              lam_init).astype(ox_ref.dtype)

    tq = q_ref.shape[1]
    for r0 in range(0, tq, sub_rows):
        rs = slice(r0, min(tq, r0 + sub_rows))
        o_ref[0, rs, :] = _diff_rows(q_ref[0, rs, :], k_scr[...], v_scr[...], lam, hn, lam_init).astype(o_ref.dtype)


def _diff_mix(p, pc, lam, head_norm, lam_init, need_ctx, tq):
    b, t, _ = p.shape
    nc = pc.shape[1]
    w = DIFF_DV
    kblk, vblk = DIFF_HEADS, 2 * DIFF_HEADS
    out_w = DIFF_HEADS * DIFF_DV

    def col_spec(rows, col):
        return pl.BlockSpec((1, rows, w), lambda bi, h, i: (bi, 0, col + h))

    in_specs = [pl.BlockSpec((1, tq, w), lambda bi, h, i: (bi, i, h))]
    args = [p]
    out_specs = [pl.BlockSpec((1, tq, w), lambda bi, h, i: (bi, i, h))]
    out_shape = [jax.ShapeDtypeStruct((b, t, out_w), BF16)]
    if need_ctx:
        in_specs.append(col_spec(nc, 0))
        args.append(pc)
        out_specs.append(col_spec(nc, 0))
        out_shape.append(jax.ShapeDtypeStruct((b, nc, out_w), BF16))
    in_specs += [col_spec(nc, kblk), col_spec(nc, vblk), col_spec(t, kblk), col_spec(t, vblk),
                 pl.BlockSpec((4, DIFF_DH), lambda bi, h, i: (0, 0)),
                 pl.BlockSpec((1, w), lambda bi, h, i: (0, h))]
    args += [pc, pc, p, p, lam, head_norm]
    outs = pl.pallas_call(
        functools.partial(_diff_kernel, lam_init=lam_init, need_ctx=need_ctx, sub_rows=128),
        grid=(b, DIFF_HEADS, t // tq),
        in_specs=in_specs,
        out_specs=out_specs,
        out_shape=out_shape,
        scratch_shapes=[pltpu.VMEM((nc + t, w), BF16), pltpu.VMEM((nc + t, 2 * w), BF16)],
        compiler_params=_params("parallel", "parallel", "arbitrary"),
        name="diff_mix",
    )(*args)
    return (outs[0], outs[1]) if need_ctx else (outs[0], None)


def _post_kernel(*refs, ctx_row, sub_rows, ff_chunk, final, n_cast):
    h_ref, y_ref, g2_ref, sh_ref, sc_ref, g5_ref, gain_ref, wo_ref, w1_ref, w2_ref = refs[:10]
    rest = refs[10:]
    if final:
        fn_ref, rest = rest[0], rest[1:]
    cast_in, o_ref, cast_out = rest[:n_cast], rest[n_cast], rest[n_cast + 1:]
    for src, dst in zip(cast_in, cast_out):
        dst[...] = src[...].astype(dst.dtype)
    row = pl.program_id(0) if ctx_row is None else ctx_row
    g2, g5 = _mod_row(g2_ref, row), _mod_row(g5_ref, row)
    shift, scale = _mod_row(sh_ref, row), _mod_row(sc_ref, row)
    tm = h_ref.shape[1]
    ff = w1_ref.shape[1]
    for r0 in range(0, tm, sub_rows):
        rs = slice(r0, r0 + sub_rows)
        h1 = h_ref[0, rs, :] + g2 * _dot(y_ref[0, rs, :], wo_ref[...])
        u = _norm_mod(h1, gain_ref[...], shift, scale).astype(BF16)
        acc = None
        for c0 in range(0, ff, ff_chunk):
            hidden = jnp.square(jnp.maximum(_dot(u, w1_ref[:, c0:c0 + ff_chunk]), 0.0)).astype(BF16)
            part = _dot(hidden, w2_ref[c0:c0 + ff_chunk, :])
            acc = part if acc is None else acc + part
        out = h1 + g5 * acc
        if final:
            out = out * lax.rsqrt(jnp.mean(out * out, axis=-1, keepdims=True) + NORM_EPS) * fn_ref[...]
        o_ref[0, rs, :] = out


def _post(h, y, mods, layer, gain, wo, w1, w2, *, ctx_row=None, tm, final_gain=None, cast_along=()):
    b, t, d = h.shape
    dy = y.shape[2]
    ff = w1.shape[1]
    r = mods.shape[1]
    assert t % tm == 0
    final = final_gain is not None
    n_i = t // tm
    n_steps = b * n_i

    def mod_spec(k):
        return pl.BlockSpec((1, r, d), lambda bi, i: (layer, 0, k))

    in_specs = [pl.BlockSpec((1, tm, d), lambda bi, i: (bi, i, 0)),
                pl.BlockSpec((1, tm, dy), lambda bi, i: (bi, i, 0)),
                mod_spec(2), mod_spec(3), mod_spec(4), mod_spec(5),
                pl.BlockSpec((1, d), lambda bi, i: (0, 0)),
                _resident((dy, d)), _resident((d, ff)), _resident((ff, d))]
    args = [h, y, mods, mods, mods, mods, gain.reshape(1, d), wo, w1, w2]
    if final:
        in_specs.append(pl.BlockSpec((1, d), lambda bi, i: (0, 0)))
        args.append(final_gain.reshape(1, d))
    out_specs = [pl.BlockSpec((1, tm, d), lambda bi, i: (bi, i, 0))]
    out_shape = [jax.ShapeDtypeStruct((b, t, d), F32)]
    for stacked, li in cast_along:
        _, rows, cols = stacked.shape
        assert rows % (n_steps * 2 * SUBLANES) == 0
        in_specs.append(pl.BlockSpec((1, rows // n_steps, cols), lambda bi, i, li=li: (li, bi * n_i + i, 0)))
        args.append(stacked)
        out_specs.append(pl.BlockSpec((1, rows // n_steps, cols), lambda bi, i: (0, bi * n_i + i, 0)))
        out_shape.append(jax.ShapeDtypeStruct((1, rows, cols), BF16))
    outs = pl.pallas_call(
        functools.partial(_post_kernel, ctx_row=ctx_row, sub_rows=min(tm, 512), ff_chunk=1024, final=final,
                          n_cast=len(cast_along)),
        grid=(b, n_i),
        in_specs=in_specs,
        out_specs=out_specs,
        out_shape=out_shape,
        compiler_params=_params("parallel", "parallel"),
        name="post",
    )(*args)
    return outs[0] if not cast_along else (outs[0],) + tuple(o[0] for o in outs[1:])


def _rope_tables(n_tok, head_dim):
    rows = n_tok // GRID_W
    row = jnp.repeat(jnp.arange(rows, dtype=jnp.int32), GRID_W).astype(F32)
    col = jnp.tile(jnp.arange(GRID_W, dtype=jnp.int32), rows).astype(F32)
    quarter = head_dim // 4
    inv = ROPE_BASE ** (-jnp.arange(quarter, dtype=F32) / quarter)
    ang = jnp.concatenate([row[:, None] * inv, col[:, None] * inv], axis=-1)
    cos, sin = jnp.cos(ang), jnp.sin(ang)
    return jnp.tile(cos, (1, 4)), jnp.tile(jnp.concatenate([-sin, sin], axis=-1), (1, 2))


def _mlstm_weights(w_in, gate_b):
    d = w_in.shape[0]
    nk = A_HEADS * A_DQK
    main = 2 * nk + 2 * A_HEADS * A_DV
    w = jnp.concatenate([w_in[:, :nk], w_in[:, 2 * nk:main]], axis=1).astype(BF16)
    wg = jnp.transpose(w_in[:, main:].reshape(d, 4, A_PAIRS, 2), (0, 2, 1, 3)).reshape(d, A_GATES)
    wt = jnp.concatenate([w_in[:, nk:2 * nk], wg], axis=1).T.astype(BF16)
    gb = jnp.transpose(gate_b.astype(F32).reshape(4, A_PAIRS, 2), (1, 0, 2)).reshape(A_GATES, 1)
    return w, wt, jnp.broadcast_to(gb, (A_GATES, LANES))


def kernel(x, c, ctx, c_ctx, ada_w, ada_b, norm_mix, norm_ffn, ffn_w1, ffn_w2, mlstm_w_in, mlstm_gate_b, mlstm_head_norm, mlstm_w_out, swa_w_in, swa_sink, swa_w_out, diff_w_in, diff_lambda_q1, diff_lambda_k1, diff_lambda_q2, diff_lambda_k2, diff_head_norm, diff_w_out, final_norm):
    bsz, n_tok, d = x.shape
    n_ctx = ctx.shape[1]
    depth = ada_w.shape[0]
    rows = -(-(bsz + 1) // SUBLANES) * SUBLANES
    cond = jnp.concatenate([c, c_ctx[None, :], jnp.zeros((rows - bsz - 1, d), F32)], axis=0)
    mods = _ada_table(cond, ada_w, ada_b)
    rope = _rope_tables(n_tok, SWA_DH)
    tm_lat, tm_post = min(n_tok, 1024), 512

    h, hc = x, ctx
    w1, w2 = ffn_w1[0].astype(BF16), ffn_w2[0].astype(BF16)
    for i in range(depth):
        kind, slot = i % N_MIXERS, i // N_MIXERS
        need_ctx = i < depth - 1
        if kind == 0:
            w, wt, gb = _mlstm_weights(mlstm_w_in[slot], mlstm_gate_b[slot])
            proj = functools.partial(_project_mlstm, mods=mods, layer=i, gain=norm_mix[i], w=w, wt=wt, gb=gb)
            lat = proj(h, tm=tm_lat)
            cx = proj(hc, tm=n_ctx, ctx_row=bsz)
            y, yc = _mlstm_mix(*lat, *cx, mlstm_head_norm[slot].reshape(1, -1), need_ctx)
            wo = mlstm_w_out[slot]
        else:
            proj = functools.partial(_project, mods=mods, layer=i, gain=norm_mix[i])
            if kind == 1:
                w = swa_w_in[slot].astype(BF16)
                rc = (SWA_HEADS + SWA_KV_HEADS) * SWA_DH
                qs = (SWA_HEADS * SWA_DH, SWA_DH ** -0.5 * LOG2E)
            else:
                w = diff_w_in[slot].astype(BF16)
                rc = 4 * DIFF_HEADS * DIFF_DH
                qs = (2 * DIFF_HEADS * DIFF_DH, DIFF_DH ** -0.5 * LOG2E)
            p = proj(h, w=w, tm=tm_lat, rope=rope, rope_cols=rc, qscale=qs)
            pc = proj(hc, w=w, tm=n_ctx, ctx_row=bsz, qscale=qs)
            if kind == 1:
                y, yc = _swa_mix(p, pc, swa_sink[slot], need_ctx)
                wo = swa_w_out[slot]
            else:
                lam = jnp.stack([diff_lambda_q1[slot], diff_lambda_k1[slot], diff_lambda_q2[slot], diff_lambda_k2[slot]])
                lam_init = 0.8 - 0.6 * math.exp(-0.3 * i)
                y, yc = _diff_mix(p, pc, lam.astype(F32), diff_head_norm[slot].reshape(1, -1), lam_init, need_ctx, tq=min(n_tok, 2048))
                wo = diff_w_out[slot]
        post = functools.partial(_post, mods=mods, layer=i, gain=norm_ffn[i], wo=wo.astype(BF16), w1=w1, w2=w2)
        if i + 1 < depth:
            h, w1, w2 = post(h, y, tm=tm_post, cast_along=((ffn_w1, i + 1), (ffn_w2, i + 1)))
        else:
            h = post(h, y, tm=tm_post, final_gain=final_norm)
        if need_ctx:
            hc = post(hc, yc, tm=n_ctx, ctx_row=bsz)
    return h
```

```python
import functools
import math

import jax
import jax.numpy as jnp
from jax import lax
from jax.experimental import pallas as pl
from jax.experimental.pallas import tpu as pltpu

F32 = jnp.float32
BF16 = jnp.bfloat16

LANES = 128
SUBLANES = 8
VMEM_LIMIT_BYTES = 56 * 1024 * 1024
LOG2E = math.log2(math.e)

NORM_EPS = 1e-6
ROPE_BASE = 10000.0
GRID_W = 64
N_MIXERS = 3

A_HEADS = 8
A_DQK = 64
A_DV = 128
A_CHUNK = 128
A_PAIRS = A_HEADS // 2
A_GATES = 4 * A_HEADS

SWA_HEADS = 16
SWA_KV_HEADS = 4
SWA_DH = 64
SWA_GROUP = SWA_HEADS // SWA_KV_HEADS
SWA_BLOCK = 128

DIFF_HEADS = 8
DIFF_DH = 64
DIFF_DV = 128


def _params(*sem):
    return pltpu.CompilerParams(dimension_semantics=sem, vmem_limit_bytes=VMEM_LIMIT_BYTES)


def _dot(a, b):
    return jnp.dot(a, b, preferred_element_type=F32)


def _dot_nt(a, b):
    return lax.dot_general(a, b, (((1,), (1,)), ((), ())), preferred_element_type=F32)


def _norm_mod(x, gain, shift, scale):
    y = x * lax.rsqrt(jnp.mean(x * x, axis=-1, keepdims=True) + NORM_EPS) * gain
    return y * (1.0 + scale) + shift


def _mod_row(ref, row):
    return ref[0, pl.ds(row, 1), :]


def _resident(shape):
    return pl.BlockSpec(shape, lambda *_: (0,) * len(shape), pipeline_mode=pl.Buffered(1))


def _ada_kernel(c_ref, w_ref, b_ref, o_ref):
    c = c_ref[...]
    s = (c * jax.nn.sigmoid(c)).astype(BF16)
    o_ref[0] = _dot(s, w_ref[0].astype(BF16)) + b_ref[0]


def _ada_table(cond, ada_w, ada_b):
    depth, d, n = ada_w.shape
    r = cond.shape[0]
    tn = n // 4
    return pl.pallas_call(
        _ada_kernel,
        grid=(depth, n // tn),
        in_specs=[pl.BlockSpec((r, d), lambda i, j: (0, 0)),
                  pl.BlockSpec((1, d, tn), lambda i, j: (i, 0, j)),
                  pl.BlockSpec((1, 1, tn), lambda i, j: (i, 0, j))],
        out_specs=pl.BlockSpec((1, r, tn), lambda i, j: (i, 0, j)),
        out_shape=jax.ShapeDtypeStruct((depth, r, n), F32),
        compiler_params=_params("parallel", "parallel"),
        name="ada_table",
    )(cond, ada_w, ada_b.reshape(depth, 1, n))


def _rope_block(blk, cos, sin_signed):
    lane = lax.broadcasted_iota(jnp.int32, blk.shape, 1)
    first_half = (lane & 32) == 0
    partner = jnp.where(first_half, pltpu.roll(blk, LANES - 32, 1), pltpu.roll(blk, 32, 1))
    return blk * cos + partner * sin_signed


def _proj_kernel(*refs, ctx_row, n_out, rope_cols, qscale, chunk):
    if rope_cols:
        x_ref, sh_ref, sc_ref, g_ref, w_ref, cos_ref, sin_ref, o_ref = refs
    else:
        x_ref, sh_ref, sc_ref, g_ref, w_ref, o_ref = refs
    row = pl.program_id(0) if ctx_row is None else ctx_row
    u = _norm_mod(x_ref[0], g_ref[...], _mod_row(sh_ref, row), _mod_row(sc_ref, row)).astype(BF16)
    for c0 in range(0, n_out, chunk):
        acc = _dot(u, w_ref[:, c0:c0 + chunk])
        for l0 in range(0, chunk, LANES):
            col = c0 + l0
            blk = acc[:, l0:l0 + LANES]
            if col < rope_cols:
                blk = _rope_block(blk, cos_ref[...], sin_ref[...])
            if col < qscale[0]:
                blk = blk * qscale[1]
            o_ref[0, :, col:col + LANES] = blk.astype(o_ref.dtype)


def _project(h, mods, layer, gain, w, *, ctx_row=None, tm, rope=None, rope_cols=0, qscale):
    b, t, d = h.shape
    n = w.shape[1]
    r = mods.shape[1]
    chunk = 512
    assert t % tm == 0 and n % chunk == 0
    in_specs = [pl.BlockSpec((1, tm, d), lambda bi, i: (bi, i, 0)),
                pl.BlockSpec((1, r, d), lambda bi, i: (layer, 0, 0)),
                pl.BlockSpec((1, r, d), lambda bi, i: (layer, 0, 1)),
                pl.BlockSpec((1, d), lambda bi, i: (0, 0)),
                _resident((d, n))]
    args = [h, mods, mods, gain.reshape(1, d), w]
    if rope_cols:
        in_specs += [pl.BlockSpec((tm, LANES), lambda bi, i: (i, 0))] * 2
        args += list(rope)
    return pl.pallas_call(
        functools.partial(_proj_kernel, ctx_row=ctx_row, n_out=n, rope_cols=rope_cols, qscale=qscale, chunk=chunk),
        grid=(b, t // tm),
        in_specs=in_specs,
        out_specs=pl.BlockSpec((1, tm, n), lambda bi, i: (bi, i, 0)),
        out_shape=jax.ShapeDtypeStruct((b, t, n), BF16),
        compiler_params=_params("parallel", "parallel"),
        name="project",
    )(*args)


def _log_sigmoid(x):
    return jnp.minimum(x, 0.0) - jnp.log1p(jnp.exp(-jnp.abs(x)))


def _lane_scan(x, op, fill, reverse):
    lane = lax.broadcasted_iota(jnp.int32, x.shape, 1)
    k = 1
    while k < LANES:
        if reverse:
            shifted, ok = pltpu.roll(x, LANES - k, 1), lane < LANES - k
        else:
            shifted, ok = pltpu.roll(x, k, 1), lane >= k
        x = op(x, jnp.where(ok, shifted, fill))
        k *= 2
    return x


def _proj_mlstm_kernel(x_ref, sh_ref, sc_ref, g_ref, w_ref, wt_ref, gb_ref, o_ref, kt_ref, gt_ref, ct_ref, ws_ref,
                       *, ctx_row, n_out, chunk):
    row = pl.program_id(0) if ctx_row is None else ctx_row
    u = _norm_mod(x_ref[0], g_ref[...], _mod_row(sh_ref, row), _mod_row(sc_ref, row)).astype(BF16)
    ut = _dot_nt(wt_ref[...], u)
    for c0 in range(0, n_out, chunk):
        o_ref[0, :, c0:c0 + chunk] = _dot(u, w_ref[:, c0:c0 + chunk])
    nk = A_HEADS * A_DQK
    row8 = lax.broadcasted_iota(jnp.int32, (A_GATES, LANES), 0) & 7
    fwd = row8 < 4
    is_cum = (row8 & 2) != 0
    for ci in range(u.shape[0] // A_CHUNK):
        cols = slice(ci * A_CHUNK, (ci + 1) * A_CHUNK)
        kt_ref[0, ci] = ut[:nk, cols] * (A_DQK ** -0.5)
        x = ut[nk:, cols] + gb_ref[...]
        lf = _log_sigmoid(x)
        cum = jnp.where(fwd, _lane_scan(lf, jnp.add, 0.0, False), _lane_scan(lf, jnp.add, 0.0, True))
        gt_ref[0, ci] = jnp.where(is_cum, cum, x)
        cum_up = pltpu.roll(cum, A_GATES - 2, 0)
        r = x - cum_up
        cmax = jnp.where(fwd, _lane_scan(r, jnp.maximum, -jnp.inf, False), _lane_scan(r, jnp.maximum, -jnp.inf, True))
        b_last = jnp.where(fwd, jnp.broadcast_to(cum_up[:, A_CHUNK - 1:A_CHUNK], cum_up.shape),
                           jnp.broadcast_to(cum_up[:, 0:1], cum_up.shape))
        a = (b_last - cum_up) + x
        g = jnp.broadcast_to(jnp.max(a, axis=-1, keepdims=True), a.shape)
        ct_ref[0, ci] = jnp.where(is_cum, pltpu.roll(b_last, 2, 0), cmax)
        ws_ref[0, ci] = jnp.where(is_cum, pltpu.roll(g, 2, 0), jnp.exp(a - g))


def _project_mlstm(h, mods, layer, gain, w, wt, gb, *, ctx_row=None, tm):
    b, t, d = h.shape
    n = w.shape[1]
    r = mods.shape[1]
    nt = wt.shape[0]
    nk = A_HEADS * A_DQK
    chunk = 512
    cpt = tm // A_CHUNK
    assert t % tm == 0 and n % chunk == 0 and tm % A_CHUNK == 0
    return pl.pallas_call(
        functools.partial(_proj_mlstm_kernel, ctx_row=ctx_row, n_out=n, chunk=chunk),
        grid=(b, t // tm),
        in_specs=[pl.BlockSpec((1, tm, d), lambda bi, i: (bi, i, 0)),
                  pl.BlockSpec((1, r, d), lambda bi, i: (layer, 0, 0)),
                  pl.BlockSpec((1, r, d), lambda bi, i: (layer, 0, 1)),
                  pl.BlockSpec((1, d), lambda bi, i: (0, 0)),
                  _resident((d, n)), _resident((nt, d)),
                  pl.BlockSpec((A_GATES, LANES), lambda bi, i: (0, 0))],
        out_specs=[pl.BlockSpec((1, tm, n), lambda bi, i: (bi, i, 0)),
                   pl.BlockSpec((1, cpt, nk, A_CHUNK), lambda bi, i: (bi, i, 0, 0)),
                   ] + [pl.BlockSpec((1, cpt, A_GATES, A_CHUNK), lambda bi, i: (bi, i, 0, 0))] * 3,
        out_shape=[jax.ShapeDtypeStruct((b, t, n), F32),
                   jax.ShapeDtypeStruct((b, t // A_CHUNK, nk, A_CHUNK), F32),
                   ] + [jax.ShapeDtypeStruct((b, t // A_CHUNK, A_GATES, A_CHUNK), F32)] * 3,
        compiler_params=_params("parallel", "parallel"),
        name="project_mlstm",
    )(h, mods, mods, gain.reshape(1, d), w, wt, gb)


def _mlstm_kernel(*refs, n_ctx, n_lat, need_ctx):
    if need_ctx:
        (ql, vl, ol, ktl, gtl, ctl, wsl, qc, vc, oc, ktc, gtc, ctc, wsc, hn_ref, yl_ref, yc_ref,
         kv_scr, cbd_scr, cst_scr, g_scr, bl_scr, m0_scr, hs_scr) = refs
    else:
        (ql, vl, ol, ktl, gtl, ctl, wsl, qc, vc, ktc, gtc, ctc, wsc, hn_ref, yl_ref,
         kv_scr, cbd_scr, cst_scr, g_scr, bl_scr, m0_scr, hs_scr) = refs
        oc = yc_ref = None
    L = A_CHUNK
    n_all = n_ctx + n_lat
    ones_v = jnp.ones((L, A_DV), BF16)

    def v_ext(v2, hh):
        return jnp.concatenate([v2[:, hh * A_DV:(hh + 1) * A_DV].astype(BF16), ones_v], axis=1)

    def rows(gt, dr, hh):
        return gt[dr * 4 + hh:dr * 4 + hh + 1, :], gt[dr * 4 + 2 + hh:dr * 4 + 3 + hh, :]

    def contrib(c, ct, ws, kt, v2):
        for hh in range(2):
            vx = v_ext(v2, hh)
            kth = kt[hh * A_DQK:(hh + 1) * A_DQK, :]
            for dr in range(2):
                w, g = rows(ws, dr, hh)
                idx = c * 4 + dr * 2 + hh
                kv_scr[idx] = _dot((kth * w).astype(BF16), vx)
                g_scr[pl.ds(idx, 1), :] = g
                bl_scr[pl.ds(idx, 1), :] = rows(ct, dr, hh)[1]

    for c in range(n_ctx):
        contrib(c, ctc[0, c], wsc[0, c], ktc[0, c], vc[0, c * L:(c + 1) * L, :])

    def contrib_lat(c, carry):
        contrib(c + n_ctx, ctl[0, c], wsl[0, c], ktl[0, c], vl[0, pl.ds(pl.multiple_of(c * L, L), L), :])
        return carry

    lax.fori_loop(0, n_lat, contrib_lat, 0, unroll=4)

    cst_scr[...] = jnp.zeros_like(cst_scr)
    zpad = jnp.zeros((A_DQK, 2 * A_DV), BF16)

    def scan_step(i, ms):
        c_bwd = jnp.where(i < n_ctx, n_ctx - 1 - i, n_all - 1 - (i - n_ctx))
        new_ms = []
        for dr, c in ((0, i), (1, c_bwd)):
            c0s = [cst_scr[dr * 2 + hh] for hh in range(2)]
            for hh in range(2):
                idx = c * 4 + dr * 2 + hh
                c0b = c0s[hh].astype(BF16)
                cbd_scr[idx] = jnp.concatenate([zpad, c0b] if hh else [c0b, zpad], axis=0)
                m0 = ms[dr * 2 + hh]
                m0_scr[pl.ds(idx, 1), :] = m0
                g, b_last = g_scr[pl.ds(idx, 1), :], bl_scr[pl.ds(idx, 1), :]
                m_new = jnp.maximum(b_last + m0, g)
                decay = jnp.exp(b_last + m0 - m_new)
                inject = jnp.exp(g - m_new)
                decay, inject = (jnp.concatenate([z, z], axis=1) for z in (decay, inject))
                cst_scr[dr * 2 + hh] = decay * c0s[hh] + inject * kv_scr[idx]
                new_ms.append(m_new)
        return tuple(new_ms)

    lax.fori_loop(0, n_all, scan_step, tuple(jnp.zeros((1, L), F32) for _ in range(4)))

    t_i = lax.broadcasted_iota(jnp.int32, (L, L), 0)
    s_i = lax.broadcasted_iota(jnp.int32, (L, L), 1)
    masks = (s_i <= t_i, s_i >= t_i)
    zk = jnp.zeros((A_DQK, L), BF16)
    zrows = jnp.zeros((L - 2 * SUBLANES, L), F32)

    def mix(c, gt, ct, kt, q2, v2):
        colm = jnp.concatenate([gt, ct, zrows], axis=0).T
        qb = q2.astype(BF16)
        ktb = kt.astype(BF16)
        kt_bd = jnp.concatenate([jnp.concatenate([ktb[:A_DQK], zk], axis=1),
                                 jnp.concatenate([zk, ktb[A_DQK:]], axis=1)], axis=0)
        s2 = _dot(qb, kt_bd)
        vxs = [v_ext(v2, hh) for hh in range(2)]
        hsum = [None, None]
        for dr in range(2):
            for hh in range(2):
                idx = c * 4 + dr * 2 + hh
                li, cum = rows(gt, dr, hh)
                m0 = m0_scr[pl.ds(idx, 1), :]
                cmax = jnp.broadcast_to(colm[:, 8 + dr * 4 + hh:9 + dr * 4 + hh], (L, L))
                cum_t = jnp.broadcast_to(colm[:, dr * 4 + 2 + hh:dr * 4 + 3 + hh], (L, L))
                mm = jnp.maximum(cmax, m0)
                p = jnp.where(masks[dr], jnp.exp((li - cum) - mm), 0.0)
                wq = (p * s2[:, hh * L:(hh + 1) * L]).astype(BF16)
                carry = jnp.exp(m0 - mm)
                lhs = jnp.concatenate([wq, (q2 * carry).astype(BF16)], axis=1)
                ne = _dot(lhs, jnp.concatenate([vxs[hh], cbd_scr[idx]], axis=0))
                h = ne[:, :A_DV] / jnp.maximum(jnp.abs(ne[:, A_DV:]), jnp.exp(-(cum_t + mm)))
                hsum[hh] = h if dr == 0 else hsum[hh] + h
        return hsum

    def finish(hs2, o2):
        ys = []
        for hh in range(2):
            hs = hs2[:, hh * A_DV:(hh + 1) * A_DV]
            hn = hs * lax.rsqrt(jnp.mean(hs * hs, axis=-1, keepdims=True) + NORM_EPS)
            hn = hn * hn_ref[:, hh * A_DV:(hh + 1) * A_DV]
            ys.append(hn * jax.nn.sigmoid(o2[:, hh * A_DV:(hh + 1) * A_DV]))
        return jnp.concatenate(ys, axis=1).astype(BF16)

    if need_ctx:
        for c in range(n_ctx):
            sl = slice(c * L, (c + 1) * L)
            hs = mix(c, gtc[0, c], ctc[0, c], ktc[0, c], qc[0, sl, :], vc[0, sl, :])
            yc_ref[0, sl, :] = finish(jnp.concatenate(hs, axis=1), oc[0, sl, :])

    def mix_lat(c):
        src = pl.ds(c * L if isinstance(c, int) else pl.multiple_of(c * L, L), L)
        hs = mix(c + n_ctx, gtl[0, c], ctl[0, c], ktl[0, c], ql[0, src, :], vl[0, src, :])
        hs_scr[...] = jnp.concatenate(hs, axis=1)

    def finish_lat(c):
        src = pl.ds(c * L if isinstance(c, int) else pl.multiple_of(c * L, L), L)
        yl_ref[0, src, :] = finish(hs_scr[...], ol[0, src, :])

    def pipelined(c, carry):
        finish_lat(c - 1)
        mix_lat(c)
        return carry

    mix_lat(0)
    lax.fori_loop(1, n_lat, pipelined, 0)
    finish_lat(n_lat - 1)


def _mlstm_mix(p, kt, gt, ct, ws, pc, ktc, gtc, ctc, wsc, head_norm, need_ctx):
    b, t, _ = p.shape
    nc = pc.shape[1]
    L = A_CHUNK
    n_lat, n_ctx = t // L, nc // L
    n_all = n_lat + n_ctx
    qw, vw = 2 * A_DQK, 2 * A_DV
    v_blk = A_HEADS * A_DQK // vw
    o_blk = v_blk + A_PAIRS

    def specs(rows, nch, with_o):
        s = [pl.BlockSpec((1, rows, qw), lambda bi, hp: (bi, 0, hp)),
             pl.BlockSpec((1, rows, vw), lambda bi, hp: (bi, 0, v_blk + hp))]
        if with_o:
            s.append(pl.BlockSpec((1, rows, vw), lambda bi, hp: (bi, 0, o_blk + hp)))
        s.append(pl.BlockSpec((1, nch, qw, L), lambda bi, hp: (bi, 0, hp, 0)))
        s += [pl.BlockSpec((1, nch, SUBLANES, L), lambda bi, hp: (bi, 0, hp, 0))] * 3
        return s

    in_specs = specs(t, n_lat, True) + specs(nc, n_ctx, need_ctx)
    in_specs.append(pl.BlockSpec((1, vw), lambda bi, hp: (0, hp)))
    args = [p, p, p, kt, gt, ct, ws] + ([pc, pc, pc] if need_ctx else [pc, pc]) + [ktc, gtc, ctc, wsc, head_norm]
    out_specs = [pl.BlockSpec((1, t, vw), lambda bi, hp: (bi, 0, hp))]
    out_shape = [jax.ShapeDtypeStruct((b, t, A_HEADS * A_DV), BF16)]
    if need_ctx:
        out_specs.append(pl.BlockSpec((1, nc, vw), lambda bi, hp: (bi, 0, hp)))
        out_shape.append(jax.ShapeDtypeStruct((b, nc, A_HEADS * A_DV), BF16))
    n_rows = -(-n_all * 4 // SUBLANES) * SUBLANES
    outs = pl.pallas_call(
        functools.partial(_mlstm_kernel, n_ctx=n_ctx, n_lat=n_lat, need_ctx=need_ctx),
        grid=(b, A_PAIRS),
        in_specs=in_specs,
        out_specs=out_specs,
        out_shape=out_shape,
        scratch_shapes=[pltpu.VMEM((n_all * 4, A_DQK, vw), F32),
                        pltpu.VMEM((n_all * 4, qw, vw), BF16),
                        pltpu.VMEM((4, A_DQK, vw), F32),
                        pltpu.VMEM((n_rows, L), F32), pltpu.VMEM((n_rows, L), F32), pltpu.VMEM((n_rows, L), F32),
                        pltpu.VMEM((L, vw), F32)],
        compiler_params=_params("parallel", "parallel"),
        name="mlstm_mix",
    )(*args)
    return (outs[0], outs[1]) if need_ctx else (outs[0], None)


def _swa_attend(q4, k, v, valid, sink_col):
    s = _dot_nt(q4, k)
    if valid is not None:
        s = jnp.where(valid, s, -jnp.inf)
    m = jnp.maximum(jnp.max(s, axis=-1, keepdims=True), sink_col)
    p = jnp.exp2(s - m)
    den = jnp.sum(p, axis=-1, keepdims=True) + jnp.exp2(sink_col - m)
    return _dot(p.astype(BF16), v) / den


def _swa_heads(q, keys, vals, valid, sink_ref, o_ref, rows, row0=0):
    row_i = lax.broadcasted_iota(jnp.int32, (SWA_GROUP * rows, 1), 0)
    for hk in range(SWA_KV_HEADS):
        ks = slice(hk * SWA_DH, (hk + 1) * SWA_DH)
        k = jnp.concatenate([x[:, ks] for x in keys], axis=0)
        v = jnp.concatenate([x[:, ks] for x in vals], axis=0)
        q4 = jnp.concatenate([q[:, (hk * SWA_GROUP + g) * SWA_DH:(hk * SWA_GROUP + g + 1) * SWA_DH]
                              for g in range(SWA_GROUP)], axis=0)
        sink_col = jnp.zeros((SWA_GROUP * rows, 1), F32)
        for g in range(SWA_GROUP):
            head = hk * SWA_GROUP + g
            sink_col = jnp.where((row_i >= g * rows) & (row_i < (g + 1) * rows),
                                 sink_ref[:, head:head + 1] * LOG2E, sink_col)
        o4 = _swa_attend(q4, k, v, valid, sink_col)
        for g in range(0, SWA_GROUP, 2):
            col = (hk * SWA_GROUP + g) * SWA_DH
            pair = jnp.concatenate([o4[g * rows:(g + 1) * rows], o4[(g + 1) * rows:(g + 2) * rows]], axis=1)
            o_ref[0, row0:row0 + rows, col:col + 2 * SWA_DH] = pair.astype(o_ref.dtype)


def _swa_kernel(*refs, n_tok, nqb, need_ctx):
    if need_ctx:
        q_ref, k_ref, v_ref, kx_ref, vx_ref, sink_ref, qx_ref, o_ref, ox_ref = refs
    else:
        q_ref, k_ref, v_ref, kx_ref, vx_ref, sink_ref, o_ref = refs
    step = pl.program_id(1)
    if need_ctx:
        @pl.when(step == 0)
        def _():
            _swa_heads(qx_ref[0], [kx_ref[0]], [vx_ref[0]], None, sink_ref, ox_ref, qx_ref.shape[1])
    L = SWA_BLOCK
    nc = kx_ref.shape[1]
    qi = lax.broadcasted_iota(jnp.int32, (L, 3 * L), 0)
    ki = lax.broadcasted_iota(jnp.int32, (L, 3 * L), 1)
    ctx_valid = jnp.ones((L, nc), jnp.bool_)
    for jb in range(nqb):
        j = step * nqb + jb
        start = pl.multiple_of(jnp.clip((j - 1) * L, 0, n_tok - 3 * L), L)
        rel = (start - j * L) + ki - qi
        valid = jnp.concatenate([jnp.abs(rel) <= L, ctx_valid], axis=1)
        valid = jnp.concatenate([valid] * SWA_GROUP, axis=0)
        _swa_heads(q_ref[0, jb * L:(jb + 1) * L, :], [k_ref[0, pl.ds(start, 3 * L), :], kx_ref[0]],
                   [v_ref[0, pl.ds(start, 3 * L), :], vx_ref[0]], valid, sink_ref, o_ref, L, row0=jb * L)


def _swa_mix(p, pc, sink, need_ctx):
    b, t, _ = p.shape
    nc = pc.shape[1]
    L = SWA_BLOCK
    nblk = t // L
    nqb = 4 if nblk % 4 == 0 else 1
    assert t >= 3 * L
    qw = SWA_HEADS * SWA_DH
    kvw = SWA_KV_HEADS * SWA_DH
    kblk, vblk = qw // kvw, qw // kvw + 1
    in_specs = [pl.BlockSpec((1, nqb * L, qw), lambda bi, j: (bi, j, 0)),
                pl.BlockSpec((1, t, kvw), lambda bi, j: (bi, 0, kblk)),
                pl.BlockSpec((1, t, kvw), lambda bi, j: (bi, 0, vblk)),
                pl.BlockSpec((1, nc, kvw), lambda bi, j: (bi, 0, kblk)),
                pl.BlockSpec((1, nc, kvw), lambda bi, j: (bi, 0, vblk)),
                pl.BlockSpec((1, SWA_HEADS), lambda bi, j: (0, 0))]
    args = [p, p, p, pc, pc, sink.reshape(1, SWA_HEADS)]
    out_specs = [pl.BlockSpec((1, nqb * L, qw), lambda bi, j: (bi, j, 0))]
    out_shape = [jax.ShapeDtypeStruct((b, t, qw), BF16)]
    if need_ctx:
        in_specs.append(pl.BlockSpec((1, nc, qw), lambda bi, j: (bi, 0, 0)))
        args.append(pc)
        out_specs.append(pl.BlockSpec((1, nc, qw), lambda bi, j: (bi, 0, 0)))
        out_shape.append(jax.ShapeDtypeStruct((b, nc, qw), BF16))
    outs = pl.pallas_call(
        functools.partial(_swa_kernel, n_tok=t, nqb=nqb, need_ctx=need_ctx),
        grid=(b, nblk // nqb),
        in_specs=in_specs,
        out_specs=out_specs,
        out_shape=out_shape,
        compiler_params=_params("parallel", "arbitrary"),
        name="swa_mix",
    )(*args)
    return (outs[0], outs[1]) if need_ctx else (outs[0], None)


def _diff_rows(q, k, v, lam, hn, lam_init):
    lane = lax.broadcasted_iota(jnp.int32, q.shape, 1)
    outs = []
    for m in range(2):
        qm = jnp.where((lane >= DIFF_DH) if m else (lane < DIFF_DH), q, jnp.zeros_like(q))
        s = _dot_nt(qm, k)
        p = jnp.exp2(s - jnp.max(s, axis=-1, keepdims=True)).astype(BF16)
        ne = _dot(p, v)
        outs.append(ne[:, :DIFF_DV] / ne[:, DIFF_DV:])
    od = outs[0] - lam * outs[1]
    od = od * lax.rsqrt(jnp.mean(od * od, axis=-1, keepdims=True) + NORM_EPS)
    return od * hn * (1.0 - lam_init)


def _diff_kernel(*refs, lam_init, need_ctx, sub_rows, n_tiles):
    if need_ctx:
        q_ref, qx_ref, kx_ref, vx_ref, kl_ref, vl_ref, lam_ref, hn_ref, o_ref, ox_ref, k_scr, v_scr = refs
    else:
        q_ref, kx_ref, vx_ref, kl_ref, vl_ref, lam_ref, hn_ref, o_ref, k_scr, v_scr = refs
    nc = kx_ref.shape[1]
    lam = (jnp.exp(jnp.sum(lam_ref[0:1, :] * lam_ref[1:2, :], axis=-1, keepdims=True))
           - jnp.exp(jnp.sum(lam_ref[2:3, :] * lam_ref[3:4, :], axis=-1, keepdims=True)) + lam_init)
    hn = hn_ref[...]

    def first_tile():
        k_scr[0:nc, :] = kx_ref[0]
        v_scr[0:nc, 0:DIFF_DV] = vx_ref[0]
        k_scr[nc:, :] = kl_ref[0]
        v_scr[nc:, 0:DIFF_DV] = vl_ref[0]
        v_scr[:, DIFF_DV:] = jnp.ones((v_scr.shape[0], DIFF_DV), BF16)
        if need_ctx:
            for r0 in range(0, nc, sub_rows):
                rs = slice(r0, min(nc, r0 + sub_rows))
                ox_ref[0, rs, :] = _diff_rows(qx_ref[0, rs, :], k_scr[0:nc, :], v_scr[0:nc, :], lam, hn,
                                              lam_init).astype(ox_ref.dtype)

    if n_tiles == 1:
        first_tile()
    else:
        pl.when(pl.program_id(2) == 0)(first_tile)

    tq = q_ref.shape[1]
    for r0 in range(0, tq, sub_rows):
        rs = slice(r0, min(tq, r0 + sub_rows))
        o_ref[0, rs, :] = _diff_rows(q_ref[0, rs, :], k_scr[...], v_scr[...], lam, hn, lam_init).astype(o_ref.dtype)


def _diff_mix(p, pc, lam, head_norm, lam_init, need_ctx, tq):
    b, t, _ = p.shape
    nc = pc.shape[1]
    w = DIFF_DV
    kblk, vblk = DIFF_HEADS, 2 * DIFF_HEADS
    out_w = DIFF_HEADS * DIFF_DV

    def col_spec(rows, col):
        return pl.BlockSpec((1, rows, w), lambda bi, h, i: (bi, 0, col + h))

    in_specs = [pl.BlockSpec((1, tq, w), lambda bi, h, i: (bi, i, h))]
    args = [p]
    out_specs = [pl.BlockSpec((1, tq, w), lambda bi, h, i: (bi, i, h))]
    out_shape = [jax.ShapeDtypeStruct((b, t, out_w), BF16)]
    if need_ctx:
        in_specs.append(col_spec(nc, 0))
        args.append(pc)
        out_specs.append(col_spec(nc, 0))
        out_shape.append(jax.ShapeDtypeStruct((b, nc, out_w), BF16))
    in_specs += [col_spec(nc, kblk), col_spec(nc, vblk), col_spec(t, kblk), col_spec(t, vblk),
                 pl.BlockSpec((4, DIFF_DH), lambda bi, h, i: (0, 0)),
                 pl.BlockSpec((1, w), lambda bi, h, i: (0, h))]
    args += [pc, pc, p, p, lam, head_norm]
    outs = pl.pallas_call(
        functools.partial(_diff_kernel, lam_init=lam_init, need_ctx=need_ctx, sub_rows=128, n_tiles=t // tq),
        grid=(b, DIFF_HEADS, t // tq),
        in_specs=in_specs,
        out_specs=out_specs,
        out_shape=out_shape,
        scratch_shapes=[pltpu.VMEM((nc + t, w), BF16), pltpu.VMEM((nc + t, 2 * w), BF16)],
        compiler_params=_params("parallel", "parallel", "arbitrary"),
        name="diff_mix",
    )(*args)
    return (outs[0], outs[1]) if need_ctx else (outs[0], None)


def _post_kernel(*refs, ctx_row, sub_rows, ff_chunk, final, n_cast):
    h_ref, y_ref, g2_ref, sh_ref, sc_ref, g5_ref, gain_ref, wo_ref, w1_ref, w2_ref = refs[:10]
    rest = refs[10:]
    if final:
        fn_ref, rest = rest[0], rest[1:]
    cast_in, o_ref, cast_out = rest[:n_cast], rest[n_cast], rest[n_cast + 1:]
    for src, dst in zip(cast_in, cast_out):
        dst[...] = src[...].astype(dst.dtype)
    row = pl.program_id(0) if ctx_row is None else ctx_row
    g2, g5 = _mod_row(g2_ref, row), _mod_row(g5_ref, row)
    shift, scale = _mod_row(sh_ref, row), _mod_row(sc_ref, row)
    tm = h_ref.shape[1]
    ff = w1_ref.shape[1]
    for r0 in range(0, tm, sub_rows):
        rs = slice(r0, r0 + sub_rows)
        h1 = h_ref[0, rs, :] + g2 * _dot(y_ref[0, rs, :], wo_ref[...])
        u = _norm_mod(h1, gain_ref[...], shift, scale).astype(BF16)
        acc = None
        for c0 in range(0, ff, ff_chunk):
            hidden = jnp.square(jnp.maximum(_dot(u, w1_ref[:, c0:c0 + ff_chunk]), 0.0)).astype(BF16)
            part = _dot(hidden, w2_ref[c0:c0 + ff_chunk, :])
            acc = part if acc is None else acc + part
        out = h1 + g5 * acc
        if final:
            out = out * lax.rsqrt(jnp.mean(out * out, axis=-1, keepdims=True) + NORM_EPS) * fn_ref[...]
        o_ref[0, rs, :] = out


def _post(h, y, mods, layer, gain, wo, w1, w2, *, ctx_row=None, tm, final_gain=None, cast_along=()):
    b, t, d = h.shape
    dy = y.shape[2]
    ff = w1.shape[1]
    r = mods.shape[1]
    assert t % tm == 0
    final = final_gain is not None
    n_i = t // tm
    n_steps = b * n_i

    def mod_spec(k):
        return pl.BlockSpec((1, r, d), lambda bi, i: (layer, 0, k))

    in_specs = [pl.BlockSpec((1, tm, d), lambda bi, i: (bi, i, 0)),
                pl.BlockSpec((1, tm, dy), lambda bi, i: (bi, i, 0)),
                mod_spec(2), mod_spec(3), mod_spec(4), mod_spec(5),
                pl.BlockSpec((1, d), lambda bi, i: (0, 0)),
                _resident((dy, d)), _resident((d, ff)), _resident((ff, d))]
    args = [h, y, mods, mods, mods, mods, gain.reshape(1, d), wo, w1, w2]
    if final:
        in_specs.append(pl.BlockSpec((1, d), lambda bi, i: (0, 0)))
        args.append(final_gain.reshape(1, d))
    out_specs = [pl.BlockSpec((1, tm, d), lambda bi, i: (bi, i, 0))]
    out_shape = [jax.ShapeDtypeStruct((b, t, d), F32)]
    for stacked, li in cast_along:
        _, rows, cols = stacked.shape
        assert rows % (n_steps * 2 * SUBLANES) == 0
        in_specs.append(pl.BlockSpec((1, rows // n_steps, cols), lambda bi, i, li=li: (li, bi * n_i + i, 0)))
        args.append(stacked)
        out_specs.append(pl.BlockSpec((1, rows // n_steps, cols), lambda bi, i: (0, bi * n_i + i, 0)))
        out_shape.append(jax.ShapeDtypeStruct((1, rows, cols), BF16))
    outs = pl.pallas_call(
        functools.partial(_post_kernel, ctx_row=ctx_row, sub_rows=min(tm, 512), ff_chunk=1024, final=final,
                          n_cast=len(cast_along)),
        grid=(b, n_i),
        in_specs=in_specs,
        out_specs=out_specs,
        out_shape=out_shape,
        compiler_params=_params("parallel", "parallel"),
        name="post",
    )(*args)
    return outs[0] if not cast_along else (outs[0],) + tuple(o[0] for o in outs[1:])


def _rope_tables(n_tok, head_dim):
    rows = n_tok // GRID_W
    row = jnp.repeat(jnp.arange(rows, dtype=jnp.int32), GRID_W).astype(F32)
    col = jnp.tile(jnp.arange(GRID_W, dtype=jnp.int32), rows).astype(F32)
    quarter = head_dim // 4
    inv = ROPE_BASE ** (-jnp.arange(quarter, dtype=F32) / quarter)
    ang = jnp.concatenate([row[:, None] * inv, col[:, None] * inv], axis=-1)
    cos, sin = jnp.cos(ang), jnp.sin(ang)
    return jnp.tile(cos, (1, 4)), jnp.tile(jnp.concatenate([-sin, sin], axis=-1), (1, 2))


def _mlstm_weights(w_in, gate_b):
    d = w_in.shape[0]
    nk = A_HEADS * A_DQK
    main = 2 * nk + 2 * A_HEADS * A_DV
    w = jnp.concatenate([w_in[:, :nk], w_in[:, 2 * nk:main]], axis=1).astype(BF16)
    wg = jnp.transpose(w_in[:, main:].reshape(d, 4, A_PAIRS, 2), (0, 2, 1, 3)).reshape(d, A_GATES)
    wt = jnp.concatenate([w_in[:, nk:2 * nk], wg], axis=1).T.astype(BF16)
    gb = jnp.transpose(gate_b.astype(F32).reshape(4, A_PAIRS, 2), (1, 0, 2)).reshape(A_GATES, 1)
    return w, wt, jnp.broadcast_to(gb, (A_GATES, LANES))


def kernel(x, c, ctx, c_ctx, ada_w, ada_b, norm_mix, norm_ffn, ffn_w1, ffn_w2, mlstm_w_in, mlstm_gate_b, mlstm_head_norm, mlstm_w_out, swa_w_in, swa_sink, swa_w_out, diff_w_in, diff_lambda_q1, diff_lambda_k1, diff_lambda_q2, diff_lambda_k2, diff_head_norm, diff_w_out, final_norm):
    bsz, n_tok, d = x.shape
    n_ctx = ctx.shape[1]
    depth = ada_w.shape[0]
    rows = -(-(bsz + 1) // SUBLANES) * SUBLANES
    cond = jnp.concatenate([c, c_ctx[None, :], jnp.zeros((rows - bsz - 1, d), F32)], axis=0)
    mods = _ada_table(cond, ada_w, ada_b)
    rope = _rope_tables(n_tok, SWA_DH)
    tm_lat, tm_post = min(n_tok, 1024), min(n_tok, 1024)

    h, hc = x, ctx
    w1, w2 = ffn_w1[0].astype(BF16), ffn_w2[0].astype(BF16)
    for i in range(depth):
        kind, slot = i % N_MIXERS, i // N_MIXERS
        need_ctx = i < depth - 1
        if kind == 0:
            w, wt, gb = _mlstm_weights(mlstm_w_in[slot], mlstm_gate_b[slot])
            proj = functools.partial(_project_mlstm, mods=mods, layer=i, gain=norm_mix[i], w=w, wt=wt, gb=gb)
            lat = proj(h, tm=tm_lat)
            cx = proj(hc, tm=n_ctx, ctx_row=bsz)
            y, yc = _mlstm_mix(*lat, *cx, mlstm_head_norm[slot].reshape(1, -1), need_ctx)
            wo = mlstm_w_out[slot]
        else:
            proj = functools.partial(_project, mods=mods, layer=i, gain=norm_mix[i])
            if kind == 1:
                w = swa_w_in[slot].astype(BF16)
                rc = (SWA_HEADS + SWA_KV_HEADS) * SWA_DH
                qs = (SWA_HEADS * SWA_DH, SWA_DH ** -0.5 * LOG2E)
            else:
                w = diff_w_in[slot].astype(BF16)
                rc = 4 * DIFF_HEADS * DIFF_DH
                qs = (2 * DIFF_HEADS * DIFF_DH, DIFF_DH ** -0.5 * LOG2E)
            p = proj(h, w=w, tm=tm_lat, rope=rope, rope_cols=rc, qscale=qs)
            pc = proj(hc, w=w, tm=n_ctx, ctx_row=bsz, qscale=qs)
            if kind == 1:
                y, yc = _swa_mix(p, pc, swa_sink[slot], need_ctx)
                wo = swa_w_out[slot]
            else:
                lam = jnp.stack([diff_lambda_q1[slot], diff_lambda_k1[slot], diff_lambda_q2[slot], diff_lambda_k2[slot]])
                lam_init = 0.8 - 0.6 * math.exp(-0.3 * i)
                y, yc = _diff_mix(p, pc, lam.astype(F32), diff_head_norm[slot].reshape(1, -1), lam_init, need_ctx, tq=min(n_tok, 2048))
                wo = diff_w_out[slot]
        post = functools.partial(_post, mods=mods, layer=i, gain=norm_ffn[i], wo=wo.astype(BF16), w1=w1, w2=w2)
        if i + 1 < depth:
            h, w1, w2 = post(h, y, tm=tm_post, cast_along=((ffn_w1, i + 1), (ffn_w2, i + 1)))
        else:
            h = post(h, y, tm=tm_post, final_gain=final_norm)
        if need_ctx:
            hc = post(hc, yc, tm=n_ctx, ctx_row=bsz)
    return h
```

```python
import functools
import math

import jax
import jax.numpy as jnp
from jax import lax
from jax.experimental import pallas as pl
from jax.experimental.pallas import tpu as pltpu

F32 = jnp.float32
BF16 = jnp.bfloat16

LANES = 128
SUBLANES = 8
VMEM_LIMIT_BYTES = 56 * 1024 * 1024
LOG2E = math.log2(math.e)

NORM_EPS = 1e-6
ROPE_BASE = 10000.0
GRID_W = 64
N_MIXERS = 3

A_HEADS = 8
A_DQK = 64
A_DV = 128
A_CHUNK = 128
A_PAIRS = A_HEADS // 2
A_GATES = 4 * A_HEADS

SWA_HEADS = 16
SWA_KV_HEADS = 4
SWA_DH = 64
SWA_GROUP = SWA_HEADS // SWA_KV_HEADS
SWA_BLOCK = 128

DIFF_HEADS = 8
DIFF_DH = 64
DIFF_DV = 128


def _params(*sem, flags=None):
    return pltpu.CompilerParams(dimension_semantics=sem, vmem_limit_bytes=VMEM_LIMIT_BYTES, flags=flags)


def _dot(a, b):
    return jnp.dot(a, b, preferred_element_type=F32)


def _dot_nt(a, b):
    return lax.dot_general(a, b, (((1,), (1,)), ((), ())), preferred_element_type=F32)


def _norm_mod(x, gain, shift, scale):
    y = x * lax.rsqrt(jnp.mean(x * x, axis=-1, keepdims=True) + NORM_EPS) * gain
    return y * (1.0 + scale) + shift


def _mod_row(ref, row):
    return ref[0, pl.ds(row, 1), :]


def _resident(shape):
    return pl.BlockSpec(shape, lambda *_: (0,) * len(shape), pipeline_mode=pl.Buffered(1))


def _ada_kernel(c_ref, w_ref, b_ref, o_ref):
    c = c_ref[...]
    s = (c * jax.nn.sigmoid(c)).astype(BF16)
    o_ref[0] = _dot(s, w_ref[0].astype(BF16)) + b_ref[0]


def _ada_table(cond, ada_w, ada_b):
    depth, d, n = ada_w.shape
    r = cond.shape[0]
    tn = n // 4
    return pl.pallas_call(
        _ada_kernel,
        grid=(depth, n // tn),
        in_specs=[pl.BlockSpec((r, d), lambda i, j: (0, 0)),
                  pl.BlockSpec((1, d, tn), lambda i, j: (i, 0, j)),
                  pl.BlockSpec((1, 1, tn), lambda i, j: (i, 0, j))],
        out_specs=pl.BlockSpec((1, r, tn), lambda i, j: (i, 0, j)),
        out_shape=jax.ShapeDtypeStruct((depth, r, n), F32),
        compiler_params=_params("parallel", "parallel"),
        name="ada_table",
    )(cond, ada_w, ada_b.reshape(depth, 1, n))


def _rope_block(blk, cos, sin_signed):
    lane = lax.broadcasted_iota(jnp.int32, blk.shape, 1)
    first_half = (lane & 32) == 0
    partner = jnp.where(first_half, pltpu.roll(blk, LANES - 32, 1), pltpu.roll(blk, 32, 1))
    return blk * cos + partner * sin_signed


def _proj_kernel(*refs, ctx_row, n_out, rope_cols, qscale, chunk):
    if rope_cols:
        x_ref, sh_ref, sc_ref, g_ref, w_ref, cos_ref, sin_ref, o_ref = refs
    else:
        x_ref, sh_ref, sc_ref, g_ref, w_ref, o_ref = refs
    row = pl.program_id(0) if ctx_row is None else ctx_row
    u = _norm_mod(x_ref[0], g_ref[...], _mod_row(sh_ref, row), _mod_row(sc_ref, row)).astype(BF16)
    for c0 in range(0, n_out, chunk):
        acc = _dot(u, w_ref[:, c0:c0 + chunk])
        for l0 in range(0, chunk, LANES):
            col = c0 + l0
            blk = acc[:, l0:l0 + LANES]
            if col < rope_cols:
                blk = _rope_block(blk, cos_ref[...], sin_ref[...])
            if col < qscale[0]:
                blk = blk * qscale[1]
            o_ref[0, :, col:col + LANES] = blk.astype(o_ref.dtype)


def _project(h, mods, layer, gain, w, *, ctx_row=None, tm, rope=None, rope_cols=0, qscale):
    b, t, d = h.shape
    n = w.shape[1]
    r = mods.shape[1]
    chunk = 512
    assert t % tm == 0 and n % chunk == 0
    in_specs = [pl.BlockSpec((1, tm, d), lambda bi, i: (bi, i, 0)),
                pl.BlockSpec((1, r, d), lambda bi, i: (layer, 0, 0)),
                pl.BlockSpec((1, r, d), lambda bi, i: (layer, 0, 1)),
                pl.BlockSpec((1, d), lambda bi, i: (0, 0)),
                _resident((d, n))]
    args = [h, mods, mods, gain.reshape(1, d), w]
    if rope_cols:
        in_specs += [pl.BlockSpec((tm, LANES), lambda bi, i: (i, 0))] * 2
        args += list(rope)
    return pl.pallas_call(
        functools.partial(_proj_kernel, ctx_row=ctx_row, n_out=n, rope_cols=rope_cols, qscale=qscale, chunk=chunk),
        grid=(b, t // tm),
        in_specs=in_specs,
        out_specs=pl.BlockSpec((1, tm, n), lambda bi, i: (bi, i, 0)),
        out_shape=jax.ShapeDtypeStruct((b, t, n), BF16),
        compiler_params=_params("parallel", "parallel"),
        name="project",
    )(*args)


def _log_sigmoid(x):
    return jnp.minimum(x, 0.0) - jnp.log1p(jnp.exp(-jnp.abs(x)))


def _lane_scan(x, op, fill, reverse):
    lane = lax.broadcasted_iota(jnp.int32, x.shape, 1)
    k = 1
    while k < LANES:
        if reverse:
            shifted, ok = pltpu.roll(x, LANES - k, 1), lane < LANES - k
        else:
            shifted, ok = pltpu.roll(x, k, 1), lane >= k
        x = op(x, jnp.where(ok, shifted, fill))
        k *= 2
    return x


def _proj_mlstm_kernel(x_ref, sh_ref, sc_ref, g_ref, w_ref, wt_ref, gb_ref, o_ref, kt_ref, gt_ref, ct_ref, ws_ref,
                       *, ctx_row, n_out, chunk):
    row = pl.program_id(0) if ctx_row is None else ctx_row
    u = _norm_mod(x_ref[0], g_ref[...], _mod_row(sh_ref, row), _mod_row(sc_ref, row)).astype(BF16)
    ut = _dot_nt(wt_ref[...], u)
    for c0 in range(0, n_out, chunk):
        o_ref[0, :, c0:c0 + chunk] = _dot(u, w_ref[:, c0:c0 + chunk])
    nk = A_HEADS * A_DQK
    row8 = lax.broadcasted_iota(jnp.int32, (A_GATES, LANES), 0) & 7
    fwd = row8 < 4
    is_cum = (row8 & 2) != 0
    for ci in range(u.shape[0] // A_CHUNK):
        cols = slice(ci * A_CHUNK, (ci + 1) * A_CHUNK)
        kt_ref[0, ci] = ut[:nk, cols] * (A_DQK ** -0.5)
        x = ut[nk:, cols] + gb_ref[...]
        lf = _log_sigmoid(x)
        cum = jnp.where(fwd, _lane_scan(lf, jnp.add, 0.0, False), _lane_scan(lf, jnp.add, 0.0, True))
        gt_ref[0, ci] = jnp.where(is_cum, cum, x)
        cum_up = pltpu.roll(cum, A_GATES - 2, 0)
        r = x - cum_up
        cmax = jnp.where(fwd, _lane_scan(r, jnp.maximum, -jnp.inf, False), _lane_scan(r, jnp.maximum, -jnp.inf, True))
        b_last = jnp.where(fwd, jnp.broadcast_to(cum_up[:, A_CHUNK - 1:A_CHUNK], cum_up.shape),
                           jnp.broadcast_to(cum_up[:, 0:1], cum_up.shape))
        a = (b_last - cum_up) + x
        g = jnp.broadcast_to(jnp.max(a, axis=-1, keepdims=True), a.shape)
        ct_ref[0, ci] = jnp.where(is_cum, pltpu.roll(b_last, 2, 0), cmax)
        ws_ref[0, ci] = jnp.where(is_cum, pltpu.roll(g, 2, 0), jnp.exp(a - g))


def _project_mlstm(h, mods, layer, gain, w, wt, gb, *, ctx_row=None, tm):
    b, t, d = h.shape
    n = w.shape[1]
    r = mods.shape[1]
    nt = wt.shape[0]
    nk = A_HEADS * A_DQK
    chunk = 512
    cpt = tm // A_CHUNK
    assert t % tm == 0 and n % chunk == 0 and tm % A_CHUNK == 0
    return pl.pallas_call(
        functools.partial(_proj_mlstm_kernel, ctx_row=ctx_row, n_out=n, chunk=chunk),
        grid=(b, t // tm),
        in_specs=[pl.BlockSpec((1, tm, d), lambda bi, i: (bi, i, 0)),
                  pl.BlockSpec((1, r, d), lambda bi, i: (layer, 0, 0)),
                  pl.BlockSpec((1, r, d), lambda bi, i: (layer, 0, 1)),
                  pl.BlockSpec((1, d), lambda bi, i: (0, 0)),
                  _resident((d, n)), _resident((nt, d)),
                  pl.BlockSpec((A_GATES, LANES), lambda bi, i: (0, 0))],
        out_specs=[pl.BlockSpec((1, tm, n), lambda bi, i: (bi, i, 0)),
                   pl.BlockSpec((1, cpt, nk, A_CHUNK), lambda bi, i: (bi, i, 0, 0)),
                   ] + [pl.BlockSpec((1, cpt, A_GATES, A_CHUNK), lambda bi, i: (bi, i, 0, 0))] * 3,
        out_shape=[jax.ShapeDtypeStruct((b, t, n), F32),
                   jax.ShapeDtypeStruct((b, t // A_CHUNK, nk, A_CHUNK), F32),
                   ] + [jax.ShapeDtypeStruct((b, t // A_CHUNK, A_GATES, A_CHUNK), F32)] * 3,
        compiler_params=_params("parallel", "parallel"),
        name="project_mlstm",
    )(h, mods, mods, gain.reshape(1, d), w, wt, gb)


def _mlstm_kernel(*refs, n_ctx, n_lat, need_ctx):
    if need_ctx:
        (ql, vl, ol, ktl, gtl, ctl, wsl, qc, vc, oc, ktc, gtc, ctc, wsc, hn_ref, yl_ref, yc_ref,
         kv_scr, cbd_scr, cst_scr, g_scr, bl_scr, m0_scr, hs_scr) = refs
    else:
        (ql, vl, ol, ktl, gtl, ctl, wsl, qc, vc, ktc, gtc, ctc, wsc, hn_ref, yl_ref,
         kv_scr, cbd_scr, cst_scr, g_scr, bl_scr, m0_scr, hs_scr) = refs
        oc = yc_ref = None
    L = A_CHUNK
    n_all = n_ctx + n_lat
    ones_v = jnp.ones((L, A_DV), BF16)

    def v_ext(v2, hh):
        return jnp.concatenate([v2[:, hh * A_DV:(hh + 1) * A_DV].astype(BF16), ones_v], axis=1)

    def rows(gt, dr, hh):
        return gt[dr * 4 + hh:dr * 4 + hh + 1, :], gt[dr * 4 + 2 + hh:dr * 4 + 3 + hh, :]

    def contrib(c, ct, ws, kt, v2):
        for hh in range(2):
            vx = v_ext(v2, hh)
            kth = kt[hh * A_DQK:(hh + 1) * A_DQK, :]
            for dr in range(2):
                w, g = rows(ws, dr, hh)
                idx = c * 4 + dr * 2 + hh
                kv_scr[idx] = _dot((kth * w).astype(BF16), vx)
                g_scr[pl.ds(idx, 1), :] = g
                bl_scr[pl.ds(idx, 1), :] = rows(ct, dr, hh)[1]

    for c in range(n_ctx):
        contrib(c, ctc[0, c], wsc[0, c], ktc[0, c], vc[0, c * L:(c + 1) * L, :])

    def contrib_lat(c, carry):
        contrib(c + n_ctx, ctl[0, c], wsl[0, c], ktl[0, c], vl[0, pl.ds(pl.multiple_of(c * L, L), L), :])
        return carry

    lax.fori_loop(0, n_lat, contrib_lat, 0, unroll=16)

    cst_scr[...] = jnp.zeros_like(cst_scr)
    zpad = jnp.zeros((A_DQK, 2 * A_DV), BF16)

    def scan_step(i, ms):
        c_bwd = jnp.where(i < n_ctx, n_ctx - 1 - i, n_all - 1 - (i - n_ctx))
        new_ms = []
        for dr, c in ((0, i), (1, c_bwd)):
            c0s = [cst_scr[dr * 2 + hh] for hh in range(2)]
            for hh in range(2):
                idx = c * 4 + dr * 2 + hh
                c0b = c0s[hh].astype(BF16)
                cbd_scr[idx] = jnp.concatenate([zpad, c0b] if hh else [c0b, zpad], axis=0)
                m0 = ms[dr * 2 + hh]
                m0_scr[pl.ds(idx, 1), :] = m0
                g, b_last = g_scr[pl.ds(idx, 1), :], bl_scr[pl.ds(idx, 1), :]
                m_new = jnp.maximum(b_last + m0, g)
                decay = jnp.exp(b_last + m0 - m_new)
                inject = jnp.exp(g - m_new)
                decay, inject = (jnp.concatenate([z, z], axis=1) for z in (decay, inject))
                cst_scr[dr * 2 + hh] = decay * c0s[hh] + inject * kv_scr[idx]
                new_ms.append(m_new)
        return tuple(new_ms)

    lax.fori_loop(0, n_all, scan_step, tuple(jnp.zeros((1, L), F32) for _ in range(4)), unroll=True)

    t_i = lax.broadcasted_iota(jnp.int32, (L, L), 0)
    s_i = lax.broadcasted_iota(jnp.int32, (L, L), 1)
    masks = (s_i <= t_i, s_i >= t_i)
    zk = jnp.zeros((A_DQK, L), BF16)
    zrows = jnp.zeros((L - 2 * SUBLANES, L), F32)

    def mix(c, gt, ct, kt, q2, v2):
        colm = jnp.concatenate([gt, ct, zrows], axis=0).T
        qb = q2.astype(BF16)
        ktb = kt.astype(BF16)
        kt_bd = jnp.concatenate([jnp.concatenate([ktb[:A_DQK], zk], axis=1),
                                 jnp.concatenate([zk, ktb[A_DQK:]], axis=1)], axis=0)
        s2 = _dot(qb, kt_bd)
        vxs = [v_ext(v2, hh) for hh in range(2)]
        hsum = [None, None]
        for dr in range(2):
            for hh in range(2):
                idx = c * 4 + dr * 2 + hh
                li, cum = rows(gt, dr, hh)
                m0 = m0_scr[pl.ds(idx, 1), :]
                cmax = jnp.broadcast_to(colm[:, 8 + dr * 4 + hh:9 + dr * 4 + hh], (L, L))
                cum_t = jnp.broadcast_to(colm[:, dr * 4 + 2 + hh:dr * 4 + 3 + hh], (L, L))
                mm = jnp.maximum(cmax, m0)
                p = jnp.where(masks[dr], jnp.exp((li - cum) - mm), 0.0)
                wq = (p * s2[:, hh * L:(hh + 1) * L]).astype(BF16)
                carry = jnp.exp(m0 - mm)
                lhs = jnp.concatenate([wq, (q2 * carry).astype(BF16)], axis=1)
                ne = _dot(lhs, jnp.concatenate([vxs[hh], cbd_scr[idx]], axis=0))
                h = ne[:, :A_DV] / jnp.maximum(jnp.abs(ne[:, A_DV:]), jnp.exp(-(cum_t + mm)))
                hsum[hh] = h if dr == 0 else hsum[hh] + h
        return hsum

    def finish(hs2, o2):
        ys = []
        for hh in range(2):
            hs = hs2[:, hh * A_DV:(hh + 1) * A_DV]
            hn = hs * lax.rsqrt(jnp.mean(hs * hs, axis=-1, keepdims=True) + NORM_EPS)
            hn = hn * hn_ref[:, hh * A_DV:(hh + 1) * A_DV]
            ys.append(hn * jax.nn.sigmoid(o2[:, hh * A_DV:(hh + 1) * A_DV]))
        return jnp.concatenate(ys, axis=1).astype(BF16)

    if need_ctx:
        for c in range(n_ctx):
            sl = slice(c * L, (c + 1) * L)
            hs = mix(c, gtc[0, c], ctc[0, c], ktc[0, c], qc[0, sl, :], vc[0, sl, :])
            yc_ref[0, sl, :] = finish(jnp.concatenate(hs, axis=1), oc[0, sl, :])

    def mix_lat(c):
        src = pl.ds(c * L if isinstance(c, int) else pl.multiple_of(c * L, L), L)
        hs = mix(c + n_ctx, gtl[0, c], ctl[0, c], ktl[0, c], ql[0, src, :], vl[0, src, :])
        hs_scr[...] = jnp.concatenate(hs, axis=1)

    def finish_lat(c):
        src = pl.ds(c * L if isinstance(c, int) else pl.multiple_of(c * L, L), L)
        yl_ref[0, src, :] = finish(hs_scr[...], ol[0, src, :])

    def pipelined(c, carry):
        finish_lat(c - 1)
        mix_lat(c)
        return carry

    mix_lat(0)
    lax.fori_loop(1, n_lat, pipelined, 0, unroll=3)
    finish_lat(n_lat - 1)


def _mlstm_mix(p, kt, gt, ct, ws, pc, ktc, gtc, ctc, wsc, head_norm, need_ctx):
    b, t, _ = p.shape
    nc = pc.shape[1]
    L = A_CHUNK
    n_lat, n_ctx = t // L, nc // L
    n_all = n_lat + n_ctx
    qw, vw = 2 * A_DQK, 2 * A_DV
    v_blk = A_HEADS * A_DQK // vw
    o_blk = v_blk + A_PAIRS

    def specs(rows, nch, with_o):
        s = [pl.BlockSpec((1, rows, qw), lambda bi, hp: (bi, 0, hp)),
             pl.BlockSpec((1, rows, vw), lambda bi, hp: (bi, 0, v_blk + hp))]
        if with_o:
            s.append(pl.BlockSpec((1, rows, vw), lambda bi, hp: (bi, 0, o_blk + hp)))
        s.append(pl.BlockSpec((1, nch, qw, L), lambda bi, hp: (bi, 0, hp, 0)))
        s += [pl.BlockSpec((1, nch, SUBLANES, L), lambda bi, hp: (bi, 0, hp, 0))] * 3
        return s

    in_specs = specs(t, n_lat, True) + specs(nc, n_ctx, need_ctx)
    in_specs.append(pl.BlockSpec((1, vw), lambda bi, hp: (0, hp)))
    args = [p, p, p, kt, gt, ct, ws] + ([pc, pc, pc] if need_ctx else [pc, pc]) + [ktc, gtc, ctc, wsc, head_norm]
    out_specs = [pl.BlockSpec((1, t, vw), lambda bi, hp: (bi, 0, hp))]
    out_shape = [jax.ShapeDtypeStruct((b, t, A_HEADS * A_DV), BF16)]
    if need_ctx:
        out_specs.append(pl.BlockSpec((1, nc, vw), lambda bi, hp: (bi, 0, hp)))
        out_shape.append(jax.ShapeDtypeStruct((b, nc, A_HEADS * A_DV), BF16))
    n_rows = -(-n_all * 4 // SUBLANES) * SUBLANES
    outs = pl.pallas_call(
        functools.partial(_mlstm_kernel, n_ctx=n_ctx, n_lat=n_lat, need_ctx=need_ctx),
        grid=(b, A_PAIRS),
        in_specs=in_specs,
        out_specs=out_specs,
        out_shape=out_shape,
        scratch_shapes=[pltpu.VMEM((n_all * 4, A_DQK, vw), F32),
                        pltpu.VMEM((n_all * 4, qw, vw), BF16),
                        pltpu.VMEM((4, A_DQK, vw), F32),
                        pltpu.VMEM((n_rows, L), F32), pltpu.VMEM((n_rows, L), F32), pltpu.VMEM((n_rows, L), F32),
                        pltpu.VMEM((L, vw), F32)],
        compiler_params=_params("parallel", "parallel"),
        name="mlstm_mix",
    )(*args)
    return (outs[0], outs[1]) if need_ctx else (outs[0], None)


def _swa_attend(q4, k, v, valid, sink_col):
    s = _dot_nt(q4, k)
    if valid is not None:
        s = jnp.where(valid, s, -jnp.inf)
    m = jnp.maximum(jnp.max(s, axis=-1, keepdims=True), sink_col)
    p = jnp.exp2(s - m)
    den = jnp.sum(p, axis=-1, keepdims=True) + jnp.exp2(sink_col - m)
    return _dot(p.astype(BF16), v) / den


def _swa_heads(q, keys, vals, valid, sink_ref, o_ref, rows, row0=0):
    row_i = lax.broadcasted_iota(jnp.int32, (SWA_GROUP * rows, 1), 0)
    for hk in range(SWA_KV_HEADS):
        ks = slice(hk * SWA_DH, (hk + 1) * SWA_DH)
        k = jnp.concatenate([x[:, ks] for x in keys], axis=0)
        v = jnp.concatenate([x[:, ks] for x in vals], axis=0)
        q4 = jnp.concatenate([q[:, (hk * SWA_GROUP + g) * SWA_DH:(hk * SWA_GROUP + g + 1) * SWA_DH]
                              for g in range(SWA_GROUP)], axis=0)
        sink_col = jnp.zeros((SWA_GROUP * rows, 1), F32)
        for g in range(SWA_GROUP):
            head = hk * SWA_GROUP + g
            sink_col = jnp.where((row_i >= g * rows) & (row_i < (g + 1) * rows),
                                 sink_ref[:, head:head + 1] * LOG2E, sink_col)
        o4 = _swa_attend(q4, k, v, valid, sink_col)
        for g in range(0, SWA_GROUP, 2):
            col = (hk * SWA_GROUP + g) * SWA_DH
            pair = jnp.concatenate([o4[g * rows:(g + 1) * rows], o4[(g + 1) * rows:(g + 2) * rows]], axis=1)
            o_ref[0, row0:row0 + rows, col:col + 2 * SWA_DH] = pair.astype(o_ref.dtype)


def _swa_kernel(*refs, n_tok, nqb, need_ctx):
    if need_ctx:
        q_ref, k_ref, v_ref, kx_ref, vx_ref, sink_ref, qx_ref, o_ref, ox_ref = refs
    else:
        q_ref, k_ref, v_ref, kx_ref, vx_ref, sink_ref, o_ref = refs
    step = pl.program_id(1)
    if need_ctx:
        @pl.when(step == 0)
        def _():
            _swa_heads(qx_ref[0], [kx_ref[0]], [vx_ref[0]], None, sink_ref, ox_ref, qx_ref.shape[1])
    L = SWA_BLOCK
    nc = kx_ref.shape[1]
    qi = lax.broadcasted_iota(jnp.int32, (L, 3 * L), 0)
    ki = lax.broadcasted_iota(jnp.int32, (L, 3 * L), 1)
    ctx_valid = jnp.ones((L, nc), jnp.bool_)
    for jb in range(nqb):
        j = step * nqb + jb
        start = pl.multiple_of(jnp.clip((j - 1) * L, 0, n_tok - 3 * L), L)
        rel = (start - j * L) + ki - qi
        valid = jnp.concatenate([jnp.abs(rel) <= L, ctx_valid], axis=1)
        valid = jnp.concatenate([valid] * SWA_GROUP, axis=0)
        _swa_heads(q_ref[0, jb * L:(jb + 1) * L, :], [k_ref[0, pl.ds(start, 3 * L), :], kx_ref[0]],
                   [v_ref[0, pl.ds(start, 3 * L), :], vx_ref[0]], valid, sink_ref, o_ref, L, row0=jb * L)


def _swa_mix(p, pc, sink, need_ctx):
    b, t, _ = p.shape
    nc = pc.shape[1]
    L = SWA_BLOCK
    nblk = t // L
    nqb = 2 if nblk % 2 == 0 else 1
    assert t >= 3 * L
    qw = SWA_HEADS * SWA_DH
    kvw = SWA_KV_HEADS * SWA_DH
    kblk, vblk = qw // kvw, qw // kvw + 1
    in_specs = [pl.BlockSpec((1, nqb * L, qw), lambda bi, j: (bi, j, 0)),
                pl.BlockSpec((1, t, kvw), lambda bi, j: (bi, 0, kblk)),
                pl.BlockSpec((1, t, kvw), lambda bi, j: (bi, 0, vblk)),
                pl.BlockSpec((1, nc, kvw), lambda bi, j: (bi, 0, kblk)),
                pl.BlockSpec((1, nc, kvw), lambda bi, j: (bi, 0, vblk)),
                pl.BlockSpec((1, SWA_HEADS), lambda bi, j: (0, 0))]
    args = [p, p, p, pc, pc, sink.reshape(1, SWA_HEADS)]
    out_specs = [pl.BlockSpec((1, nqb * L, qw), lambda bi, j: (bi, j, 0))]
    out_shape = [jax.ShapeDtypeStruct((b, t, qw), BF16)]
    if need_ctx:
        in_specs.append(pl.BlockSpec((1, nc, qw), lambda bi, j: (bi, 0, 0)))
        args.append(pc)
        out_specs.append(pl.BlockSpec((1, nc, qw), lambda bi, j: (bi, 0, 0)))
        out_shape.append(jax.ShapeDtypeStruct((b, nc, qw), BF16))
    outs = pl.pallas_call(
        functools.partial(_swa_kernel, n_tok=t, nqb=nqb, need_ctx=need_ctx),
        grid=(b, nblk // nqb),
        in_specs=in_specs,
        out_specs=out_specs,
        out_shape=out_shape,
        compiler_params=_params("parallel", "arbitrary"),
        name="swa_mix",
    )(*args)
    return (outs[0], outs[1]) if need_ctx else (outs[0], None)


def _diff_rows(q, k, v, lam, hn, lam_init):
    lane = lax.broadcasted_iota(jnp.int32, q.shape, 1)
    outs = []
    for m in range(2):
        qm = jnp.where((lane >= DIFF_DH) if m else (lane < DIFF_DH), q, jnp.zeros_like(q))
        s = _dot_nt(qm, k)
        p = jnp.exp2(s - jnp.max(s, axis=-1, keepdims=True)).astype(BF16)
        ne = _dot(p, v)
        outs.append(ne[:, :DIFF_DV] / ne[:, DIFF_DV:])
    od = outs[0] - lam * outs[1]
    od = od * lax.rsqrt(jnp.mean(od * od, axis=-1, keepdims=True) + NORM_EPS)
    return od * hn * (1.0 - lam_init)


def _diff_kernel(*refs, lam_init, need_ctx, sub_rows, n_tiles):
    if need_ctx:
        q_ref, qx_ref, kx_ref, vx_ref, kl_ref, vl_ref, lam_ref, hn_ref, o_ref, ox_ref, k_scr, v_scr = refs
    else:
        q_ref, kx_ref, vx_ref, kl_ref, vl_ref, lam_ref, hn_ref, o_ref, k_scr, v_scr = refs
    nc = kx_ref.shape[1]
    lam = (jnp.exp(jnp.sum(lam_ref[0:1, :] * lam_ref[1:2, :], axis=-1, keepdims=True))
           - jnp.exp(jnp.sum(lam_ref[2:3, :] * lam_ref[3:4, :], axis=-1, keepdims=True)) + lam_init)
    hn = hn_ref[...]

    def first_tile():
        k_scr[0:nc, :] = kx_ref[0]
        v_scr[0:nc, 0:DIFF_DV] = vx_ref[0]
        k_scr[nc:, :] = kl_ref[0]
        v_scr[nc:, 0:DIFF_DV] = vl_ref[0]
        v_scr[:, DIFF_DV:] = jnp.ones((v_scr.shape[0], DIFF_DV), BF16)
        if need_ctx:
            for r0 in range(0, nc, sub_rows):
                rs = slice(r0, min(nc, r0 + sub_rows))
                ox_ref[0, rs, :] = _diff_rows(qx_ref[0, rs, :], k_scr[0:nc, :], v_scr[0:nc, :], lam, hn,
                                              lam_init).astype(ox_ref.dtype)

    if n_tiles == 1:
        first_tile()
    else:
        pl.when(pl.program_id(2) == 0)(first_tile)

    tq = q_ref.shape[1]
    for r0 in range(0, tq, sub_rows):
        rs = slice(r0, min(tq, r0 + sub_rows))
        o_ref[0, rs, :] = _diff_rows(q_ref[0, rs, :], k_scr[...], v_scr[...], lam, hn, lam_init).astype(o_ref.dtype)


def _diff_mix(p, pc, lam, head_norm, lam_init, need_ctx, tq):
    b, t, _ = p.shape
    nc = pc.shape[1]
    w = DIFF_DV
    kblk, vblk = DIFF_HEADS, 2 * DIFF_HEADS
    out_w = DIFF_HEADS * DIFF_DV

    def col_spec(rows, col):
        return pl.BlockSpec((1, rows, w), lambda bi, h, i: (bi, 0, col + h))

    in_specs = [pl.BlockSpec((1, tq, w), lambda bi, h, i: (bi, i, h))]
    args = [p]
    out_specs = [pl.BlockSpec((1, tq, w), lambda bi, h, i: (bi, i, h))]
    out_shape = [jax.ShapeDtypeStruct((b, t, out_w), BF16)]
    if need_ctx:
        in_specs.append(col_spec(nc, 0))
        args.append(pc)
        out_specs.append(col_spec(nc, 0))
        out_shape.append(jax.ShapeDtypeStruct((b, nc, out_w), BF16))
    in_specs += [col_spec(nc, kblk), col_spec(nc, vblk), col_spec(t, kblk), col_spec(t, vblk),
                 pl.BlockSpec((4, DIFF_DH), lambda bi, h, i: (0, 0)),
                 pl.BlockSpec((1, w), lambda bi, h, i: (0, h))]
    args += [pc, pc, p, p, lam, head_norm]
    outs = pl.pallas_call(
        functools.partial(_diff_kernel, lam_init=lam_init, need_ctx=need_ctx, sub_rows=256, n_tiles=t // tq),
        grid=(b, DIFF_HEADS, t // tq),
        in_specs=in_specs,
        out_specs=out_specs,
        out_shape=out_shape,
        scratch_shapes=[pltpu.VMEM((nc + t, w), BF16), pltpu.VMEM((nc + t, 2 * w), BF16)],
        compiler_params=_params("parallel", "parallel", "arbitrary"),
        name="diff_mix",
    )(*args)
    return (outs[0], outs[1]) if need_ctx else (outs[0], None)


def _post_kernel(*refs, ctx_row, sub_rows, ff_chunk, final, n_cast):
    h_ref, y_ref, g2_ref, sh_ref, sc_ref, g5_ref, gain_ref, wo_ref, w1_ref, w2_ref = refs[:10]
    rest = refs[10:]
    if final:
        fn_ref, rest = rest[0], rest[1:]
    cast_in, o_ref, cast_out = rest[:n_cast], rest[n_cast], rest[n_cast + 1:]
    for src, dst in zip(cast_in, cast_out):
        dst[...] = src[...].astype(dst.dtype)
    row = pl.program_id(0) if ctx_row is None else ctx_row
    g2, g5 = _mod_row(g2_ref, row), _mod_row(g5_ref, row)
    shift, scale = _mod_row(sh_ref, row), _mod_row(sc_ref, row)
    tm = h_ref.shape[1]
    ff = w1_ref.shape[1]
    for r0 in range(0, tm, sub_rows):
        rs = slice(r0, r0 + sub_rows)
        h1 = h_ref[0, rs, :] + g2 * _dot(y_ref[0, rs, :], wo_ref[...])
        u = _norm_mod(h1, gain_ref[...], shift, scale).astype(BF16)
        acc = None
        for c0 in range(0, ff, ff_chunk):
            hidden = jnp.square(jnp.maximum(_dot(u, w1_ref[:, c0:c0 + ff_chunk]), 0.0)).astype(BF16)
            part = _dot(hidden, w2_ref[c0:c0 + ff_chunk, :])
            acc = part if acc is None else acc + part
        out = h1 + g5 * acc
        if final:
            out = out * lax.rsqrt(jnp.mean(out * out, axis=-1, keepdims=True) + NORM_EPS) * fn_ref[...]
        o_ref[0, rs, :] = out


def _post(h, y, mods, layer, gain, wo, w1, w2, *, ctx_row=None, tm, final_gain=None, cast_along=()):
    b, t, d = h.shape
    dy = y.shape[2]
    ff = w1.shape[1]
    r = mods.shape[1]
    assert t % tm == 0
    final = final_gain is not None
    n_i = t // tm
    n_steps = b * n_i

    def mod_spec(k):
        return pl.BlockSpec((1, r, d), lambda bi, i: (layer, 0, k))

    in_specs = [pl.BlockSpec((1, tm, d), lambda bi, i: (bi, i, 0)),
                pl.BlockSpec((1, tm, dy), lambda bi, i: (bi, i, 0)),
                mod_spec(2), mod_spec(3), mod_spec(4), mod_spec(5),
                pl.BlockSpec((1, d), lambda bi, i: (0, 0)),
                _resident((dy, d)), _resident((d, ff)), _resident((ff, d))]
    args = [h, y, mods, mods, mods, mods, gain.reshape(1, d), wo, w1, w2]
    if final:
        in_specs.append(pl.BlockSpec((1, d), lambda bi, i: (0, 0)))
        args.append(final_gain.reshape(1, d))
    out_specs = [pl.BlockSpec((1, tm, d), lambda bi, i: (bi, i, 0))]
    out_shape = [jax.ShapeDtypeStruct((b, t, d), F32)]
    for stacked, li in cast_along:
        _, rows, cols = stacked.shape
        assert rows % (n_steps * 2 * SUBLANES) == 0
        in_specs.append(pl.BlockSpec((1, rows // n_steps, cols), lambda bi, i, li=li: (li, bi * n_i + i, 0)))
        args.append(stacked)
        out_specs.append(pl.BlockSpec((1, rows // n_steps, cols), lambda bi, i: (0, bi * n_i + i, 0)))
        out_shape.append(jax.ShapeDtypeStruct((1, rows, cols), BF16))
    outs = pl.pallas_call(
        functools.partial(_post_kernel, ctx_row=ctx_row, sub_rows=min(tm, 512), ff_chunk=1024, final=final,
                          n_cast=len(cast_along)),
        grid=(b, n_i),
        in_specs=in_specs,
        out_specs=out_specs,
        out_shape=out_shape,
        compiler_params=_params("parallel", "parallel"),
        name="post",
    )(*args)
    return outs[0] if not cast_along else (outs[0],) + tuple(o[0] for o in outs[1:])


def _rope_tables(n_tok, head_dim):
    rows = n_tok // GRID_W
    row = jnp.repeat(jnp.arange(rows, dtype=jnp.int32), GRID_W).astype(F32)
    col = jnp.tile(jnp.arange(GRID_W, dtype=jnp.int32), rows).astype(F32)
    quarter = head_dim // 4
    inv = ROPE_BASE ** (-jnp.arange(quarter, dtype=F32) / quarter)
    ang = jnp.concatenate([row[:, None] * inv, col[:, None] * inv], axis=-1)
    cos, sin = jnp.cos(ang), jnp.sin(ang)
    return jnp.tile(cos, (1, 4)), jnp.tile(jnp.concatenate([-sin, sin], axis=-1), (1, 2))


def _mlstm_weights(w_in, gate_b):
    d = w_in.shape[0]
    nk = A_HEADS * A_DQK
    main = 2 * nk + 2 * A_HEADS * A_DV
    w = jnp.concatenate([w_in[:, :nk], w_in[:, 2 * nk:main]], axis=1).astype(BF16)
    wg = jnp.transpose(w_in[:, main:].reshape(d, 4, A_PAIRS, 2), (0, 2, 1, 3)).reshape(d, A_GATES)
    wt = jnp.concatenate([w_in[:, nk:2 * nk], wg], axis=1).T.astype(BF16)
    gb = jnp.transpose(gate_b.astype(F32).reshape(4, A_PAIRS, 2), (1, 0, 2)).reshape(A_GATES, 1)
    return w, wt, jnp.broadcast_to(gb, (A_GATES, LANES))


def kernel(x, c, ctx, c_ctx, ada_w, ada_b, norm_mix, norm_ffn, ffn_w1, ffn_w2, mlstm_w_in, mlstm_gate_b, mlstm_head_norm, mlstm_w_out, swa_w_in, swa_sink, swa_w_out, diff_w_in, diff_lambda_q1, diff_lambda_k1, diff_lambda_q2, diff_lambda_k2, diff_head_norm, diff_w_out, final_norm):
    bsz, n_tok, d = x.shape
    n_ctx = ctx.shape[1]
    depth = ada_w.shape[0]
    rows = -(-(bsz + 1) // SUBLANES) * SUBLANES
    cond = jnp.concatenate([c, c_ctx[None, :], jnp.zeros((rows - bsz - 1, d), F32)], axis=0)
    mods = _ada_table(cond, ada_w, ada_b)
    rope = _rope_tables(n_tok, SWA_DH)
    tm_lat, tm_post = min(n_tok, 1024), 512

    h, hc = x, ctx
    w1, w2 = ffn_w1[0].astype(BF16), ffn_w2[0].astype(BF16)
    for i in range(depth):
        kind, slot = i % N_MIXERS, i // N_MIXERS
        need_ctx = i < depth - 1
        if kind == 0:
            w, wt, gb = _mlstm_weights(mlstm_w_in[slot], mlstm_gate_b[slot])
            proj = functools.partial(_project_mlstm, mods=mods, layer=i, gain=norm_mix[i], w=w, wt=wt, gb=gb)
            lat = proj(h, tm=tm_lat)
            cx = proj(hc, tm=n_ctx, ctx_row=bsz)
            y, yc = _mlstm_mix(*lat, *cx, mlstm_head_norm[slot].reshape(1, -1), need_ctx)
            wo = mlstm_w_out[slot]
        else:
            proj = functools.partial(_project, mods=mods, layer=i, gain=norm_mix[i])
            if kind == 1:
                w = swa_w_in[slot].astype(BF16)
                rc = (SWA_HEADS + SWA_KV_HEADS) * SWA_DH
                qs = (SWA_HEADS * SWA_DH, SWA_DH ** -0.5 * LOG2E)
            else:
                w = diff_w_in[slot].astype(BF16)
                rc = 4 * DIFF_HEADS * DIFF_DH
                qs = (2 * DIFF_HEADS * DIFF_DH, DIFF_DH ** -0.5 * LOG2E)
            p = proj(h, w=w, tm=tm_lat, rope=rope, rope_cols=rc, qscale=qs)
            pc = proj(hc, w=w, tm=n_ctx, ctx_row=bsz, qscale=qs)
            if kind == 1:
                y, yc = _swa_mix(p, pc, swa_sink[slot], need_ctx)
                wo = swa_w_out[slot]
            else:
                lam = jnp.stack([diff_lambda_q1[slot], diff_lambda_k1[slot], diff_lambda_q2[slot], diff_lambda_k2[slot]])
                lam_init = 0.8 - 0.6 * math.exp(-0.3 * i)
                y, yc = _diff_mix(p, pc, lam.astype(F32), diff_head_norm[slot].reshape(1, -1), lam_init, need_ctx, tq=min(n_tok, 2048))
                wo = diff_w_out[slot]
        post = functools.partial(_post, mods=mods, layer=i, gain=norm_ffn[i], wo=wo.astype(BF16), w1=w1, w2=w2)
        if i + 1 < depth:
            h, w1, w2 = post(h, y, tm=tm_post, cast_along=((ffn_w1, i + 1), (ffn_w2, i + 1)))
        else:
            h = post(h, y, tm=tm_post, final_gain=final_norm)
        if need_ctx:
            hc = post(hc, yc, tm=n_ctx, ctx_row=bsz)
    return h
```

```python
import functools
import math

import jax
import jax.numpy as jnp
from jax import lax
from jax.experimental import pallas as pl
from jax.experimental.pallas import tpu as pltpu

F32 = jnp.float32
BF16 = jnp.bfloat16

LANES = 128
SUBLANES = 8
VMEM_LIMIT_BYTES = 56 * 1024 * 1024
LOG2E = math.log2(math.e)

NORM_EPS = 1e-6
ROPE_BASE = 10000.0
GRID_W = 64
N_MIXERS = 3

A_HEADS = 8
A_DQK = 64
A_DV = 128
A_CHUNK = 128
A_PAIRS = A_HEADS // 2
A_GATES = 4 * A_HEADS

SWA_HEADS = 16
SWA_KV_HEADS = 4
SWA_DH = 64
SWA_GROUP = SWA_HEADS // SWA_KV_HEADS
SWA_BLOCK = 128

DIFF_HEADS = 8
DIFF_DH = 64
DIFF_DV = 128


PROJ_ROWS = 1024
PROJ_COL_CHUNK = 512
POST_ROWS = 512
POST_FF_CHUNK = 1024
DIFF_QUERY_ROWS = 2048
DIFF_SUB_ROWS = 256
SWA_BLOCKS_PER_STEP = 2


def _params(*sem):
    return pltpu.CompilerParams(dimension_semantics=sem, vmem_limit_bytes=VMEM_LIMIT_BYTES)


def _dot(a, b):
    return jnp.dot(a, b, preferred_element_type=F32)


def _dot_nt(a, b):
    return lax.dot_general(a, b, (((1,), (1,)), ((), ())), preferred_element_type=F32)


def _norm_mod(x, gain, shift, scale):
    y = x * lax.rsqrt(jnp.mean(x * x, axis=-1, keepdims=True) + NORM_EPS) * gain
    return y * (1.0 + scale) + shift


def _mod_row(ref, row):
    return ref[0, pl.ds(row, 1), :]


def _resident(shape):
    return pl.BlockSpec(shape, lambda *_: (0,) * len(shape), pipeline_mode=pl.Buffered(1))


def _ada_kernel(c_ref, w_ref, b_ref, o_ref):
    c = c_ref[...]
    s = (c * jax.nn.sigmoid(c)).astype(BF16)
    o_ref[0] = _dot(s, w_ref[0].astype(BF16)) + b_ref[0]


def _ada_table(cond, ada_w, ada_b):
    depth, d, n = ada_w.shape
    r = cond.shape[0]
    tn = n // 4
    return pl.pallas_call(
        _ada_kernel,
        grid=(depth, n // tn),
        in_specs=[pl.BlockSpec((r, d), lambda i, j: (0, 0)),
                  pl.BlockSpec((1, d, tn), lambda i, j: (i, 0, j)),
                  pl.BlockSpec((1, 1, tn), lambda i, j: (i, 0, j))],
        out_specs=pl.BlockSpec((1, r, tn), lambda i, j: (i, 0, j)),
        out_shape=jax.ShapeDtypeStruct((depth, r, n), F32),
        compiler_params=_params("parallel", "parallel"),
        name="ada_table",
    )(cond, ada_w, ada_b.reshape(depth, 1, n))


def _rope_block(blk, cos, sin_signed):
    lane = lax.broadcasted_iota(jnp.int32, blk.shape, 1)
    first_half = (lane & 32) == 0
    partner = jnp.where(first_half, pltpu.roll(blk, LANES - 32, 1), pltpu.roll(blk, 32, 1))
    return blk * cos + partner * sin_signed


def _proj_kernel(*refs, ctx_row, n_out, rope_cols, qscale, chunk):
    if rope_cols:
        x_ref, sh_ref, sc_ref, g_ref, w_ref, cos_ref, sin_ref, o_ref = refs
    else:
        x_ref, sh_ref, sc_ref, g_ref, w_ref, o_ref = refs
    row = pl.program_id(0) if ctx_row is None else ctx_row
    u = _norm_mod(x_ref[0], g_ref[...], _mod_row(sh_ref, row), _mod_row(sc_ref, row)).astype(BF16)
    for c0 in range(0, n_out, chunk):
        acc = _dot(u, w_ref[:, c0:c0 + chunk])
        for l0 in range(0, chunk, LANES):
            col = c0 + l0
            blk = acc[:, l0:l0 + LANES]
            if col < rope_cols:
                blk = _rope_block(blk, cos_ref[...], sin_ref[...])
            if col < qscale[0]:
                blk = blk * qscale[1]
            o_ref[0, :, col:col + LANES] = blk.astype(o_ref.dtype)


def _project(h, mods, layer, gain, w, *, ctx_row=None, tm, rope=None, rope_cols=0, qscale):
    b, t, d = h.shape
    n = w.shape[1]
    r = mods.shape[1]
    chunk = PROJ_COL_CHUNK
    assert t % tm == 0 and n % chunk == 0
    in_specs = [pl.BlockSpec((1, tm, d), lambda bi, i: (bi, i, 0)),
                pl.BlockSpec((1, r, d), lambda bi, i: (layer, 0, 0)),
                pl.BlockSpec((1, r, d), lambda bi, i: (layer, 0, 1)),
                pl.BlockSpec((1, d), lambda bi, i: (0, 0)),
                _resident((d, n))]
    args = [h, mods, mods, gain.reshape(1, d), w]
    if rope_cols:
        in_specs += [pl.BlockSpec((tm, LANES), lambda bi, i: (i, 0))] * 2
        args += list(rope)
    return pl.pallas_call(
        functools.partial(_proj_kernel, ctx_row=ctx_row, n_out=n, rope_cols=rope_cols, qscale=qscale, chunk=chunk),
        grid=(b, t // tm),
        in_specs=in_specs,
        out_specs=pl.BlockSpec((1, tm, n), lambda bi, i: (bi, i, 0)),
        out_shape=jax.ShapeDtypeStruct((b, t, n), BF16),
        compiler_params=_params("parallel", "parallel"),
        name="project",
    )(*args)


def _log_sigmoid(x):
    return jnp.minimum(x, 0.0) - jnp.log1p(jnp.exp(-jnp.abs(x)))


def _lane_scan(x, op, fill, reverse):
    lane = lax.broadcasted_iota(jnp.int32, x.shape, 1)
    k = 1
    while k < LANES:
        if reverse:
            shifted, ok = pltpu.roll(x, LANES - k, 1), lane < LANES - k
        else:
            shifted, ok = pltpu.roll(x, k, 1), lane >= k
        x = op(x, jnp.where(ok, shifted, fill))
        k *= 2
    return x


def _proj_mlstm_kernel(x_ref, sh_ref, sc_ref, g_ref, w_ref, wt_ref, gb_ref, o_ref, kt_ref, gt_ref, ct_ref, ws_ref,
                       *, ctx_row, n_out, chunk):
    row = pl.program_id(0) if ctx_row is None else ctx_row
    u = _norm_mod(x_ref[0], g_ref[...], _mod_row(sh_ref, row), _mod_row(sc_ref, row)).astype(BF16)
    ut = _dot_nt(wt_ref[...], u)
    for c0 in range(0, n_out, chunk):
        o_ref[0, :, c0:c0 + chunk] = _dot(u, w_ref[:, c0:c0 + chunk])
    nk = A_HEADS * A_DQK
    row8 = lax.broadcasted_iota(jnp.int32, (A_GATES, LANES), 0) & 7
    fwd = row8 < 4
    is_cum = (row8 & 2) != 0
    for ci in range(u.shape[0] // A_CHUNK):
        cols = slice(ci * A_CHUNK, (ci + 1) * A_CHUNK)
        kt_ref[0, ci] = ut[:nk, cols] * (A_DQK ** -0.5)
        x = ut[nk:, cols] + gb_ref[...]
        lf = _log_sigmoid(x)
        cum = jnp.where(fwd, _lane_scan(lf, jnp.add, 0.0, False), _lane_scan(lf, jnp.add, 0.0, True))
        gt_ref[0, ci] = jnp.where(is_cum, cum, x)
        cum_up = pltpu.roll(cum, A_GATES - 2, 0)
        r = x - cum_up
        cmax = jnp.where(fwd, _lane_scan(r, jnp.maximum, -jnp.inf, False), _lane_scan(r, jnp.maximum, -jnp.inf, True))
        b_last = jnp.where(fwd, jnp.broadcast_to(cum_up[:, A_CHUNK - 1:A_CHUNK], cum_up.shape),
                           jnp.broadcast_to(cum_up[:, 0:1], cum_up.shape))
        a = (b_last - cum_up) + x
        g = jnp.broadcast_to(jnp.max(a, axis=-1, keepdims=True), a.shape)
        ct_ref[0, ci] = jnp.where(is_cum, pltpu.roll(b_last, 2, 0), cmax)
        ws_ref[0, ci] = jnp.where(is_cum, pltpu.roll(g, 2, 0), jnp.exp(a - g))


def _project_mlstm(h, mods, layer, gain, w, wt, gb, *, ctx_row=None, tm):
    b, t, d = h.shape
    n = w.shape[1]
    r = mods.shape[1]
    nt = wt.shape[0]
    nk = A_HEADS * A_DQK
    chunk = PROJ_COL_CHUNK
    cpt = tm // A_CHUNK
    assert t % tm == 0 and n % chunk == 0 and tm % A_CHUNK == 0
    return pl.pallas_call(
        functools.partial(_proj_mlstm_kernel, ctx_row=ctx_row, n_out=n, chunk=chunk),
        grid=(b, t // tm),
        in_specs=[pl.BlockSpec((1, tm, d), lambda bi, i: (bi, i, 0)),
                  pl.BlockSpec((1, r, d), lambda bi, i: (layer, 0, 0)),
                  pl.BlockSpec((1, r, d), lambda bi, i: (layer, 0, 1)),
                  pl.BlockSpec((1, d), lambda bi, i: (0, 0)),
                  _resident((d, n)), _resident((nt, d)),
                  pl.BlockSpec((A_GATES, LANES), lambda bi, i: (0, 0))],
        out_specs=[pl.BlockSpec((1, tm, n), lambda bi, i: (bi, i, 0)),
                   pl.BlockSpec((1, cpt, nk, A_CHUNK), lambda bi, i: (bi, i, 0, 0)),
                   ] + [pl.BlockSpec((1, cpt, A_GATES, A_CHUNK), lambda bi, i: (bi, i, 0, 0))] * 3,
        out_shape=[jax.ShapeDtypeStruct((b, t, n), F32),
                   jax.ShapeDtypeStruct((b, t // A_CHUNK, nk, A_CHUNK), F32),
                   ] + [jax.ShapeDtypeStruct((b, t // A_CHUNK, A_GATES, A_CHUNK), F32)] * 3,
        compiler_params=_params("parallel", "parallel"),
        name="project_mlstm",
    )(h, mods, mods, gain.reshape(1, d), w, wt, gb)


def _mlstm_kernel(*refs, n_ctx, n_lat, need_ctx):
    if need_ctx:
        (ql, vl, ol, ktl, gtl, ctl, wsl, qc, vc, oc, ktc, gtc, ctc, wsc, hn_ref, yl_ref, yc_ref,
         kv_scr, cbd_scr, cst_scr, g_scr, bl_scr, m0_scr, hs_scr) = refs
    else:
        (ql, vl, ol, ktl, gtl, ctl, wsl, qc, vc, ktc, gtc, ctc, wsc, hn_ref, yl_ref,
         kv_scr, cbd_scr, cst_scr, g_scr, bl_scr, m0_scr, hs_scr) = refs
        oc = yc_ref = None
    L = A_CHUNK
    n_all = n_ctx + n_lat
    ones_v = jnp.ones((L, A_DV), BF16)

    def v_ext(v2, hh):
        return jnp.concatenate([v2[:, hh * A_DV:(hh + 1) * A_DV].astype(BF16), ones_v], axis=1)

    def rows(gt, dr, hh):
        return gt[dr * 4 + hh:dr * 4 + hh + 1, :], gt[dr * 4 + 2 + hh:dr * 4 + 3 + hh, :]

    def contrib(c, ct, ws, kt, v2):
        for hh in range(2):
            vx = v_ext(v2, hh)
            kth = kt[hh * A_DQK:(hh + 1) * A_DQK, :]
            for dr in range(2):
                w, g = rows(ws, dr, hh)
                idx = c * 4 + dr * 2 + hh
                kv_scr[idx] = _dot((kth * w).astype(BF16), vx)
                g_scr[pl.ds(idx, 1), :] = g
                bl_scr[pl.ds(idx, 1), :] = rows(ct, dr, hh)[1]

    for c in range(n_ctx):
        contrib(c, ctc[0, c], wsc[0, c], ktc[0, c], vc[0, c * L:(c + 1) * L, :])

    def contrib_lat(c, carry):
        contrib(c + n_ctx, ctl[0, c], wsl[0, c], ktl[0, c], vl[0, pl.ds(pl.multiple_of(c * L, L), L), :])
        return carry

    lax.fori_loop(0, n_lat, contrib_lat, 0, unroll=16)

    cst_scr[...] = jnp.zeros_like(cst_scr)
    zpad = jnp.zeros((A_DQK, 2 * A_DV), BF16)

    def scan_step(i, ms):
        c_bwd = jnp.where(i < n_ctx, n_ctx - 1 - i, n_all - 1 - (i - n_ctx))
        new_ms = []
        for dr, c in ((0, i), (1, c_bwd)):
            c0s = [cst_scr[dr * 2 + hh] for hh in range(2)]
            for hh in range(2):
                idx = c * 4 + dr * 2 + hh
                c0b = c0s[hh].astype(BF16)
                cbd_scr[idx] = jnp.concatenate([zpad, c0b] if hh else [c0b, zpad], axis=0)
                m0 = ms[dr * 2 + hh]
                m0_scr[pl.ds(idx, 1), :] = m0
                g, b_last = g_scr[pl.ds(idx, 1), :], bl_scr[pl.ds(idx, 1), :]
                m_new = jnp.maximum(b_last + m0, g)
                decay = jnp.exp(b_last + m0 - m_new)
                inject = jnp.exp(g - m_new)
                decay, inject = (jnp.concatenate([z, z], axis=1) for z in (decay, inject))
                cst_scr[dr * 2 + hh] = decay * c0s[hh] + inject * kv_scr[idx]
                new_ms.append(m_new)
        return tuple(new_ms)

    lax.fori_loop(0, n_all, scan_step, tuple(jnp.zeros((1, L), F32) for _ in range(4)), unroll=True)

    t_i = lax.broadcasted_iota(jnp.int32, (L, L), 0)
    s_i = lax.broadcasted_iota(jnp.int32, (L, L), 1)
    masks = (s_i <= t_i, s_i >= t_i)
    zk = jnp.zeros((A_DQK, L), BF16)
    zrows = jnp.zeros((L - 2 * SUBLANES, L), F32)

    def mix(c, gt, ct, kt, q2, v2):
        colm = jnp.concatenate([gt, ct, zrows], axis=0).T
        qb = q2.astype(BF16)
        ktb = kt.astype(BF16)
        kt_bd = jnp.concatenate([jnp.concatenate([ktb[:A_DQK], zk], axis=1),
                                 jnp.concatenate([zk, ktb[A_DQK:]], axis=1)], axis=0)
        s2 = _dot(qb, kt_bd)
        vxs = [v_ext(v2, hh) for hh in range(2)]
        hsum = [None, None]
        for dr in range(2):
            for hh in range(2):
                idx = c * 4 + dr * 2 + hh
                li, cum = rows(gt, dr, hh)
                m0 = m0_scr[pl.ds(idx, 1), :]
                cmax = jnp.broadcast_to(colm[:, 8 + dr * 4 + hh:9 + dr * 4 + hh], (L, L))
                cum_t = jnp.broadcast_to(colm[:, dr * 4 + 2 + hh:dr * 4 + 3 + hh], (L, L))
                mm = jnp.maximum(cmax, m0)
                p = jnp.where(masks[dr], jnp.exp((li - cum) - mm), 0.0)
                wq = (p * s2[:, hh * L:(hh + 1) * L]).astype(BF16)
                carry = jnp.exp(m0 - mm)
                lhs = jnp.concatenate([wq, (q2 * carry).astype(BF16)], axis=1)
                ne = _dot(lhs, jnp.concatenate([vxs[hh], cbd_scr[idx]], axis=0))
                h = ne[:, :A_DV] / jnp.maximum(jnp.abs(ne[:, A_DV:]), jnp.exp(-(cum_t + mm)))
                hsum[hh] = h if dr == 0 else hsum[hh] + h
        return hsum

    def finish(hs2, o2):
        ys = []
        for hh in range(2):
            hs = hs2[:, hh * A_DV:(hh + 1) * A_DV]
            hn = hs * lax.rsqrt(jnp.mean(hs * hs, axis=-1, keepdims=True) + NORM_EPS)
            hn = hn * hn_ref[:, hh * A_DV:(hh + 1) * A_DV]
            ys.append(hn * jax.nn.sigmoid(o2[:, hh * A_DV:(hh + 1) * A_DV]))
        return jnp.concatenate(ys, axis=1).astype(BF16)

    if need_ctx:
        for c in range(n_ctx):
            sl = slice(c * L, (c + 1) * L)
            hs = mix(c, gtc[0, c], ctc[0, c], ktc[0, c], qc[0, sl, :], vc[0, sl, :])
            yc_ref[0, sl, :] = finish(jnp.concatenate(hs, axis=1), oc[0, sl, :])

    def mix_lat(c):
        src = pl.ds(c * L if isinstance(c, int) else pl.multiple_of(c * L, L), L)
        hs = mix(c + n_ctx, gtl[0, c], ctl[0, c], ktl[0, c], ql[0, src, :], vl[0, src, :])
        hs_scr[...] = jnp.concatenate(hs, axis=1)

    def finish_lat(c):
        src = pl.ds(c * L if isinstance(c, int) else pl.multiple_of(c * L, L), L)
        yl_ref[0, src, :] = finish(hs_scr[...], ol[0, src, :])

    def pipelined(c, carry):
        finish_lat(c - 1)
        mix_lat(c)
        return carry

    mix_lat(0)
    lax.fori_loop(1, n_lat, pipelined, 0, unroll=3)
    finish_lat(n_lat - 1)


def _mlstm_mix(p, kt, gt, ct, ws, pc, ktc, gtc, ctc, wsc, head_norm, need_ctx):
    b, t, _ = p.shape
    nc = pc.shape[1]
    L = A_CHUNK
    n_lat, n_ctx = t // L, nc // L
    n_all = n_lat + n_ctx
    qw, vw = 2 * A_DQK, 2 * A_DV
    v_blk = A_HEADS * A_DQK // vw
    o_blk = v_blk + A_PAIRS

    def specs(rows, nch, with_o):
        s = [pl.BlockSpec((1, rows, qw), lambda bi, hp: (bi, 0, hp)),
             pl.BlockSpec((1, rows, vw), lambda bi, hp: (bi, 0, v_blk + hp))]
        if with_o:
            s.append(pl.BlockSpec((1, rows, vw), lambda bi, hp: (bi, 0, o_blk + hp)))
        s.append(pl.BlockSpec((1, nch, qw, L), lambda bi, hp: (bi, 0, hp, 0)))
        s += [pl.BlockSpec((1, nch, SUBLANES, L), lambda bi, hp: (bi, 0, hp, 0))] * 3
        return s

    in_specs = specs(t, n_lat, True) + specs(nc, n_ctx, need_ctx)
    in_specs.append(pl.BlockSpec((1, vw), lambda bi, hp: (0, hp)))
    args = [p, p, p, kt, gt, ct, ws] + ([pc, pc, pc] if need_ctx else [pc, pc]) + [ktc, gtc, ctc, wsc, head_norm]
    out_specs = [pl.BlockSpec((1, t, vw), lambda bi, hp: (bi, 0, hp))]
    out_shape = [jax.ShapeDtypeStruct((b, t, A_HEADS * A_DV), BF16)]
    if need_ctx:
        out_specs.append(pl.BlockSpec((1, nc, vw), lambda bi, hp: (bi, 0, hp)))
        out_shape.append(jax.ShapeDtypeStruct((b, nc, A_HEADS * A_DV), BF16))
    n_rows = -(-n_all * 4 // SUBLANES) * SUBLANES
    outs = pl.pallas_call(
        functools.partial(_mlstm_kernel, n_ctx=n_ctx, n_lat=n_lat, need_ctx=need_ctx),
        grid=(b, A_PAIRS),
        in_specs=in_specs,
        out_specs=out_specs,
        out_shape=out_shape,
        scratch_shapes=[pltpu.VMEM((n_all * 4, A_DQK, vw), F32),
                        pltpu.VMEM((n_all * 4, qw, vw), BF16),
                        pltpu.VMEM((4, A_DQK, vw), F32),
                        pltpu.VMEM((n_rows, L), F32), pltpu.VMEM((n_rows, L), F32), pltpu.VMEM((n_rows, L), F32),
                        pltpu.VMEM((L, vw), F32)],
        compiler_params=_params("parallel", "parallel"),
        name="mlstm_mix",
    )(*args)
    return (outs[0], outs[1]) if need_ctx else (outs[0], None)


def _swa_attend(q4, k, v, valid, sink_col):
    s = _dot_nt(q4, k)
    if valid is not None:
        s = jnp.where(valid, s, -jnp.inf)
    m = jnp.maximum(jnp.max(s, axis=-1, keepdims=True), sink_col)
    p = jnp.exp2(s - m)
    den = jnp.sum(p, axis=-1, keepdims=True) + jnp.exp2(sink_col - m)
    return _dot(p.astype(BF16), v) / den


def _swa_heads(q, keys, vals, valid, sink_ref, o_ref, rows, row0=0):
    row_i = lax.broadcasted_iota(jnp.int32, (SWA_GROUP * rows, 1), 0)
    for hk in range(SWA_KV_HEADS):
        ks = slice(hk * SWA_DH, (hk + 1) * SWA_DH)
        k = jnp.concatenate([x[:, ks] for x in keys], axis=0)
        v = jnp.concatenate([x[:, ks] for x in vals], axis=0)
        q4 = jnp.concatenate([q[:, (hk * SWA_GROUP + g) * SWA_DH:(hk * SWA_GROUP + g + 1) * SWA_DH]
                              for g in range(SWA_GROUP)], axis=0)
        sink_col = jnp.zeros((SWA_GROUP * rows, 1), F32)
        for g in range(SWA_GROUP):
            head = hk * SWA_GROUP + g
            sink_col = jnp.where((row_i >= g * rows) & (row_i < (g + 1) * rows),
                                 sink_ref[:, head:head + 1] * LOG2E, sink_col)
        o4 = _swa_attend(q4, k, v, valid, sink_col)
        for g in range(0, SWA_GROUP, 2):
            col = (hk * SWA_GROUP + g) * SWA_DH
            pair = jnp.concatenate([o4[g * rows:(g + 1) * rows], o4[(g + 1) * rows:(g + 2) * rows]], axis=1)
            o_ref[0, row0:row0 + rows, col:col + 2 * SWA_DH] = pair.astype(o_ref.dtype)


def _swa_kernel(*refs, n_tok, nqb, need_ctx):
    if need_ctx:
        q_ref, k_ref, v_ref, kx_ref, vx_ref, sink_ref, qx_ref, o_ref, ox_ref = refs
    else:
        q_ref, k_ref, v_ref, kx_ref, vx_ref, sink_ref, o_ref = refs
    step = pl.program_id(1)
    if need_ctx:
        @pl.when(step == 0)
        def _():
            _swa_heads(qx_ref[0], [kx_ref[0]], [vx_ref[0]], None, sink_ref, ox_ref, qx_ref.shape[1])
    L = SWA_BLOCK
    nc = kx_ref.shape[1]
    qi = lax.broadcasted_iota(jnp.int32, (L, 3 * L), 0)
    ki = lax.broadcasted_iota(jnp.int32, (L, 3 * L), 1)
    ctx_valid = jnp.ones((L, nc), jnp.bool_)
    for jb in range(nqb):
        j = step * nqb + jb
        start = pl.multiple_of(jnp.clip((j - 1) * L, 0, n_tok - 3 * L), L)
        rel = (start - j * L) + ki - qi
        valid = jnp.concatenate([jnp.abs(rel) <= L, ctx_valid], axis=1)
        valid = jnp.concatenate([valid] * SWA_GROUP, axis=0)
        _swa_heads(q_ref[0, jb * L:(jb + 1) * L, :], [k_ref[0, pl.ds(start, 3 * L), :], kx_ref[0]],
                   [v_ref[0, pl.ds(start, 3 * L), :], vx_ref[0]], valid, sink_ref, o_ref, L, row0=jb * L)


def _swa_mix(p, pc, sink, need_ctx):
    b, t, _ = p.shape
    nc = pc.shape[1]
    L = SWA_BLOCK
    nblk = t // L
    nqb = SWA_BLOCKS_PER_STEP if nblk % SWA_BLOCKS_PER_STEP == 0 else 1
    assert t >= 3 * L
    qw = SWA_HEADS * SWA_DH
    kvw = SWA_KV_HEADS * SWA_DH
    kblk, vblk = qw // kvw, qw // kvw + 1
    in_specs = [pl.BlockSpec((1, nqb * L, qw), lambda bi, j: (bi, j, 0)),
                pl.BlockSpec((1, t, kvw), lambda bi, j: (bi, 0, kblk)),
                pl.BlockSpec((1, t, kvw), lambda bi, j: (bi, 0, vblk)),
                pl.BlockSpec((1, nc, kvw), lambda bi, j: (bi, 0, kblk)),
                pl.BlockSpec((1, nc, kvw), lambda bi, j: (bi, 0, vblk)),
                pl.BlockSpec((1, SWA_HEADS), lambda bi, j: (0, 0))]
    args = [p, p, p, pc, pc, sink.reshape(1, SWA_HEADS)]
    out_specs = [pl.BlockSpec((1, nqb * L, qw), lambda bi, j: (bi, j, 0))]
    out_shape = [jax.ShapeDtypeStruct((b, t, qw), BF16)]
    if need_ctx:
        in_specs.append(pl.BlockSpec((1, nc, qw), lambda bi, j: (bi, 0, 0)))
        args.append(pc)
        out_specs.append(pl.BlockSpec((1, nc, qw), lambda bi, j: (bi, 0, 0)))
        out_shape.append(jax.ShapeDtypeStruct((b, nc, qw), BF16))
    outs = pl.pallas_call(
        functools.partial(_swa_kernel, n_tok=t, nqb=nqb, need_ctx=need_ctx),
        grid=(b, nblk // nqb),
        in_specs=in_specs,
        out_specs=out_specs,
        out_shape=out_shape,
        compiler_params=_params("parallel", "arbitrary"),
        name="swa_mix",
    )(*args)
    return (outs[0], outs[1]) if need_ctx else (outs[0], None)


def _diff_rows(q, k, v, lam, hn, lam_init):
    lane = lax.broadcasted_iota(jnp.int32, q.shape, 1)
    outs = []
    for m in range(2):
        qm = jnp.where((lane >= DIFF_DH) if m else (lane < DIFF_DH), q, jnp.zeros_like(q))
        s = _dot_nt(qm, k)
        p = jnp.exp2(s - jnp.max(s, axis=-1, keepdims=True)).astype(BF16)
        ne = _dot(p, v)
        outs.append(ne[:, :DIFF_DV] / ne[:, DIFF_DV:])
    od = outs[0] - lam * outs[1]
    od = od * lax.rsqrt(jnp.mean(od * od, axis=-1, keepdims=True) + NORM_EPS)
    return od * hn * (1.0 - lam_init)


def _diff_kernel(*refs, lam_init, need_ctx, sub_rows, n_tiles):
    if need_ctx:
        q_ref, qx_ref, kx_ref, vx_ref, kl_ref, vl_ref, lam_ref, hn_ref, o_ref, ox_ref, k_scr, v_scr = refs
    else:
        q_ref, kx_ref, vx_ref, kl_ref, vl_ref, lam_ref, hn_ref, o_ref, k_scr, v_scr = refs
    nc = kx_ref.shape[1]
    lam = (jnp.exp(jnp.sum(lam_ref[0:1, :] * lam_ref[1:2, :], axis=-1, keepdims=True))
           - jnp.exp(jnp.sum(lam_ref[2:3, :] * lam_ref[3:4, :], axis=-1, keepdims=True)) + lam_init)
    hn = hn_ref[...]

    def first_tile():
        k_scr[0:nc, :] = kx_ref[0]
        v_scr[0:nc, 0:DIFF_DV] = vx_ref[0]
        k_scr[nc:, :] = kl_ref[0]
        v_scr[nc:, 0:DIFF_DV] = vl_ref[0]
        v_scr[:, DIFF_DV:] = jnp.ones((v_scr.shape[0], DIFF_DV), BF16)
        if need_ctx:
            for r0 in range(0, nc, sub_rows):
                rs = slice(r0, min(nc, r0 + sub_rows))
                ox_ref[0, rs, :] = _diff_rows(qx_ref[0, rs, :], k_scr[0:nc, :], v_scr[0:nc, :], lam, hn,
                                              lam_init).astype(ox_ref.dtype)

    if n_tiles == 1:
        first_tile()
    else:
        pl.when(pl.program_id(2) == 0)(first_tile)

    tq = q_ref.shape[1]
    for r0 in range(0, tq, sub_rows):
        rs = slice(r0, min(tq, r0 + sub_rows))
        o_ref[0, rs, :] = _diff_rows(q_ref[0, rs, :], k_scr[...], v_scr[...], lam, hn, lam_init).astype(o_ref.dtype)


def _diff_mix(p, pc, lam, head_norm, lam_init, need_ctx, tq):
    b, t, _ = p.shape
    nc = pc.shape[1]
    w = DIFF_DV
    kblk, vblk = DIFF_HEADS, 2 * DIFF_HEADS
    out_w = DIFF_HEADS * DIFF_DV

    def col_spec(rows, col):
        return pl.BlockSpec((1, rows, w), lambda bi, h, i: (bi, 0, col + h))

    in_specs = [pl.BlockSpec((1, tq, w), lambda bi, h, i: (bi, i, h))]
    args = [p]
    out_specs = [pl.BlockSpec((1, tq, w), lambda bi, h, i: (bi, i, h))]
    out_shape = [jax.ShapeDtypeStruct((b, t, out_w), BF16)]
    if need_ctx:
        in_specs.append(col_spec(nc, 0))
        args.append(pc)
        out_specs.append(col_spec(nc, 0))
        out_shape.append(jax.ShapeDtypeStruct((b, nc, out_w), BF16))
    in_specs += [col_spec(nc, kblk), col_spec(nc, vblk), col_spec(t, kblk), col_spec(t, vblk),
                 pl.BlockSpec((4, DIFF_DH), lambda bi, h, i: (0, 0)),
                 pl.BlockSpec((1, w), lambda bi, h, i: (0, h))]
    args += [pc, pc, p, p, lam, head_norm]
    outs = pl.pallas_call(
        functools.partial(_diff_kernel, lam_init=lam_init, need_ctx=need_ctx, sub_rows=DIFF_SUB_ROWS,
                          n_tiles=t // tq),
        grid=(b, DIFF_HEADS, t // tq),
        in_specs=in_specs,
        out_specs=out_specs,
        out_shape=out_shape,
        scratch_shapes=[pltpu.VMEM((nc + t, w), BF16), pltpu.VMEM((nc + t, 2 * w), BF16)],
        compiler_params=_params("parallel", "parallel", "arbitrary"),
        name="diff_mix",
    )(*args)
    return (outs[0], outs[1]) if need_ctx else (outs[0], None)


def _post_kernel(*refs, ctx_row, sub_rows, ff_chunk, final, n_cast):
    h_ref, y_ref, g2_ref, sh_ref, sc_ref, g5_ref, gain_ref, wo_ref, w1_ref, w2_ref = refs[:10]
    rest = refs[10:]
    if final:
        fn_ref, rest = rest[0], rest[1:]
    cast_in, o_ref, cast_out = rest[:n_cast], rest[n_cast], rest[n_cast + 1:]
    for src, dst in zip(cast_in, cast_out):
        dst[...] = src[...].astype(dst.dtype)
    row = pl.program_id(0) if ctx_row is None else ctx_row
    g2, g5 = _mod_row(g2_ref, row), _mod_row(g5_ref, row)
    shift, scale = _mod_row(sh_ref, row), _mod_row(sc_ref, row)
    tm = h_ref.shape[1]
    ff = w1_ref.shape[1]
    for r0 in range(0, tm, sub_rows):
        rs = slice(r0, r0 + sub_rows)
        h1 = h_ref[0, rs, :] + g2 * _dot(y_ref[0, rs, :], wo_ref[...])
        u = _norm_mod(h1, gain_ref[...], shift, scale).astype(BF16)
        acc = None
        for c0 in range(0, ff, ff_chunk):
            hidden = jnp.square(jnp.maximum(_dot(u, w1_ref[:, c0:c0 + ff_chunk]), 0.0)).astype(BF16)
            part = _dot(hidden, w2_ref[c0:c0 + ff_chunk, :])
            acc = part if acc is None else acc + part
        out = h1 + g5 * acc
        if final:
            out = out * lax.rsqrt(jnp.mean(out * out, axis=-1, keepdims=True) + NORM_EPS) * fn_ref[...]
        o_ref[0, rs, :] = out


def _post(h, y, mods, layer, gain, wo, w1, w2, *, ctx_row=None, tm, final_gain=None, cast_along=()):
    b, t, d = h.shape
    dy = y.shape[2]
    ff = w1.shape[1]
    r = mods.shape[1]
    assert t % tm == 0
    final = final_gain is not None
    n_i = t // tm
    n_steps = b * n_i

    def mod_spec(k):
        return pl.BlockSpec((1, r, d), lambda bi, i: (layer, 0, k))

    in_specs = [pl.BlockSpec((1, tm, d), lambda bi, i: (bi, i, 0)),
                pl.BlockSpec((1, tm, dy), lambda bi, i: (bi, i, 0)),
                mod_spec(2), mod_spec(3), mod_spec(4), mod_spec(5),
                pl.BlockSpec((1, d), lambda bi, i: (0, 0)),
                _resident((dy, d)), _resident((d, ff)), _resident((ff, d))]
    args = [h, y, mods, mods, mods, mods, gain.reshape(1, d), wo, w1, w2]
    if final:
        in_specs.append(pl.BlockSpec((1, d), lambda bi, i: (0, 0)))
        args.append(final_gain.reshape(1, d))
    out_specs = [pl.BlockSpec((1, tm, d), lambda bi, i: (bi, i, 0))]
    out_shape = [jax.ShapeDtypeStruct((b, t, d), F32)]
    for stacked, li in cast_along:
        _, rows, cols = stacked.shape
        assert rows % (n_steps * 2 * SUBLANES) == 0
        in_specs.append(pl.BlockSpec((1, rows // n_steps, cols), lambda bi, i, li=li: (li, bi * n_i + i, 0)))
        args.append(stacked)
        out_specs.append(pl.BlockSpec((1, rows // n_steps, cols), lambda bi, i: (0, bi * n_i + i, 0)))
        out_shape.append(jax.ShapeDtypeStruct((1, rows, cols), BF16))
    outs = pl.pallas_call(
        functools.partial(_post_kernel, ctx_row=ctx_row, sub_rows=tm, ff_chunk=POST_FF_CHUNK, final=final,
                          n_cast=len(cast_along)),
        grid=(b, n_i),
        in_specs=in_specs,
        out_specs=out_specs,
        out_shape=out_shape,
        compiler_params=_params("parallel", "parallel"),
        name="post",
    )(*args)
    return outs[0] if not cast_along else (outs[0],) + tuple(o[0] for o in outs[1:])


def _rope_tables(n_tok, head_dim):
    rows = n_tok // GRID_W
    row = jnp.repeat(jnp.arange(rows, dtype=jnp.int32), GRID_W).astype(F32)
    col = jnp.tile(jnp.arange(GRID_W, dtype=jnp.int32), rows).astype(F32)
    quarter = head_dim // 4
    inv = ROPE_BASE ** (-jnp.arange(quarter, dtype=F32) / quarter)
    ang = jnp.concatenate([row[:, None] * inv, col[:, None] * inv], axis=-1)
    cos, sin = jnp.cos(ang), jnp.sin(ang)
    return jnp.tile(cos, (1, 4)), jnp.tile(jnp.concatenate([-sin, sin], axis=-1), (1, 2))


def _mlstm_weights(w_in, gate_b):
    d = w_in.shape[0]
    nk = A_HEADS * A_DQK
    main = 2 * nk + 2 * A_HEADS * A_DV
    w = jnp.concatenate([w_in[:, :nk], w_in[:, 2 * nk:main]], axis=1).astype(BF16)
    wg = jnp.transpose(w_in[:, main:].reshape(d, 4, A_PAIRS, 2), (0, 2, 1, 3)).reshape(d, A_GATES)
    wt = jnp.concatenate([w_in[:, nk:2 * nk], wg], axis=1).T.astype(BF16)
    gb = jnp.transpose(gate_b.astype(F32).reshape(4, A_PAIRS, 2), (1, 0, 2)).reshape(A_GATES, 1)
    return w, wt, jnp.broadcast_to(gb, (A_GATES, LANES))


def kernel(x, c, ctx, c_ctx, ada_w, ada_b, norm_mix, norm_ffn, ffn_w1, ffn_w2, mlstm_w_in, mlstm_gate_b, mlstm_head_norm, mlstm_w_out, swa_w_in, swa_sink, swa_w_out, diff_w_in, diff_lambda_q1, diff_lambda_k1, diff_lambda_q2, diff_lambda_k2, diff_head_norm, diff_w_out, final_norm):
    bsz, n_tok, d = x.shape
    n_ctx = ctx.shape[1]
    depth = ada_w.shape[0]
    rows = -(-(bsz + 1) // SUBLANES) * SUBLANES
    cond = jnp.concatenate([c, c_ctx[None, :], jnp.zeros((rows - bsz - 1, d), F32)], axis=0)
    mods = _ada_table(cond, ada_w, ada_b)
    rope = _rope_tables(n_tok, SWA_DH)
    tm_lat, tm_post = min(n_tok, PROJ_ROWS), min(n_tok, POST_ROWS)

    n_cx = bsz * n_ctx
    tmc_lat = PROJ_ROWS if n_cx % PROJ_ROWS == 0 else n_ctx
    tmc_post = POST_ROWS if n_cx % POST_ROWS == 0 else n_ctx
    h, hc = x, ctx.reshape(1, n_cx, d)
    w1, w2 = ffn_w1[0].astype(BF16), ffn_w2[0].astype(BF16)
    for i in range(depth):
        kind, slot = i % N_MIXERS, i // N_MIXERS
        need_ctx = i < depth - 1
        if kind == 0:
            w, wt, gb = _mlstm_weights(mlstm_w_in[slot], mlstm_gate_b[slot])
            proj = functools.partial(_project_mlstm, mods=mods, layer=i, gain=norm_mix[i], w=w, wt=wt, gb=gb)
            lat = proj(h, tm=tm_lat)
            cx = [a.reshape((bsz, a.shape[1] // bsz) + a.shape[2:]) for a in proj(hc, tm=tmc_lat, ctx_row=bsz)]
            y, yc = _mlstm_mix(*lat, *cx, mlstm_head_norm[slot].reshape(1, -1), need_ctx)
            wo = mlstm_w_out[slot]
        else:
            proj = functools.partial(_project, mods=mods, layer=i, gain=norm_mix[i])
            if kind == 1:
                w = swa_w_in[slot].astype(BF16)
                rc = (SWA_HEADS + SWA_KV_HEADS) * SWA_DH
                qs = (SWA_HEADS * SWA_DH, SWA_DH ** -0.5 * LOG2E)
            else:
                w = diff_w_in[slot].astype(BF16)
                rc = 4 * DIFF_HEADS * DIFF_DH
                qs = (2 * DIFF_HEADS * DIFF_DH, DIFF_DH ** -0.5 * LOG2E)
            p = proj(h, w=w, tm=tm_lat, rope=rope, rope_cols=rc, qscale=qs)
            pc = proj(hc, w=w, tm=tmc_lat, ctx_row=bsz, qscale=qs).reshape(bsz, n_ctx, -1)
            if kind == 1:
                y, yc = _swa_mix(p, pc, swa_sink[slot], need_ctx)
                wo = swa_w_out[slot]
            else:
                lam = jnp.stack([diff_lambda_q1[slot], diff_lambda_k1[slot], diff_lambda_q2[slot], diff_lambda_k2[slot]])
                lam_init = 0.8 - 0.6 * math.exp(-0.3 * i)
                y, yc = _diff_mix(p, pc, lam.astype(F32), diff_head_norm[slot].reshape(1, -1), lam_init, need_ctx,
                                      tq=min(n_tok, DIFF_QUERY_ROWS))
                wo = diff_w_out[slot]
        post = functools.partial(_post, mods=mods, layer=i, gain=norm_ffn[i], wo=wo.astype(BF16), w1=w1, w2=w2)
        if i + 1 < depth:
            h, w1, w2 = post(h, y, tm=tm_post, cast_along=((ffn_w1, i + 1), (ffn_w2, i + 1)))
        else:
            h = post(h, y, tm=tm_post, final_gain=final_norm)
        if need_ctx:
            hc = post(hc, yc.reshape(1, bsz * n_ctx, -1), tm=tmc_post, ctx_row=bsz)
    return h
```

```python
import functools
import math

import jax
import jax.numpy as jnp
from jax import lax
from jax.experimental import pallas as pl
from jax.experimental.pallas import tpu as pltpu

F32 = jnp.float32
BF16 = jnp.bfloat16

LANES = 128
SUBLANES = 8
VMEM_LIMIT_BYTES = 56 * 1024 * 1024
LOG2E = math.log2(math.e)

NORM_EPS = 1e-6
ROPE_BASE = 10000.0
GRID_W = 64
N_MIXERS = 3

A_HEADS = 8
A_DQK = 64
A_DV = 128
A_CHUNK = 128
A_PAIRS = A_HEADS // 2
A_GATES = 4 * A_HEADS

SWA_HEADS = 16
SWA_KV_HEADS = 4
SWA_DH = 64
SWA_GROUP = SWA_HEADS // SWA_KV_HEADS
SWA_BLOCK = 128

DIFF_HEADS = 8
DIFF_DH = 64
DIFF_DV = 128


PROJ_ROWS = 1024
PROJ_COL_CHUNK = 512
POST_ROWS = 512
POST_FF_CHUNK = 1024
DIFF_QUERY_ROWS = 2048
DIFF_SUB_ROWS = 256
SWA_BLOCKS_PER_STEP = 2


def _params(*sem):
    return pltpu.CompilerParams(dimension_semantics=sem, vmem_limit_bytes=VMEM_LIMIT_BYTES)


def _dot(a, b):
    return jnp.dot(a, b, preferred_element_type=F32)


def _dot_nt(a, b):
    return lax.dot_general(a, b, (((1,), (1,)), ((), ())), preferred_element_type=F32)


def _norm_mod(x, gain, shift, scale):
    y = x * lax.rsqrt(jnp.mean(x * x, axis=-1, keepdims=True) + NORM_EPS) * gain
    return y * (1.0 + scale) + shift


def _mod_row(ref, row):
    return ref[0, pl.ds(row, 1), :]


def _resident(shape):
    return pl.BlockSpec(shape, lambda *_: (0,) * len(shape), pipeline_mode=pl.Buffered(1))


def _ada_kernel(c_ref, w_ref, b_ref, o_ref):
    c = c_ref[...]
    s = (c * jax.nn.sigmoid(c)).astype(BF16)
    o_ref[0] = _dot(s, w_ref[0].astype(BF16)) + b_ref[0]


def _ada_table(cond, ada_w, ada_b):
    depth, d, n = ada_w.shape
    r = cond.shape[0]
    tn = n // 4
    return pl.pallas_call(
        _ada_kernel,
        grid=(depth, n // tn),
        in_specs=[pl.BlockSpec((r, d), lambda i, j: (0, 0)),
                  pl.BlockSpec((1, d, tn), lambda i, j: (i, 0, j)),
                  pl.BlockSpec((1, 1, tn), lambda i, j: (i, 0, j))],
        out_specs=pl.BlockSpec((1, r, tn), lambda i, j: (i, 0, j)),
        out_shape=jax.ShapeDtypeStruct((depth, r, n), F32),
        compiler_params=_params("parallel", "parallel"),
        name="ada_table",
    )(cond, ada_w, ada_b.reshape(depth, 1, n))


def _rope_block(blk, cos, sin_signed):
    lane = lax.broadcasted_iota(jnp.int32, blk.shape, 1)
    first_half = (lane & 32) == 0
    partner = jnp.where(first_half, pltpu.roll(blk, LANES - 32, 1), pltpu.roll(blk, 32, 1))
    return blk * cos + partner * sin_signed


def _proj_kernel(*refs, ctx_row, n_out, rope_cols, qscale, chunk):
    if rope_cols:
        x_ref, sh_ref, sc_ref, g_ref, w_ref, cos_ref, sin_ref, o_ref = refs
    else:
        x_ref, sh_ref, sc_ref, g_ref, w_ref, o_ref = refs
    row = pl.program_id(0) if ctx_row is None else ctx_row
    u = _norm_mod(x_ref[0], g_ref[...], _mod_row(sh_ref, row), _mod_row(sc_ref, row)).astype(BF16)
    for c0 in range(0, n_out, chunk):
        acc = _dot(u, w_ref[:, c0:c0 + chunk])
        for l0 in range(0, chunk, LANES):
            col = c0 + l0
            blk = acc[:, l0:l0 + LANES]
            if col < rope_cols:
                blk = _rope_block(blk, cos_ref[...], sin_ref[...])
            if col < qscale[0]:
                blk = blk * qscale[1]
            o_ref[0, :, col:col + LANES] = blk.astype(o_ref.dtype)


def _project(h, mods, layer, gain, w, *, ctx_row=None, tm, rope=None, rope_cols=0, qscale):
    b, t, d = h.shape
    n = w.shape[1]
    r = mods.shape[1]
    chunk = PROJ_COL_CHUNK
    assert t % tm == 0 and n % chunk == 0
    in_specs = [pl.BlockSpec((1, tm, d), lambda bi, i: (bi, i, 0)),
                pl.BlockSpec((1, r, d), lambda bi, i: (layer, 0, 0)),
                pl.BlockSpec((1, r, d), lambda bi, i: (layer, 0, 1)),
                pl.BlockSpec((1, d), lambda bi, i: (0, 0)),
                _resident((d, n))]
    args = [h, mods, mods, gain.reshape(1, d), w]
    if rope_cols:
        in_specs += [pl.BlockSpec((tm, LANES), lambda bi, i: (i, 0))] * 2
        args += list(rope)
    return pl.pallas_call(
        functools.partial(_proj_kernel, ctx_row=ctx_row, n_out=n, rope_cols=rope_cols, qscale=qscale, chunk=chunk),
        grid=(b, t // tm),
        in_specs=in_specs,
        out_specs=pl.BlockSpec((1, tm, n), lambda bi, i: (bi, i, 0)),
        out_shape=jax.ShapeDtypeStruct((b, t, n), BF16),
        compiler_params=_params("parallel", "parallel"),
        name="project",
    )(*args)


def _log_sigmoid(x):
    return jnp.minimum(x, 0.0) - jnp.log1p(jnp.exp(-jnp.abs(x)))


def _lane_scan(x, op, fill, reverse):
    lane = lax.broadcasted_iota(jnp.int32, x.shape, 1)
    k = 1
    while k < LANES:
        if reverse:
            shifted, ok = pltpu.roll(x, LANES - k, 1), lane < LANES - k
        else:
            shifted, ok = pltpu.roll(x, k, 1), lane >= k
        x = op(x, jnp.where(ok, shifted, fill))
        k *= 2
    return x


def _proj_mlstm_kernel(x_ref, sh_ref, sc_ref, g_ref, w_ref, wt_ref, gb_ref, o_ref, kt_ref, gt_ref, ct_ref, ws_ref,
                       *, ctx_row, n_out, chunk):
    row = pl.program_id(0) if ctx_row is None else ctx_row
    u = _norm_mod(x_ref[0], g_ref[...], _mod_row(sh_ref, row), _mod_row(sc_ref, row)).astype(BF16)
    ut = _dot_nt(wt_ref[...], u)
    for c0 in range(0, n_out, chunk):
        o_ref[0, :, c0:c0 + chunk] = _dot(u, w_ref[:, c0:c0 + chunk])
    nk = A_HEADS * A_DQK
    row8 = lax.broadcasted_iota(jnp.int32, (A_GATES, LANES), 0) & 7
    fwd = row8 < 4
    is_cum = (row8 & 2) != 0
    for ci in range(u.shape[0] // A_CHUNK):
        cols = slice(ci * A_CHUNK, (ci + 1) * A_CHUNK)
        kt_ref[0, ci] = ut[:nk, cols] * (A_DQK ** -0.5)
        x = ut[nk:, cols] + gb_ref[...]
        lf = _log_sigmoid(x)
        cum = jnp.where(fwd, _lane_scan(lf, jnp.add, 0.0, False), _lane_scan(lf, jnp.add, 0.0, True))
        gt_ref[0, ci] = jnp.where(is_cum, cum, x)
        cum_up = pltpu.roll(cum, A_GATES - 2, 0)
        r = x - cum_up
        cmax = jnp.where(fwd, _lane_scan(r, jnp.maximum, -jnp.inf, False), _lane_scan(r, jnp.maximum, -jnp.inf, True))
        b_last = jnp.where(fwd, jnp.broadcast_to(cum_up[:, A_CHUNK - 1:A_CHUNK], cum_up.shape),
                           jnp.broadcast_to(cum_up[:, 0:1], cum_up.shape))
        a = (b_last - cum_up) + x
        g = jnp.broadcast_to(jnp.max(a, axis=-1, keepdims=True), a.shape)
        ct_ref[0, ci] = jnp.where(is_cum, pltpu.roll(b_last, 2, 0), cmax)
        ws_ref[0, ci] = jnp.where(is_cum, pltpu.roll(g, 2, 0), jnp.exp(a - g))


def _project_mlstm(h, mods, layer, gain, w, wt, gb, *, ctx_row=None, tm):
    b, t, d = h.shape
    n = w.shape[1]
    r = mods.shape[1]
    nt = wt.shape[0]
    nk = A_HEADS * A_DQK
    chunk = PROJ_COL_CHUNK
    cpt = tm // A_CHUNK
    assert t % tm == 0 and n % chunk == 0 and tm % A_CHUNK == 0
    return pl.pallas_call(
        functools.partial(_proj_mlstm_kernel, ctx_row=ctx_row, n_out=n, chunk=chunk),
        grid=(b, t // tm),
        in_specs=[pl.BlockSpec((1, tm, d), lambda bi, i: (bi, i, 0)),
                  pl.BlockSpec((1, r, d), lambda bi, i: (layer, 0, 0)),
                  pl.BlockSpec((1, r, d), lambda bi, i: (layer, 0, 1)),
                  pl.BlockSpec((1, d), lambda bi, i: (0, 0)),
                  _resident((d, n)), _resident((nt, d)),
                  pl.BlockSpec((A_GATES, LANES), lambda bi, i: (0, 0))],
        out_specs=[pl.BlockSpec((1, tm, n), lambda bi, i: (bi, i, 0)),
                   pl.BlockSpec((1, cpt, nk, A_CHUNK), lambda bi, i: (bi, i, 0, 0)),
                   ] + [pl.BlockSpec((1, cpt, A_GATES, A_CHUNK), lambda bi, i: (bi, i, 0, 0))] * 3,
        out_shape=[jax.ShapeDtypeStruct((b, t, n), F32),
                   jax.ShapeDtypeStruct((b, t // A_CHUNK, nk, A_CHUNK), F32),
                   ] + [jax.ShapeDtypeStruct((b, t // A_CHUNK, A_GATES, A_CHUNK), F32)] * 3,
        compiler_params=_params("parallel", "parallel"),
        name="project_mlstm",
    )(h, mods, mods, gain.reshape(1, d), w, wt, gb)


def _mlstm_kernel(*refs, n_ctx, n_lat, need_ctx):
    if need_ctx:
        (ql, vl, ol, ktl, gtl, ctl, wsl, qc, vc, oc, ktc, gtc, ctc, wsc, hn_ref, yl_ref, yc_ref,
         kv_scr, cbd_scr, cst_scr, g_scr, bl_scr, m0_scr, hs_scr) = refs
    else:
        (ql, vl, ol, ktl, gtl, ctl, wsl, qc, vc, ktc, gtc, ctc, wsc, hn_ref, yl_ref,
         kv_scr, cbd_scr, cst_scr, g_scr, bl_scr, m0_scr, hs_scr) = refs
        oc = yc_ref = None
    L = A_CHUNK
    n_all = n_ctx + n_lat
    ones_v = jnp.ones((L, A_DV), BF16)

    def v_ext(v2, hh):
        return jnp.concatenate([v2[:, hh * A_DV:(hh + 1) * A_DV].astype(BF16), ones_v], axis=1)

    def rows(gt, dr, hh):
        return gt[dr * 4 + hh:dr * 4 + hh + 1, :], gt[dr * 4 + 2 + hh:dr * 4 + 3 + hh, :]

    def contrib(c, ct, ws, kt, v2):
        for hh in range(2):
            vx = v_ext(v2, hh)
            kth = kt[hh * A_DQK:(hh + 1) * A_DQK, :]
            for dr in range(2):
                w, g = rows(ws, dr, hh)
                idx = c * 4 + dr * 2 + hh
                kv_scr[idx] = _dot((kth * w).astype(BF16), vx)
                g_scr[pl.ds(idx, 1), :] = g
                bl_scr[pl.ds(idx, 1), :] = rows(ct, dr, hh)[1]

    for c in range(n_ctx):
        contrib(c, ctc[0, c], wsc[0, c], ktc[0, c], vc[0, c * L:(c + 1) * L, :])

    def contrib_lat(c, carry):
        contrib(c + n_ctx, ctl[0, c], wsl[0, c], ktl[0, c], vl[0, pl.ds(pl.multiple_of(c * L, L), L), :])
        return carry

    lax.fori_loop(0, n_lat, contrib_lat, 0, unroll=16)

    cst_scr[...] = jnp.zeros_like(cst_scr)
    zpad = jnp.zeros((A_DQK, 2 * A_DV), BF16)

    def scan_step(i, ms):
        c_bwd = jnp.where(i < n_ctx, n_ctx - 1 - i, n_all - 1 - (i - n_ctx))
        new_ms = []
        for dr, c in ((0, i), (1, c_bwd)):
            c0s = [cst_scr[dr * 2 + hh] for hh in range(2)]
            for hh in range(2):
                idx = c * 4 + dr * 2 + hh
                c0b = c0s[hh].astype(BF16)
                cbd_scr[idx] = jnp.concatenate([zpad, c0b] if hh else [c0b, zpad], axis=0)
                m0 = ms[dr * 2 + hh]
                m0_scr[pl.ds(idx, 1), :] = m0
                g, b_last = g_scr[pl.ds(idx, 1), :], bl_scr[pl.ds(idx, 1), :]
                m_new = jnp.maximum(b_last + m0, g)
                decay = jnp.exp(b_last + m0 - m_new)
                inject = jnp.exp(g - m_new)
                decay, inject = (jnp.concatenate([z, z], axis=1) for z in (decay, inject))
                cst_scr[dr * 2 + hh] = decay * c0s[hh] + inject * kv_scr[idx]
                new_ms.append(m_new)
        return tuple(new_ms)

    lax.fori_loop(0, n_all, scan_step, tuple(jnp.zeros((1, L), F32) for _ in range(4)), unroll=True)

    t_i = lax.broadcasted_iota(jnp.int32, (L, L), 0)
    s_i = lax.broadcasted_iota(jnp.int32, (L, L), 1)
    masks = (s_i <= t_i, s_i >= t_i)
    zk = jnp.zeros((A_DQK, L), BF16)
    zrows = jnp.zeros((L - 2 * SUBLANES, L), F32)

    def mix(c, gt, ct, kt, q2, v2):
        colm = jnp.concatenate([gt, ct, zrows], axis=0).T
        qb = q2.astype(BF16)
        ktb = kt.astype(BF16)
        kt_bd = jnp.concatenate([jnp.concatenate([ktb[:A_DQK], zk], axis=1),
                                 jnp.concatenate([zk, ktb[A_DQK:]], axis=1)], axis=0)
        s2 = _dot(qb, kt_bd)
        vxs = [v_ext(v2, hh) for hh in range(2)]
        hsum = [None, None]
        for dr in range(2):
            for hh in range(2):
                idx = c * 4 + dr * 2 + hh
                li, cum = rows(gt, dr, hh)
                m0 = m0_scr[pl.ds(idx, 1), :]
                cmax = jnp.broadcast_to(colm[:, 8 + dr * 4 + hh:9 + dr * 4 + hh], (L, L))
                cum_t = jnp.broadcast_to(colm[:, dr * 4 + 2 + hh:dr * 4 + 3 + hh], (L, L))
                mm = jnp.maximum(cmax, m0)
                p = jnp.where(masks[dr], jnp.exp((li - cum) - mm), 0.0)
                wq = (p * s2[:, hh * L:(hh + 1) * L]).astype(BF16)
                carry = jnp.exp(m0 - mm)
                lhs = jnp.concatenate([wq, (q2 * carry).astype(BF16)], axis=1)
                ne = _dot(lhs, jnp.concatenate([vxs[hh], cbd_scr[idx]], axis=0))
                h = ne[:, :A_DV] / jnp.maximum(jnp.abs(ne[:, A_DV:]), jnp.exp(-(cum_t + mm)))
                hsum[hh] = h if dr == 0 else hsum[hh] + h
        return hsum

    def finish(hs2, o2):
        ys = []
        for hh in range(2):
            hs = hs2[:, hh * A_DV:(hh + 1) * A_DV]
            hn = hs * lax.rsqrt(jnp.mean(hs * hs, axis=-1, keepdims=True) + NORM_EPS)
            hn = hn * hn_ref[:, hh * A_DV:(hh + 1) * A_DV]
            ys.append(hn * jax.nn.sigmoid(o2[:, hh * A_DV:(hh + 1) * A_DV]))
        return jnp.concatenate(ys, axis=1).astype(BF16)

    if need_ctx:
        for c in range(n_ctx):
            sl = slice(c * L, (c + 1) * L)
            hs = mix(c, gtc[0, c], ctc[0, c], ktc[0, c], qc[0, sl, :], vc[0, sl, :])
            yc_ref[0, sl, :] = finish(jnp.concatenate(hs, axis=1), oc[0, sl, :])

    def mix_lat(c):
        src = pl.ds(c * L if isinstance(c, int) else pl.multiple_of(c * L, L), L)
        hs = mix(c + n_ctx, gtl[0, c], ctl[0, c], ktl[0, c], ql[0, src, :], vl[0, src, :])
        hs_scr[...] = jnp.concatenate(hs, axis=1)

    def finish_lat(c):
        src = pl.ds(c * L if isinstance(c, int) else pl.multiple_of(c * L, L), L)
        yl_ref[0, src, :] = finish(hs_scr[...], ol[0, src, :])

    def pipelined(c, carry):
        finish_lat(c - 1)
        mix_lat(c)
        return carry

    mix_lat(0)
    lax.fori_loop(1, n_lat, pipelined, 0, unroll=True)
    finish_lat(n_lat - 1)


def _mlstm_mix(p, kt, gt, ct, ws, pc, ktc, gtc, ctc, wsc, head_norm, need_ctx):
    b, t, _ = p.shape
    nc = pc.shape[1]
    L = A_CHUNK
    n_lat, n_ctx = t // L, nc // L
    n_all = n_lat + n_ctx
    qw, vw = 2 * A_DQK, 2 * A_DV
    v_blk = A_HEADS * A_DQK // vw
    o_blk = v_blk + A_PAIRS

    def specs(rows, nch, with_o):
        s = [pl.BlockSpec((1, rows, qw), lambda bi, hp: (bi, 0, hp)),
             pl.BlockSpec((1, rows, vw), lambda bi, hp: (bi, 0, v_blk + hp))]
        if with_o:
            s.append(pl.BlockSpec((1, rows, vw), lambda bi, hp: (bi, 0, o_blk + hp)))
        s.append(pl.BlockSpec((1, nch, qw, L), lambda bi, hp: (bi, 0, hp, 0)))
        s += [pl.BlockSpec((1, nch, SUBLANES, L), lambda bi, hp: (bi, 0, hp, 0))] * 3
        return s

    in_specs = specs(t, n_lat, True) + specs(nc, n_ctx, need_ctx)
    in_specs.append(pl.BlockSpec((1, vw), lambda bi, hp: (0, hp)))
    args = [p, p, p, kt, gt, ct, ws] + ([pc, pc, pc] if need_ctx else [pc, pc]) + [ktc, gtc, ctc, wsc, head_norm]
    out_specs = [pl.BlockSpec((1, t, vw), lambda bi, hp: (bi, 0, hp))]
    out_shape = [jax.ShapeDtypeStruct((b, t, A_HEADS * A_DV), BF16)]
    if need_ctx:
        out_specs.append(pl.BlockSpec((1, nc, vw), lambda bi, hp: (bi, 0, hp)))
        out_shape.append(jax.ShapeDtypeStruct((b, nc, A_HEADS * A_DV), BF16))
    n_rows = -(-n_all * 4 // SUBLANES) * SUBLANES
    outs = pl.pallas_call(
        functools.partial(_mlstm_kernel, n_ctx=n_ctx, n_lat=n_lat, need_ctx=need_ctx),
        grid=(b, A_PAIRS),
        in_specs=in_specs,
        out_specs=out_specs,
        out_shape=out_shape,
        scratch_shapes=[pltpu.VMEM((n_all * 4, A_DQK, vw), F32),
                        pltpu.VMEM((n_all * 4, qw, vw), BF16),
                        pltpu.VMEM((4, A_DQK, vw), F32),
                        pltpu.VMEM((n_rows, L), F32), pltpu.VMEM((n_rows, L), F32), pltpu.VMEM((n_rows, L), F32),
                        pltpu.VMEM((L, vw), F32)],
        compiler_params=_params("parallel", "parallel"),
        name="mlstm_mix",
    )(*args)
    return (outs[0], outs[1]) if need_ctx else (outs[0], None)


def _swa_attend(q4, k, v, valid, sink_col):
    s = _dot_nt(q4, k)
    if valid is not None:
        s = jnp.where(valid, s, -jnp.inf)
    m = jnp.maximum(jnp.max(s, axis=-1, keepdims=True), sink_col)
    p = jnp.exp2(s - m)
    den = jnp.sum(p, axis=-1, keepdims=True) + jnp.exp2(sink_col - m)
    return _dot(p.astype(BF16), v) / den


def _swa_heads(q, keys, vals, valid, sink_ref, o_ref, rows, row0=0):
    row_i = lax.broadcasted_iota(jnp.int32, (SWA_GROUP * rows, 1), 0)
    for hk in range(SWA_KV_HEADS):
        ks = slice(hk * SWA_DH, (hk + 1) * SWA_DH)
        k = jnp.concatenate([x[:, ks] for x in keys], axis=0)
        v = jnp.concatenate([x[:, ks] for x in vals], axis=0)
        q4 = jnp.concatenate([q[:, (hk * SWA_GROUP + g) * SWA_DH:(hk * SWA_GROUP + g + 1) * SWA_DH]
                              for g in range(SWA_GROUP)], axis=0)
        sink_col = jnp.zeros((SWA_GROUP * rows, 1), F32)
        for g in range(SWA_GROUP):
            head = hk * SWA_GROUP + g
            sink_col = jnp.where((row_i >= g * rows) & (row_i < (g + 1) * rows),
                                 sink_ref[:, head:head + 1] * LOG2E, sink_col)
        o4 = _swa_attend(q4, k, v, valid, sink_col)
        for g in range(0, SWA_GROUP, 2):
            col = (hk * SWA_GROUP + g) * SWA_DH
            pair = jnp.concatenate([o4[g * rows:(g + 1) * rows], o4[(g + 1) * rows:(g + 2) * rows]], axis=1)
            o_ref[0, row0:row0 + rows, col:col + 2 * SWA_DH] = pair.astype(o_ref.dtype)


def _swa_kernel(*refs, n_tok, nqb, need_ctx):
    if need_ctx:
        q_ref, k_ref, v_ref, kx_ref, vx_ref, sink_ref, qx_ref, o_ref, ox_ref = refs
    else:
        q_ref, k_ref, v_ref, kx_ref, vx_ref, sink_ref, o_ref = refs
    step = pl.program_id(1)
    if need_ctx:
        @pl.when(step == 0)
        def _():
            _swa_heads(qx_ref[0], [kx_ref[0]], [vx_ref[0]], None, sink_ref, ox_ref, qx_ref.shape[1])
    L = SWA_BLOCK
    nc = kx_ref.shape[1]
    qi = lax.broadcasted_iota(jnp.int32, (L, 3 * L), 0)
    ki = lax.broadcasted_iota(jnp.int32, (L, 3 * L), 1)
    ctx_valid = jnp.ones((L, nc), jnp.bool_)
    for jb in range(nqb):
        j = step * nqb + jb
        start = pl.multiple_of(jnp.clip((j - 1) * L, 0, n_tok - 3 * L), L)
        rel = (start - j * L) + ki - qi
        valid = jnp.concatenate([jnp.abs(rel) <= L, ctx_valid], axis=1)
        valid = jnp.concatenate([valid] * SWA_GROUP, axis=0)
        _swa_heads(q_ref[0, jb * L:(jb + 1) * L, :], [k_ref[0, pl.ds(start, 3 * L), :], kx_ref[0]],
                   [v_ref[0, pl.ds(start, 3 * L), :], vx_ref[0]], valid, sink_ref, o_ref, L, row0=jb * L)


def _swa_mix(p, pc, sink, need_ctx):
    b, t, _ = p.shape
    nc = pc.shape[1]
    L = SWA_BLOCK
    nblk = t // L
    nqb = SWA_BLOCKS_PER_STEP if nblk % SWA_BLOCKS_PER_STEP == 0 else 1
    assert t >= 3 * L
    qw = SWA_HEADS * SWA_DH
    kvw = SWA_KV_HEADS * SWA_DH
    kblk, vblk = qw // kvw, qw // kvw + 1
    in_specs = [pl.BlockSpec((1, nqb * L, qw), lambda bi, j: (bi, j, 0)),
                pl.BlockSpec((1, t, kvw), lambda bi, j: (bi, 0, kblk)),
                pl.BlockSpec((1, t, kvw), lambda bi, j: (bi, 0, vblk)),
                pl.BlockSpec((1, nc, kvw), lambda bi, j: (bi, 0, kblk)),
                pl.BlockSpec((1, nc, kvw), lambda bi, j: (bi, 0, vblk)),
                pl.BlockSpec((1, SWA_HEADS), lambda bi, j: (0, 0))]
    args = [p, p, p, pc, pc, sink.reshape(1, SWA_HEADS)]
    out_specs = [pl.BlockSpec((1, nqb * L, qw), lambda bi, j: (bi, j, 0))]
    out_shape = [jax.ShapeDtypeStruct((b, t, qw), BF16)]
    if need_ctx:
        in_specs.append(pl.BlockSpec((1, nc, qw), lambda bi, j: (bi, 0, 0)))
        args.append(pc)
        out_specs.append(pl.BlockSpec((1, nc, qw), lambda bi, j: (bi, 0, 0)))
        out_shape.append(jax.ShapeDtypeStruct((b, nc, qw), BF16))
    outs = pl.pallas_call(
        functools.partial(_swa_kernel, n_tok=t, nqb=nqb, need_ctx=need_ctx),
        grid=(b, nblk // nqb),
        in_specs=in_specs,
        out_specs=out_specs,
        out_shape=out_shape,
        compiler_params=_params("parallel", "arbitrary"),
        name="swa_mix",
    )(*args)
    return (outs[0], outs[1]) if need_ctx else (outs[0], None)


def _diff_rows(q, k, v, lam, hn, lam_init):
    lane = lax.broadcasted_iota(jnp.int32, q.shape, 1)
    outs = []
    for m in range(2):
        qm = jnp.where((lane >= DIFF_DH) if m else (lane < DIFF_DH), q, jnp.zeros_like(q))
        s = _dot_nt(qm, k)
        p = jnp.exp2(s - jnp.max(s, axis=-1, keepdims=True)).astype(BF16)
        ne = _dot(p, v)
        outs.append(ne[:, :DIFF_DV] / ne[:, DIFF_DV:])
    od = outs[0] - lam * outs[1]
    od = od * lax.rsqrt(jnp.mean(od * od, axis=-1, keepdims=True) + NORM_EPS)
    return od * hn * (1.0 - lam_init)


def _diff_kernel(*refs, lam_init, need_ctx, sub_rows, n_tiles):
    if need_ctx:
        q_ref, qx_ref, kx_ref, vx_ref, kl_ref, vl_ref, lam_ref, hn_ref, o_ref, ox_ref, k_scr, v_scr = refs
    else:
        q_ref, kx_ref, vx_ref, kl_ref, vl_ref, lam_ref, hn_ref, o_ref, k_scr, v_scr = refs
    nc = kx_ref.shape[1]
    lam = (jnp.exp(jnp.sum(lam_ref[0:1, :] * lam_ref[1:2, :], axis=-1, keepdims=True))
           - jnp.exp(jnp.sum(lam_ref[2:3, :] * lam_ref[3:4, :], axis=-1, keepdims=True)) + lam_init)
    hn = hn_ref[...]

    def first_tile():
        k_scr[0:nc, :] = kx_ref[0]
        v_scr[0:nc, 0:DIFF_DV] = vx_ref[0]
        k_scr[nc:, :] = kl_ref[0]
        v_scr[nc:, 0:DIFF_DV] = vl_ref[0]
        v_scr[:, DIFF_DV:] = jnp.ones((v_scr.shape[0], DIFF_DV), BF16)
        if need_ctx:
            for r0 in range(0, nc, sub_rows):
                rs = slice(r0, min(nc, r0 + sub_rows))
                ox_ref[0, rs, :] = _diff_rows(qx_ref[0, rs, :], k_scr[0:nc, :], v_scr[0:nc, :], lam, hn,
                                              lam_init).astype(ox_ref.dtype)

    if n_tiles == 1:
        first_tile()
    else:
        pl.when(pl.program_id(2) == 0)(first_tile)

    tq = q_ref.shape[1]
    for r0 in range(0, tq, sub_rows):
        rs = slice(r0, min(tq, r0 + sub_rows))
        o_ref[0, rs, :] = _diff_rows(q_ref[0, rs, :], k_scr[...], v_scr[...], lam, hn, lam_init).astype(o_ref.dtype)


def _diff_mix(p, pc, lam, head_norm, lam_init, need_ctx, tq):
    b, t, _ = p.shape
    nc = pc.shape[1]
    w = DIFF_DV
    kblk, vblk = DIFF_HEADS, 2 * DIFF_HEADS
    out_w = DIFF_HEADS * DIFF_DV

    def col_spec(rows, col):
        return pl.BlockSpec((1, rows, w), lambda bi, h, i: (bi, 0, col + h))

    in_specs = [pl.BlockSpec((1, tq, w), lambda bi, h, i: (bi, i, h))]
    args = [p]
    out_specs = [pl.BlockSpec((1, tq, w), lambda bi, h, i: (bi, i, h))]
    out_shape = [jax.ShapeDtypeStruct((b, t, out_w), BF16)]
    if need_ctx:
        in_specs.append(col_spec(nc, 0))
        args.append(pc)
        out_specs.append(col_spec(nc, 0))
        out_shape.append(jax.ShapeDtypeStruct((b, nc, out_w), BF16))
    in_specs += [col_spec(nc, kblk), col_spec(nc, vblk), col_spec(t, kblk), col_spec(t, vblk),
                 pl.BlockSpec((4, DIFF_DH), lambda bi, h, i: (0, 0)),
                 pl.BlockSpec((1, w), lambda bi, h, i: (0, h))]
    args += [pc, pc, p, p, lam, head_norm]
    outs = pl.pallas_call(
        functools.partial(_diff_kernel, lam_init=lam_init, need_ctx=need_ctx, sub_rows=DIFF_SUB_ROWS,
                          n_tiles=t // tq),
        grid=(b, DIFF_HEADS, t // tq),
        in_specs=in_specs,
        out_specs=out_specs,
        out_shape=out_shape,
        scratch_shapes=[pltpu.VMEM((nc + t, w), BF16), pltpu.VMEM((nc + t, 2 * w), BF16)],
        compiler_params=_params("parallel", "parallel", "arbitrary"),
        name="diff_mix",
    )(*args)
    return (outs[0], outs[1]) if need_ctx else (outs[0], None)


def _post_kernel(*refs, ctx_row, sub_rows, ff_chunk, final, n_cast):
    h_ref, y_ref, g2_ref, sh_ref, sc_ref, g5_ref, gain_ref, wo_ref, w1_ref, w2_ref = refs[:10]
    rest = refs[10:]
    if final:
        fn_ref, rest = rest[0], rest[1:]
    cast_in, o_ref, cast_out = rest[:n_cast], rest[n_cast], rest[n_cast + 1:]
    for src, dst in zip(cast_in, cast_out):
        dst[...] = src[...].astype(dst.dtype)
    row = pl.program_id(0) if ctx_row is None else ctx_row
    g2, g5 = _mod_row(g2_ref, row), _mod_row(g5_ref, row)
    shift, scale = _mod_row(sh_ref, row), _mod_row(sc_ref, row)
    tm = h_ref.shape[1]
    ff = w1_ref.shape[1]
    for r0 in range(0, tm, sub_rows):
        rs = slice(r0, r0 + sub_rows)
        h1 = h_ref[0, rs, :] + g2 * _dot(y_ref[0, rs, :], wo_ref[...])
        u = _norm_mod(h1, gain_ref[...], shift, scale).astype(BF16)
        acc = None
        for c0 in range(0, ff, ff_chunk):
            hidden = jnp.square(jnp.maximum(_dot(u, w1_ref[:, c0:c0 + ff_chunk]), 0.0)).astype(BF16)
            part = _dot(hidden, w2_ref[c0:c0 + ff_chunk, :])
            acc = part if acc is None else acc + part
        out = h1 + g5 * acc
        if final:
            out = out * lax.rsqrt(jnp.mean(out * out, axis=-1, keepdims=True) + NORM_EPS) * fn_ref[...]
        o_ref[0, rs, :] = out


def _post(h, y, mods, layer, gain, wo, w1, w2, *, ctx_row=None, tm, final_gain=None, cast_along=()):
    b, t, d = h.shape
    dy = y.shape[2]
    ff = w1.shape[1]
    r = mods.shape[1]
    assert t % tm == 0
    final = final_gain is not None
    n_i = t // tm
    n_steps = b * n_i

    def mod_spec(k):
        return pl.BlockSpec((1, r, d), lambda bi, i: (layer, 0, k))

    in_specs = [pl.BlockSpec((1, tm, d), lambda bi, i: (bi, i, 0)),
                pl.BlockSpec((1, tm, dy), lambda bi, i: (bi, i, 0)),
                mod_spec(2), mod_spec(3), mod_spec(4), mod_spec(5),
                pl.BlockSpec((1, d), lambda bi, i: (0, 0)),
                _resident((dy, d)), _resident((d, ff)), _resident((ff, d))]
    args = [h, y, mods, mods, mods, mods, gain.reshape(1, d), wo, w1, w2]
    if final:
        in_specs.append(pl.BlockSpec((1, d), lambda bi, i: (0, 0)))
        args.append(final_gain.reshape(1, d))
    out_specs = [pl.BlockSpec((1, tm, d), lambda bi, i: (bi, i, 0))]
    out_shape = [jax.ShapeDtypeStruct((b, t, d), F32)]
    for stacked, li in cast_along:
        _, rows, cols = stacked.shape
        assert rows % (n_steps * 2 * SUBLANES) == 0
        in_specs.append(pl.BlockSpec((1, rows // n_steps, cols), lambda bi, i, li=li: (li, bi * n_i + i, 0)))
        args.append(stacked)
        out_specs.append(pl.BlockSpec((1, rows // n_steps, cols), lambda bi, i: (0, bi * n_i + i, 0)))
        out_shape.append(jax.ShapeDtypeStruct((1, rows, cols), BF16))
    outs = pl.pallas_call(
        functools.partial(_post_kernel, ctx_row=ctx_row, sub_rows=tm, ff_chunk=POST_FF_CHUNK, final=final,
                          n_cast=len(cast_along)),
        grid=(b, n_i),
        in_specs=in_specs,
        out_specs=out_specs,
        out_shape=out_shape,
        compiler_params=_params("parallel", "parallel"),
        name="post",
    )(*args)
    return outs[0] if not cast_along else (outs[0],) + tuple(o[0] for o in outs[1:])


def _rope_tables(n_tok, head_dim):
    rows = n_tok // GRID_W
    row = jnp.repeat(jnp.arange(rows, dtype=jnp.int32), GRID_W).astype(F32)
    col = jnp.tile(jnp.arange(GRID_W, dtype=jnp.int32), rows).astype(F32)
    quarter = head_dim // 4
    inv = ROPE_BASE ** (-jnp.arange(quarter, dtype=F32) / quarter)
    ang = jnp.concatenate([row[:, None] * inv, col[:, None] * inv], axis=-1)
    cos, sin = jnp.cos(ang), jnp.sin(ang)
    return jnp.tile(cos, (1, 4)), jnp.tile(jnp.concatenate([-sin, sin], axis=-1), (1, 2))


def _mlstm_weights(w_in, gate_b):
    d = w_in.shape[0]
    nk = A_HEADS * A_DQK
    main = 2 * nk + 2 * A_HEADS * A_DV
    w = jnp.concatenate([w_in[:, :nk], w_in[:, 2 * nk:main]], axis=1).astype(BF16)
    wg = jnp.transpose(w_in[:, main:].reshape(d, 4, A_PAIRS, 2), (0, 2, 1, 3)).reshape(d, A_GATES)
    wt = jnp.concatenate([w_in[:, nk:2 * nk], wg], axis=1).T.astype(BF16)
    gb = jnp.transpose(gate_b.astype(F32).reshape(4, A_PAIRS, 2), (1, 0, 2)).reshape(A_GATES, 1)
    return w, wt, jnp.broadcast_to(gb, (A_GATES, LANES))


def kernel(x, c, ctx, c_ctx, ada_w, ada_b, norm_mix, norm_ffn, ffn_w1, ffn_w2, mlstm_w_in, mlstm_gate_b, mlstm_head_norm, mlstm_w_out, swa_w_in, swa_sink, swa_w_out, diff_w_in, diff_lambda_q1, diff_lambda_k1, diff_lambda_q2, diff_lambda_k2, diff_head_norm, diff_w_out, final_norm):
    bsz, n_tok, d = x.shape
    n_ctx = ctx.shape[1]
    depth = ada_w.shape[0]
    rows = -(-(bsz + 1) // SUBLANES) * SUBLANES
    cond = jnp.concatenate([c, c_ctx[None, :], jnp.zeros((rows - bsz - 1, d), F32)], axis=0)
    mods = _ada_table(cond, ada_w, ada_b)
    rope = _rope_tables(n_tok, SWA_DH)
    tm_lat, tm_post = min(n_tok, PROJ_ROWS), min(n_tok, POST_ROWS)

    n_cx = bsz * n_ctx
    tmc_lat = PROJ_ROWS if n_cx % PROJ_ROWS == 0 else n_ctx
    tmc_post = POST_ROWS if n_cx % POST_ROWS == 0 else n_ctx
    h, hc = x, ctx.reshape(1, n_cx, d)
    w1, w2 = ffn_w1[0].astype(BF16), ffn_w2[0].astype(BF16)
    for i in range(depth):
        kind, slot = i % N_MIXERS, i // N_MIXERS
        need_ctx = i < depth - 1
        if kind == 0:
            w, wt, gb = _mlstm_weights(mlstm_w_in[slot], mlstm_gate_b[slot])
            proj = functools.partial(_project_mlstm, mods=mods, layer=i, gain=norm_mix[i], w=w, wt=wt, gb=gb)
            lat = proj(h, tm=tm_lat)
            cx = [a.reshape((bsz, a.shape[1] // bsz) + a.shape[2:]) for a in proj(hc, tm=tmc_lat, ctx_row=bsz)]
            y, yc = _mlstm_mix(*lat, *cx, mlstm_head_norm[slot].reshape(1, -1), need_ctx)
            wo = mlstm_w_out[slot]
        else:
            proj = functools.partial(_project, mods=mods, layer=i, gain=norm_mix[i])
            if kind == 1:
                w = swa_w_in[slot].astype(BF16)
                rc = (SWA_HEADS + SWA_KV_HEADS) * SWA_DH
                qs = (SWA_HEADS * SWA_DH, SWA_DH ** -0.5 * LOG2E)
            else:
                w = diff_w_in[slot].astype(BF16)
                rc = 4 * DIFF_HEADS * DIFF_DH
                qs = (2 * DIFF_HEADS * DIFF_DH, DIFF_DH ** -0.5 * LOG2E)
            p = proj(h, w=w, tm=tm_lat, rope=rope, rope_cols=rc, qscale=qs)
            pc = proj(hc, w=w, tm=tmc_lat, ctx_row=bsz, qscale=qs).reshape(bsz, n_ctx, -1)
            if kind == 1:
                y, yc = _swa_mix(p, pc, swa_sink[slot], need_ctx)
                wo = swa_w_out[slot]
            else:
                lam = jnp.stack([diff_lambda_q1[slot], diff_lambda_k1[slot], diff_lambda_q2[slot], diff_lambda_k2[slot]])
                lam_init = 0.8 - 0.6 * math.exp(-0.3 * i)
                y, yc = _diff_mix(p, pc, lam.astype(F32), diff_head_norm[slot].reshape(1, -1), lam_init, need_ctx,
                                      tq=min(n_tok, DIFF_QUERY_ROWS))
                wo = diff_w_out[slot]
        post = functools.partial(_post, mods=mods, layer=i, gain=norm_ffn[i], wo=wo.astype(BF16), w1=w1, w2=w2)
        if i + 1 < depth:
            h, w1, w2 = post(h, y, tm=tm_post, cast_along=((ffn_w1, i + 1), (ffn_w2, i + 1)))
        else:
            h = post(h, y, tm=tm_post, final_gain=final_norm)
        if need_ctx:
            hc = post(hc, yc.reshape(1, bsz * n_ctx, -1), tm=tmc_post, ctx_row=bsz)
    return h
```

```python
import functools
import math

import jax
import jax.numpy as jnp
from jax import lax
from jax.experimental import pallas as pl
from jax.experimental.pallas import tpu as pltpu

F32 = jnp.float32
BF16 = jnp.bfloat16

LANES = 128
SUBLANES = 8
VMEM_LIMIT_BYTES = 56 * 1024 * 1024
LOG2E = math.log2(math.e)

NORM_EPS = 1e-6
ROPE_BASE = 10000.0
GRID_W = 64
N_MIXERS = 3

A_HEADS = 8
A_DQK = 64
A_DV = 128
A_CHUNK = 128
A_PAIRS = A_HEADS // 2
A_GATES = 4 * A_HEADS

SWA_HEADS = 16
SWA_KV_HEADS = 4
SWA_DH = 64
SWA_GROUP = SWA_HEADS // SWA_KV_HEADS
SWA_BLOCK = 128

DIFF_HEADS = 8
DIFF_DH = 64
DIFF_DV = 128


PROJ_ROWS = 1024
PROJ_COL_CHUNK = 512
POST_ROWS = 512
POST_FF_CHUNK = 1024
DIFF_QUERY_ROWS = 2048
DIFF_SUB_ROWS = 256
SWA_BLOCKS_PER_STEP = 2


def _params(*sem):
    return pltpu.CompilerParams(dimension_semantics=sem, vmem_limit_bytes=VMEM_LIMIT_BYTES)


def _dot(a, b):
    return jnp.dot(a, b, preferred_element_type=F32)


def _dot_nt(a, b):
    return lax.dot_general(a, b, (((1,), (1,)), ((), ())), preferred_element_type=F32)


def _norm_mod(x, gain, shift, scale):
    y = x * lax.rsqrt(jnp.mean(x * x, axis=-1, keepdims=True) + NORM_EPS) * gain
    return y * (1.0 + scale) + shift


def _mod_row(ref, row):
    return ref[0, pl.ds(row, 1), :]


def _resident(shape):
    return pl.BlockSpec(shape, lambda *_: (0,) * len(shape), pipeline_mode=pl.Buffered(1))


def _ada_kernel(c_ref, w_ref, b_ref, o_ref):
    c = c_ref[...]
    s = (c * jax.nn.sigmoid(c)).astype(BF16)
    o_ref[0] = _dot(s, w_ref[0].astype(BF16)) + b_ref[0]


def _ada_table(cond, ada_w, ada_b):
    depth, d, n = ada_w.shape
    r = cond.shape[0]
    tn = n // 4
    return pl.pallas_call(
        _ada_kernel,
        grid=(depth, n // tn),
        in_specs=[pl.BlockSpec((r, d), lambda i, j: (0, 0)),
                  pl.BlockSpec((1, d, tn), lambda i, j: (i, 0, j)),
                  pl.BlockSpec((1, 1, tn), lambda i, j: (i, 0, j))],
        out_specs=pl.BlockSpec((1, r, tn), lambda i, j: (i, 0, j)),
        out_shape=jax.ShapeDtypeStruct((depth, r, n), F32),
        compiler_params=_params("parallel", "parallel"),
        name="ada_table",
    )(cond, ada_w, ada_b.reshape(depth, 1, n))


def _rope_block(blk, cos, sin_signed):
    lane = lax.broadcasted_iota(jnp.int32, blk.shape, 1)
    first_half = (lane & 32) == 0
    partner = jnp.where(first_half, pltpu.roll(blk, LANES - 32, 1), pltpu.roll(blk, 32, 1))
    return blk * cos + partner * sin_signed


def _proj_kernel(*refs, ctx_row, n_out, rope_cols, qscale, chunk):
    if rope_cols:
        x_ref, sh_ref, sc_ref, g_ref, w_ref, cos_ref, sin_ref, o_ref = refs
    else:
        x_ref, sh_ref, sc_ref, g_ref, w_ref, o_ref = refs
    row = pl.program_id(0) if ctx_row is None else ctx_row
    u = _norm_mod(x_ref[0], g_ref[...], _mod_row(sh_ref, row), _mod_row(sc_ref, row)).astype(BF16)
    for c0 in range(0, n_out, chunk):
        acc = _dot(u, w_ref[:, c0:c0 + chunk])
        for l0 in range(0, chunk, LANES):
            col = c0 + l0
            blk = acc[:, l0:l0 + LANES]
            if col < rope_cols:
                blk = _rope_block(blk, cos_ref[...], sin_ref[...])
            if col < qscale[0]:
                blk = blk * qscale[1]
            o_ref[0, :, col:col + LANES] = blk.astype(o_ref.dtype)


def _project(h, mods, layer, gain, w, *, ctx_row=None, tm, rope=None, rope_cols=0, qscale):
    b, t, d = h.shape
    n = w.shape[1]
    r = mods.shape[1]
    chunk = PROJ_COL_CHUNK
    assert t % tm == 0 and n % chunk == 0
    in_specs = [pl.BlockSpec((1, tm, d), lambda bi, i: (bi, i, 0)),
                pl.BlockSpec((1, r, d), lambda bi, i: (layer, 0, 0)),
                pl.BlockSpec((1, r, d), lambda bi, i: (layer, 0, 1)),
                pl.BlockSpec((1, d), lambda bi, i: (0, 0)),
                _resident((d, n))]
    args = [h, mods, mods, gain.reshape(1, d), w]
    if rope_cols:
        in_specs += [pl.BlockSpec((tm, LANES), lambda bi, i: (i, 0))] * 2
        args += list(rope)
    return pl.pallas_call(
        functools.partial(_proj_kernel, ctx_row=ctx_row, n_out=n, rope_cols=rope_cols, qscale=qscale, chunk=chunk),
        grid=(b, t // tm),
        in_specs=in_specs,
        out_specs=pl.BlockSpec((1, tm, n), lambda bi, i: (bi, i, 0)),
        out_shape=jax.ShapeDtypeStruct((b, t, n), BF16),
        compiler_params=_params("parallel", "parallel"),
        name="project",
    )(*args)


def _log_sigmoid(x):
    return jnp.minimum(x, 0.0) - jnp.log1p(jnp.exp(-jnp.abs(x)))


def _lane_scan(x, op, fill, reverse):
    lane = lax.broadcasted_iota(jnp.int32, x.shape, 1)
    k = 1
    while k < LANES:
        if reverse:
            shifted, ok = pltpu.roll(x, LANES - k, 1), lane < LANES - k
        else:
            shifted, ok = pltpu.roll(x, k, 1), lane >= k
        x = op(x, jnp.where(ok, shifted, fill))
        k *= 2
    return x


def _proj_mlstm_kernel(x_ref, sh_ref, sc_ref, g_ref, w_ref, wt_ref, gb_ref, o_ref, kt_ref, gt_ref, ct_ref, ws_ref,
                       *, ctx_row, n_out, chunk):
    row = pl.program_id(0) if ctx_row is None else ctx_row
    u = _norm_mod(x_ref[0], g_ref[...], _mod_row(sh_ref, row), _mod_row(sc_ref, row)).astype(BF16)
    ut = _dot_nt(wt_ref[...], u)
    for c0 in range(0, n_out, chunk):
        o_ref[0, :, c0:c0 + chunk] = _dot(u, w_ref[:, c0:c0 + chunk])
    nk = A_HEADS * A_DQK
    row8 = lax.broadcasted_iota(jnp.int32, (A_GATES, LANES), 0) & 7
    fwd = row8 < 4
    is_cum = (row8 & 2) != 0
    for ci in range(u.shape[0] // A_CHUNK):
        cols = slice(ci * A_CHUNK, (ci + 1) * A_CHUNK)
        kt_ref[0, ci] = ut[:nk, cols] * (A_DQK ** -0.5)
        x = ut[nk:, cols] + gb_ref[...]
        lf = _log_sigmoid(x)
        cum = jnp.where(fwd, _lane_scan(lf, jnp.add, 0.0, False), _lane_scan(lf, jnp.add, 0.0, True))
        gt_ref[0, ci] = jnp.where(is_cum, cum, x)
        cum_up = pltpu.roll(cum, A_GATES - 2, 0)
        r = x - cum_up
        cmax = jnp.where(fwd, _lane_scan(r, jnp.maximum, -jnp.inf, False), _lane_scan(r, jnp.maximum, -jnp.inf, True))
        b_last = jnp.where(fwd, jnp.broadcast_to(cum_up[:, A_CHUNK - 1:A_CHUNK], cum_up.shape),
                           jnp.broadcast_to(cum_up[:, 0:1], cum_up.shape))
        a = (b_last - cum_up) + x
        g = jnp.broadcast_to(jnp.max(a, axis=-1, keepdims=True), a.shape)
        ct_ref[0, ci] = jnp.where(is_cum, pltpu.roll(b_last, 2, 0), cmax)
        ws_ref[0, ci] = jnp.where(is_cum, pltpu.roll(g, 2, 0), jnp.exp(a - g))


def _project_mlstm(h, mods, layer, gain, w, wt, gb, *, ctx_row=None, tm):
    b, t, d = h.shape
    n = w.shape[1]
    r = mods.shape[1]
    nt = wt.shape[0]
    nk = A_HEADS * A_DQK
    chunk = PROJ_COL_CHUNK
    cpt = tm // A_CHUNK
    assert t % tm == 0 and n % chunk == 0 and tm % A_CHUNK == 0
    return pl.pallas_call(
        functools.partial(_proj_mlstm_kernel, ctx_row=ctx_row, n_out=n, chunk=chunk),
        grid=(b, t // tm),
        in_specs=[pl.BlockSpec((1, tm, d), lambda bi, i: (bi, i, 0)),
                  pl.BlockSpec((1, r, d), lambda bi, i: (layer, 0, 0)),
                  pl.BlockSpec((1, r, d), lambda bi, i: (layer, 0, 1)),
                  pl.BlockSpec((1, d), lambda bi, i: (0, 0)),
                  _resident((d, n)), _resident((nt, d)),
                  pl.BlockSpec((A_GATES, LANES), lambda bi, i: (0, 0))],
        out_specs=[pl.BlockSpec((1, tm, n), lambda bi, i: (bi, i, 0)),
                   pl.BlockSpec((1, cpt, nk, A_CHUNK), lambda bi, i: (bi, i, 0, 0)),
                   ] + [pl.BlockSpec((1, cpt, A_GATES, A_CHUNK), lambda bi, i: (bi, i, 0, 0))] * 3,
        out_shape=[jax.ShapeDtypeStruct((b, t, n), F32),
                   jax.ShapeDtypeStruct((b, t // A_CHUNK, nk, A_CHUNK), F32),
                   ] + [jax.ShapeDtypeStruct((b, t // A_CHUNK, A_GATES, A_CHUNK), F32)] * 3,
        compiler_params=_params("parallel", "parallel"),
        name="project_mlstm",
    )(h, mods, mods, gain.reshape(1, d), w, wt, gb)


def _mlstm_kernel(*refs, n_ctx, n_lat, need_ctx):
    if need_ctx:
        (ql, vl, ol, ktl, gtl, ctl, wsl, qc, vc, oc, ktc, gtc, ctc, wsc, hn_ref, yl_ref, yc_ref,
         kv_scr, cbd_scr, cst_scr, g_scr, bl_scr, m0_scr, hs_scr) = refs
    else:
        (ql, vl, ol, ktl, gtl, ctl, wsl, qc, vc, ktc, gtc, ctc, wsc, hn_ref, yl_ref,
         kv_scr, cbd_scr, cst_scr, g_scr, bl_scr, m0_scr, hs_scr) = refs
        oc = yc_ref = None
    L = A_CHUNK
    n_all = n_ctx + n_lat
    ones_v = jnp.ones((L, A_DV), BF16)

    def v_ext(v2, hh):
        return jnp.concatenate([v2[:, hh * A_DV:(hh + 1) * A_DV].astype(BF16), ones_v], axis=1)

    def rows(gt, dr, hh):
        return gt[dr * 4 + hh:dr * 4 + hh + 1, :], gt[dr * 4 + 2 + hh:dr * 4 + 3 + hh, :]

    def contrib(c, ct, ws, kt, v2):
        for hh in range(2):
            vx = v_ext(v2, hh)
            kth = kt[hh * A_DQK:(hh + 1) * A_DQK, :]
            for dr in range(2):
                w, g = rows(ws, dr, hh)
                idx = c * 4 + dr * 2 + hh
                kv_scr[idx] = _dot((kth * w).astype(BF16), vx)
                g_scr[pl.ds(idx, 1), :] = g
                bl_scr[pl.ds(idx, 1), :] = rows(ct, dr, hh)[1]

    for c in range(n_ctx):
        contrib(c, ctc[0, c], wsc[0, c], ktc[0, c], vc[0, c * L:(c + 1) * L, :])

    def contrib_lat(c, carry):
        contrib(c + n_ctx, ctl[0, c], wsl[0, c], ktl[0, c], vl[0, pl.ds(pl.multiple_of(c * L, L), L), :])
        return carry

    lax.fori_loop(0, n_lat, contrib_lat, 0, unroll=16)

    cst_scr[...] = jnp.zeros_like(cst_scr)
    zpad = jnp.zeros((A_DQK, 2 * A_DV), BF16)

    def scan_step(i, ms):
        c_bwd = jnp.where(i < n_ctx, n_ctx - 1 - i, n_all - 1 - (i - n_ctx))
        new_ms = []
        for dr, c in ((0, i), (1, c_bwd)):
            c0s = [cst_scr[dr * 2 + hh] for hh in range(2)]
            for hh in range(2):
                idx = c * 4 + dr * 2 + hh
                c0b = c0s[hh].astype(BF16)
                cbd_scr[idx] = jnp.concatenate([zpad, c0b] if hh else [c0b, zpad], axis=0)
                m0 = ms[dr * 2 + hh]
                m0_scr[pl.ds(idx, 1), :] = m0
                g, b_last = g_scr[pl.ds(idx, 1), :], bl_scr[pl.ds(idx, 1), :]
                m_new = jnp.maximum(b_last + m0, g)
                decay = jnp.exp(b_last + m0 - m_new)
                inject = jnp.exp(g - m_new)
                decay, inject = (jnp.concatenate([z, z], axis=1) for z in (decay, inject))
                cst_scr[dr * 2 + hh] = decay * c0s[hh] + inject * kv_scr[idx]
                new_ms.append(m_new)
        return tuple(new_ms)

    lax.fori_loop(0, n_all, scan_step, tuple(jnp.zeros((1, L), F32) for _ in range(4)), unroll=True)

    t_i = lax.broadcasted_iota(jnp.int32, (L, L), 0)
    s_i = lax.broadcasted_iota(jnp.int32, (L, L), 1)
    masks = (s_i <= t_i, s_i >= t_i)
    zk = jnp.zeros((A_DQK, L), BF16)
    zrows = jnp.zeros((L - 2 * SUBLANES, L), F32)

    def mix(c, gt, ct, kt, q2, v2):
        colm = jnp.concatenate([gt, ct, zrows], axis=0).T
        qb = q2.astype(BF16)
        ktb = kt.astype(BF16)
        kt_bd = jnp.concatenate([jnp.concatenate([ktb[:A_DQK], zk], axis=1),
                                 jnp.concatenate([zk, ktb[A_DQK:]], axis=1)], axis=0)
        s2 = _dot(qb, kt_bd)
        vxs = [v_ext(v2, hh) for hh in range(2)]
        hsum = [None, None]
        for dr in range(2):
            for hh in range(2):
                idx = c * 4 + dr * 2 + hh
                li, cum = rows(gt, dr, hh)
                m0 = m0_scr[pl.ds(idx, 1), :]
                cmax = jnp.broadcast_to(colm[:, 8 + dr * 4 + hh:9 + dr * 4 + hh], (L, L))
                cum_t = jnp.broadcast_to(colm[:, dr * 4 + 2 + hh:dr * 4 + 3 + hh], (L, L))
                mm = jnp.maximum(cmax, m0)
                p = jnp.where(masks[dr], jnp.exp((li - cum) - mm), 0.0)
                wq = (p * s2[:, hh * L:(hh + 1) * L]).astype(BF16)
                carry = jnp.exp(m0 - mm)
                lhs = jnp.concatenate([wq, (q2 * carry).astype(BF16)], axis=1)
                ne = _dot(lhs, jnp.concatenate([vxs[hh], cbd_scr[idx]], axis=0))
                h = ne[:, :A_DV] / jnp.maximum(jnp.abs(ne[:, A_DV:]), jnp.exp(-(cum_t + mm)))
                hsum[hh] = h if dr == 0 else hsum[hh] + h
        return hsum

    def finish(hs2, o2):
        ys = []
        for hh in range(2):
            hs = hs2[:, hh * A_DV:(hh + 1) * A_DV]
            hn = hs * lax.rsqrt(jnp.mean(hs * hs, axis=-1, keepdims=True) + NORM_EPS)
            hn = hn * hn_ref[:, hh * A_DV:(hh + 1) * A_DV]
            ys.append(hn * jax.nn.sigmoid(o2[:, hh * A_DV:(hh + 1) * A_DV]))
        return jnp.concatenate(ys, axis=1).astype(BF16)

    if need_ctx:
        for c in range(n_ctx):
            sl = slice(c * L, (c + 1) * L)
            hs = mix(c, gtc[0, c], ctc[0, c], ktc[0, c], qc[0, sl, :], vc[0, sl, :])
            yc_ref[0, sl, :] = finish(jnp.concatenate(hs, axis=1), oc[0, sl, :])

    def mix_lat(c):
        src = pl.ds(c * L if isinstance(c, int) else pl.multiple_of(c * L, L), L)
        hs = mix(c + n_ctx, gtl[0, c], ctl[0, c], ktl[0, c], ql[0, src, :], vl[0, src, :])
        hs_scr[...] = jnp.concatenate(hs, axis=1)

    def finish_lat(c):
        src = pl.ds(c * L if isinstance(c, int) else pl.multiple_of(c * L, L), L)
        yl_ref[0, src, :] = finish(hs_scr[...], ol[0, src, :])

    def pipelined(c, carry):
        finish_lat(c - 1)
        mix_lat(c)
        return carry

    mix_lat(0)
    lax.fori_loop(1, n_lat, pipelined, 0, unroll=True)
    finish_lat(n_lat - 1)


def _mlstm_mix(p, kt, gt, ct, ws, pc, ktc, gtc, ctc, wsc, head_norm, need_ctx):
    b, t, _ = p.shape
    nc = pc.shape[1]
    L = A_CHUNK
    n_lat, n_ctx = t // L, nc // L
    n_all = n_lat + n_ctx
    qw, vw = 2 * A_DQK, 2 * A_DV
    v_blk = A_HEADS * A_DQK // vw
    o_blk = v_blk + A_PAIRS

    def specs(rows, nch, with_o):
        s = [pl.BlockSpec((1, rows, qw), lambda bi, hp: (bi, 0, hp)),
             pl.BlockSpec((1, rows, vw), lambda bi, hp: (bi, 0, v_blk + hp))]
        if with_o:
            s.append(pl.BlockSpec((1, rows, vw), lambda bi, hp: (bi, 0, o_blk + hp)))
        s.append(pl.BlockSpec((1, nch, qw, L), lambda bi, hp: (bi, 0, hp, 0)))
        s += [pl.BlockSpec((1, nch, SUBLANES, L), lambda bi, hp: (bi, 0, hp, 0))] * 3
        return s

    in_specs = specs(t, n_lat, True) + specs(nc, n_ctx, need_ctx)
    in_specs.append(pl.BlockSpec((1, vw), lambda bi, hp: (0, hp)))
    args = [p, p, p, kt, gt, ct, ws] + ([pc, pc, pc] if need_ctx else [pc, pc]) + [ktc, gtc, ctc, wsc, head_norm]
    out_specs = [pl.BlockSpec((1, t, vw), lambda bi, hp: (bi, 0, hp))]
    out_shape = [jax.ShapeDtypeStruct((b, t, A_HEADS * A_DV), BF16)]
    if need_ctx:
        out_specs.append(pl.BlockSpec((1, nc, vw), lambda bi, hp: (bi, 0, hp)))
        out_shape.append(jax.ShapeDtypeStruct((b, nc, A_HEADS * A_DV), BF16))
    n_rows = -(-n_all * 4 // SUBLANES) * SUBLANES
    outs = pl.pallas_call(
        functools.partial(_mlstm_kernel, n_ctx=n_ctx, n_lat=n_lat, need_ctx=need_ctx),
        grid=(b, A_PAIRS),
        in_specs=in_specs,
        out_specs=out_specs,
        out_shape=out_shape,
        scratch_shapes=[pltpu.VMEM((n_all * 4, A_DQK, vw), F32),
                        pltpu.VMEM((n_all * 4, qw, vw), BF16),
                        pltpu.VMEM((4, A_DQK, vw), F32),
                        pltpu.VMEM((n_rows, L), F32), pltpu.VMEM((n_rows, L), F32), pltpu.VMEM((n_rows, L), F32),
                        pltpu.VMEM((L, vw), F32)],
        compiler_params=_params("parallel", "parallel"),
        name="mlstm_mix",
    )(*args)
    return (outs[0], outs[1]) if need_ctx else (outs[0], None)


def _swa_attend(q4, k, v, valid, sink_col):
    s = _dot_nt(q4, k)
    if valid is not None:
        s = jnp.where(valid, s, -jnp.inf)
    m = jnp.maximum(jnp.max(s, axis=-1, keepdims=True), sink_col)
    p = jnp.exp2(s - m)
    den = jnp.sum(p, axis=-1, keepdims=True) + jnp.exp2(sink_col - m)
    return _dot(p.astype(BF16), v) / den


def _swa_heads(q, keys, vals, valid, sink_ref, o_ref, rows, row0=0):
    row_i = lax.broadcasted_iota(jnp.int32, (SWA_GROUP * rows, 1), 0)
    for hk in range(SWA_KV_HEADS):
        ks = slice(hk * SWA_DH, (hk + 1) * SWA_DH)
        k = jnp.concatenate([x[:, ks] for x in keys], axis=0)
        v = jnp.concatenate([x[:, ks] for x in vals], axis=0)
        q4 = jnp.concatenate([q[:, (hk * SWA_GROUP + g) * SWA_DH:(hk * SWA_GROUP + g + 1) * SWA_DH]
                              for g in range(SWA_GROUP)], axis=0)
        sink_col = jnp.zeros((SWA_GROUP * rows, 1), F32)
        for g in range(SWA_GROUP):
            head = hk * SWA_GROUP + g
            sink_col = jnp.where((row_i >= g * rows) & (row_i < (g + 1) * rows),
                                 sink_ref[:, head:head + 1] * LOG2E, sink_col)
        o4 = _swa_attend(q4, k, v, valid, sink_col)
        for g in range(0, SWA_GROUP, 2):
            col = (hk * SWA_GROUP + g) * SWA_DH
            pair = jnp.concatenate([o4[g * rows:(g + 1) * rows], o4[(g + 1) * rows:(g + 2) * rows]], axis=1)
            o_ref[0, row0:row0 + rows, col:col + 2 * SWA_DH] = pair.astype(o_ref.dtype)


def _swa_kernel(*refs, n_tok, nqb, need_ctx):
    nkb = nqb + 2
    q_ref, k_refs, v_refs = refs[0], refs[1:1 + nkb], refs[1 + nkb:1 + 2 * nkb]
    if need_ctx:
        kx_ref, vx_ref, sink_ref, qx_ref, o_ref, ox_ref = refs[1 + 2 * nkb:]
    else:
        kx_ref, vx_ref, sink_ref, o_ref = refs[1 + 2 * nkb:]
    step = pl.program_id(1)
    if need_ctx:
        @pl.when(step == 0)
        def _():
            _swa_heads(qx_ref[0], [kx_ref[0]], [vx_ref[0]], None, sink_ref, ox_ref, qx_ref.shape[1])
    L = SWA_BLOCK
    nc = kx_ref.shape[1]
    qi = lax.broadcasted_iota(jnp.int32, (L, 3 * L), 0)
    ki = lax.broadcasted_iota(jnp.int32, (L, 3 * L), 1)
    ctx_valid = jnp.ones((L, nc), jnp.bool_)
    for jb in range(nqb):
        j = step * nqb + jb
        k_lo = (1 - j) * L
        valid = (jnp.abs(ki - L - qi) <= L) & (ki >= k_lo) & (ki < k_lo + n_tok)
        valid = jnp.concatenate([valid, ctx_valid], axis=1)
        valid = jnp.concatenate([valid] * SWA_GROUP, axis=0)
        _swa_heads(q_ref[0, jb * L:(jb + 1) * L, :], [r[0] for r in k_refs[jb:jb + 3]] + [kx_ref[0]],
                   [r[0] for r in v_refs[jb:jb + 3]] + [vx_ref[0]], valid, sink_ref, o_ref, L, row0=jb * L)


def _swa_mix(p, pc, sink, need_ctx):
    b, t, _ = p.shape
    nc = pc.shape[1]
    L = SWA_BLOCK
    nblk = t // L
    nqb = SWA_BLOCKS_PER_STEP if nblk % SWA_BLOCKS_PER_STEP == 0 else 1
    assert t >= 3 * L
    qw = SWA_HEADS * SWA_DH
    kvw = SWA_KV_HEADS * SWA_DH
    kblk, vblk = qw // kvw, qw // kvw + 1
    def kv_spec(col, shift):
        return pl.BlockSpec((1, L, kvw), lambda bi, j: (bi, jnp.clip(j * nqb + shift, 0, nblk - 1), col))

    shifts = range(-1, nqb + 1)
    in_specs = ([pl.BlockSpec((1, nqb * L, qw), lambda bi, j: (bi, j, 0))]
                + [kv_spec(kblk, s) for s in shifts] + [kv_spec(vblk, s) for s in shifts]
                + [pl.BlockSpec((1, nc, kvw), lambda bi, j: (bi, 0, kblk)),
                   pl.BlockSpec((1, nc, kvw), lambda bi, j: (bi, 0, vblk)),
                   pl.BlockSpec((1, SWA_HEADS), lambda bi, j: (0, 0))])
    args = [p] * (1 + 2 * (nqb + 2)) + [pc, pc, sink.reshape(1, SWA_HEADS)]
    out_specs = [pl.BlockSpec((1, nqb * L, qw), lambda bi, j: (bi, j, 0))]
    out_shape = [jax.ShapeDtypeStruct((b, t, qw), BF16)]
    if need_ctx:
        in_specs.append(pl.BlockSpec((1, nc, qw), lambda bi, j: (bi, 0, 0)))
        args.append(pc)
        out_specs.append(pl.BlockSpec((1, nc, qw), lambda bi, j: (bi, 0, 0)))
        out_shape.append(jax.ShapeDtypeStruct((b, nc, qw), BF16))
    outs = pl.pallas_call(
        functools.partial(_swa_kernel, n_tok=t, nqb=nqb, need_ctx=need_ctx),
        grid=(b, nblk // nqb),
        in_specs=in_specs,
        out_specs=out_specs,
        out_shape=out_shape,
        compiler_params=_params("parallel", "arbitrary"),
        name="swa_mix",
    )(*args)
    return (outs[0], outs[1]) if need_ctx else (outs[0], None)


def _diff_rows(q, k, v, lam, hn, lam_init):
    lane = lax.broadcasted_iota(jnp.int32, q.shape, 1)
    outs = []
    for m in range(2):
        qm = jnp.where((lane >= DIFF_DH) if m else (lane < DIFF_DH), q, jnp.zeros_like(q))
        s = _dot_nt(qm, k)
        p = jnp.exp2(s - jnp.max(s, axis=-1, keepdims=True)).astype(BF16)
        ne = _dot(p, v)
        outs.append(ne[:, :DIFF_DV] / ne[:, DIFF_DV:])
    od = outs[0] - lam * outs[1]
    od = od * lax.rsqrt(jnp.mean(od * od, axis=-1, keepdims=True) + NORM_EPS)
    return od * hn * (1.0 - lam_init)


def _diff_kernel(*refs, lam_init, need_ctx, sub_rows, n_tiles):
    if need_ctx:
        q_ref, qx_ref, kx_ref, vx_ref, kl_ref, vl_ref, lam_ref, hn_ref, o_ref, ox_ref, k_scr, v_scr = refs
    else:
        q_ref, kx_ref, vx_ref, kl_ref, vl_ref, lam_ref, hn_ref, o_ref, k_scr, v_scr = refs
    nc = kx_ref.shape[1]
    lam = (jnp.exp(jnp.sum(lam_ref[0:1, :] * lam_ref[1:2, :], axis=-1, keepdims=True))
           - jnp.exp(jnp.sum(lam_ref[2:3, :] * lam_ref[3:4, :], axis=-1, keepdims=True)) + lam_init)
    hn = hn_ref[...]

    def first_tile():
        k_scr[0:nc, :] = kx_ref[0]
        v_scr[0:nc, 0:DIFF_DV] = vx_ref[0]
        k_scr[nc:, :] = kl_ref[0]
        v_scr[nc:, 0:DIFF_DV] = vl_ref[0]
        v_scr[:, DIFF_DV:] = jnp.ones((v_scr.shape[0], DIFF_DV), BF16)
        if need_ctx:
            for r0 in range(0, nc, sub_rows):
                rs = slice(r0, min(nc, r0 + sub_rows))
                ox_ref[0, rs, :] = _diff_rows(qx_ref[0, rs, :], k_scr[0:nc, :], v_scr[0:nc, :], lam, hn,
                                              lam_init).astype(ox_ref.dtype)

    if n_tiles == 1:
        first_tile()
    else:
        pl.when(pl.program_id(2) == 0)(first_tile)

    tq = q_ref.shape[1]
    for r0 in range(0, tq, sub_rows):
        rs = slice(r0, min(tq, r0 + sub_rows))
        o_ref[0, rs, :] = _diff_rows(q_ref[0, rs, :], k_scr[...], v_scr[...], lam, hn, lam_init).astype(o_ref.dtype)


def _diff_mix(p, pc, lam, head_norm, lam_init, need_ctx, tq):
    b, t, _ = p.shape
    nc = pc.shape[1]
    w = DIFF_DV
    kblk, vblk = DIFF_HEADS, 2 * DIFF_HEADS
    out_w = DIFF_HEADS * DIFF_DV

    def col_spec(rows, col):
        return pl.BlockSpec((1, rows, w), lambda bi, h, i: (bi, 0, col + h))

    in_specs = [pl.BlockSpec((1, tq, w), lambda bi, h, i: (bi, i, h))]
    args = [p]
    out_specs = [pl.BlockSpec((1, tq, w), lambda bi, h, i: (bi, i, h))]
    out_shape = [jax.ShapeDtypeStruct((b, t, out_w), BF16)]
    if need_ctx:
        in_specs.append(col_spec(nc, 0))
        args.append(pc)
        out_specs.append(col_spec(nc, 0))
        out_shape.append(jax.ShapeDtypeStruct((b, nc, out_w), BF16))
    in_specs += [col_spec(nc, kblk), col_spec(nc, vblk), col_spec(t, kblk), col_spec(t, vblk),
                 pl.BlockSpec((4, DIFF_DH), lambda bi, h, i: (0, 0)),
                 pl.BlockSpec((1, w), lambda bi, h, i: (0, h))]
    args += [pc, pc, p, p, lam, head_norm]
    outs = pl.pallas_call(
        functools.partial(_diff_kernel, lam_init=lam_init, need_ctx=need_ctx, sub_rows=DIFF_SUB_ROWS,
                          n_tiles=t // tq),
        grid=(b, DIFF_HEADS, t // tq),
        in_specs=in_specs,
        out_specs=out_specs,
        out_shape=out_shape,
        scratch_shapes=[pltpu.VMEM((nc + t, w), BF16), pltpu.VMEM((nc + t, 2 * w), BF16)],
        compiler_params=_params("parallel", "parallel", "arbitrary"),
        name="diff_mix",
    )(*args)
    return (outs[0], outs[1]) if need_ctx else (outs[0], None)


def _post_kernel(*refs, ctx_row, sub_rows, ff_chunk, final, n_cast):
    h_ref, y_ref, g2_ref, sh_ref, sc_ref, g5_ref, gain_ref, wo_ref, w1_ref, w2_ref = refs[:10]
    rest = refs[10:]
    if final:
        fn_ref, rest = rest[0], rest[1:]
    cast_in, o_ref, cast_out = rest[:n_cast], rest[n_cast], rest[n_cast + 1:]
    for src, dst in zip(cast_in, cast_out):
        dst[...] = src[...].astype(dst.dtype)
    row = pl.program_id(0) if ctx_row is None else ctx_row
    g2, g5 = _mod_row(g2_ref, row), _mod_row(g5_ref, row)
    shift, scale = _mod_row(sh_ref, row), _mod_row(sc_ref, row)
    tm = h_ref.shape[1]
    ff = w1_ref.shape[1]
    for r0 in range(0, tm, sub_rows):
        rs = slice(r0, r0 + sub_rows)
        h1 = h_ref[0, rs, :] + g2 * _dot(y_ref[0, rs, :], wo_ref[...])
        u = _norm_mod(h1, gain_ref[...], shift, scale).astype(BF16)
        acc = None
        for c0 in range(0, ff, ff_chunk):
            hidden = jnp.square(jnp.maximum(_dot(u, w1_ref[:, c0:c0 + ff_chunk]), 0.0)).astype(BF16)
            part = _dot(hidden, w2_ref[c0:c0 + ff_chunk, :])
            acc = part if acc is None else acc + part
        out = h1 + g5 * acc
        if final:
            out = out * lax.rsqrt(jnp.mean(out * out, axis=-1, keepdims=True) + NORM_EPS) * fn_ref[...]
        o_ref[0, rs, :] = out


def _post(h, y, mods, layer, gain, wo, w1, w2, *, ctx_row=None, tm, final_gain=None, cast_along=()):
    b, t, d = h.shape
    dy = y.shape[2]
    ff = w1.shape[1]
    r = mods.shape[1]
    assert t % tm == 0
    final = final_gain is not None
    n_i = t // tm
    n_steps = b * n_i

    def mod_spec(k):
        return pl.BlockSpec((1, r, d), lambda bi, i: (layer, 0, k))

    in_specs = [pl.BlockSpec((1, tm, d), lambda bi, i: (bi, i, 0)),
                pl.BlockSpec((1, tm, dy), lambda bi, i: (bi, i, 0)),
                mod_spec(2), mod_spec(3), mod_spec(4), mod_spec(5),
                pl.BlockSpec((1, d), lambda bi, i: (0, 0)),
                _resident((dy, d)), _resident((d, ff)), _resident((ff, d))]
    args = [h, y, mods, mods, mods, mods, gain.reshape(1, d), wo, w1, w2]
    if final:
        in_specs.append(pl.BlockSpec((1, d), lambda bi, i: (0, 0)))
        args.append(final_gain.reshape(1, d))
    out_specs = [pl.BlockSpec((1, tm, d), lambda bi, i: (bi, i, 0))]
    out_shape = [jax.ShapeDtypeStruct((b, t, d), F32)]
    for stacked, li in cast_along:
        _, rows, cols = stacked.shape
        assert rows % (n_steps * 2 * SUBLANES) == 0
        in_specs.append(pl.BlockSpec((1, rows // n_steps, cols), lambda bi, i, li=li: (li, bi * n_i + i, 0)))
        args.append(stacked)
        out_specs.append(pl.BlockSpec((1, rows // n_steps, cols), lambda bi, i: (0, bi * n_i + i, 0)))
        out_shape.append(jax.ShapeDtypeStruct((1, rows, cols), BF16))
    outs = pl.pallas_call(
        functools.partial(_post_kernel, ctx_row=ctx_row, sub_rows=tm, ff_chunk=POST_FF_CHUNK, final=final,
                          n_cast=len(cast_along)),
        grid=(b, n_i),
        in_specs=in_specs,
        out_specs=out_specs,
        out_shape=out_shape,
        compiler_params=_params("parallel", "parallel"),
        name="post",
    )(*args)
    return outs[0] if not cast_along else (outs[0],) + tuple(o[0] for o in outs[1:])


def _rope_tables(n_tok, head_dim):
    rows = n_tok // GRID_W
    row = jnp.repeat(jnp.arange(rows, dtype=jnp.int32), GRID_W).astype(F32)
    col = jnp.tile(jnp.arange(GRID_W, dtype=jnp.int32), rows).astype(F32)
    quarter = head_dim // 4
    inv = ROPE_BASE ** (-jnp.arange(quarter, dtype=F32) / quarter)
    ang = jnp.concatenate([row[:, None] * inv, col[:, None] * inv], axis=-1)
    cos, sin = jnp.cos(ang), jnp.sin(ang)
    return jnp.tile(cos, (1, 4)), jnp.tile(jnp.concatenate([-sin, sin], axis=-1), (1, 2))


def _mlstm_weights(w_in, gate_b):
    d = w_in.shape[0]
    nk = A_HEADS * A_DQK
    main = 2 * nk + 2 * A_HEADS * A_DV
    w = jnp.concatenate([w_in[:, :nk], w_in[:, 2 * nk:main]], axis=1).astype(BF16)
    wg = jnp.transpose(w_in[:, main:].reshape(d, 4, A_PAIRS, 2), (0, 2, 1, 3)).reshape(d, A_GATES)
    wt = jnp.concatenate([w_in[:, nk:2 * nk], wg], axis=1).T.astype(BF16)
    gb = jnp.transpose(gate_b.astype(F32).reshape(4, A_PAIRS, 2), (1, 0, 2)).reshape(A_GATES, 1)
    return w, wt, jnp.broadcast_to(gb, (A_GATES, LANES))


def kernel(x, c, ctx, c_ctx, ada_w, ada_b, norm_mix, norm_ffn, ffn_w1, ffn_w2, mlstm_w_in, mlstm_gate_b, mlstm_head_norm, mlstm_w_out, swa_w_in, swa_sink, swa_w_out, diff_w_in, diff_lambda_q1, diff_lambda_k1, diff_lambda_q2, diff_lambda_k2, diff_head_norm, diff_w_out, final_norm):
    bsz, n_tok, d = x.shape
    n_ctx = ctx.shape[1]
    depth = ada_w.shape[0]
    rows = -(-(bsz + 1) // SUBLANES) * SUBLANES
    cond = jnp.concatenate([c, c_ctx[None, :], jnp.zeros((rows - bsz - 1, d), F32)], axis=0)
    mods = _ada_table(cond, ada_w, ada_b)
    rope = _rope_tables(n_tok, SWA_DH)
    tm_lat, tm_post = min(n_tok, PROJ_ROWS), min(n_tok, POST_ROWS)

    n_cx = bsz * n_ctx
    tmc_lat = PROJ_ROWS if n_cx % PROJ_ROWS == 0 else n_ctx
    tmc_post = POST_ROWS if n_cx % POST_ROWS == 0 else n_ctx
    h, hc = x, ctx.reshape(1, n_cx, d)
    w1, w2 = ffn_w1[0].astype(BF16), ffn_w2[0].astype(BF16)
    for i in range(depth):
        kind, slot = i % N_MIXERS, i // N_MIXERS
        need_ctx = i < depth - 1
        if kind == 0:
            w, wt, gb = _mlstm_weights(mlstm_w_in[slot], mlstm_gate_b[slot])
            proj = functools.partial(_project_mlstm, mods=mods, layer=i, gain=norm_mix[i], w=w, wt=wt, gb=gb)
            lat = proj(h, tm=tm_lat)
            cx = [a.reshape((bsz, a.shape[1] // bsz) + a.shape[2:]) for a in proj(hc, tm=tmc_lat, ctx_row=bsz)]
            y, yc = _mlstm_mix(*lat, *cx, mlstm_head_norm[slot].reshape(1, -1), need_ctx)
            wo = mlstm_w_out[slot]
        else:
            proj = functools.partial(_project, mods=mods, layer=i, gain=norm_mix[i])
            if kind == 1:
                w = swa_w_in[slot].astype(BF16)
                rc = (SWA_HEADS + SWA_KV_HEADS) * SWA_DH
                qs = (SWA_HEADS * SWA_DH, SWA_DH ** -0.5 * LOG2E)
            else:
                w = diff_w_in[slot].astype(BF16)
                rc = 4 * DIFF_HEADS * DIFF_DH
                qs = (2 * DIFF_HEADS * DIFF_DH, DIFF_DH ** -0.5 * LOG2E)
            p = proj(h, w=w, tm=tm_lat, rope=rope, rope_cols=rc, qscale=qs)
            pc = proj(hc, w=w, tm=tmc_lat, ctx_row=bsz, qscale=qs).reshape(bsz, n_ctx, -1)
            if kind == 1:
                y, yc = _swa_mix(p, pc, swa_sink[slot], need_ctx)
                wo = swa_w_out[slot]
            else:
                lam = jnp.stack([diff_lambda_q1[slot], diff_lambda_k1[slot], diff_lambda_q2[slot], diff_lambda_k2[slot]])
                lam_init = 0.8 - 0.6 * math.exp(-0.3 * i)
                y, yc = _diff_mix(p, pc, lam.astype(F32), diff_head_norm[slot].reshape(1, -1), lam_init, need_ctx,
                                      tq=min(n_tok, DIFF_QUERY_ROWS))
                wo = diff_w_out[slot]
        post = functools.partial(_post, mods=mods, layer=i, gain=norm_ffn[i], wo=wo.astype(BF16), w1=w1, w2=w2)
        if i + 1 < depth:
            h, w1, w2 = post(h, y, tm=tm_post, cast_along=((ffn_w1, i + 1), (ffn_w2, i + 1)))
        else:
            h = post(h, y, tm=tm_post, final_gain=final_norm)
        if need_ctx:
            hc = post(hc, yc.reshape(1, bsz * n_ctx, -1), tm=tmc_post, ctx_row=bsz)
    return h
```
